```python
import math
import jax, jax.numpy as jnp
from jax import lax
import numpy as np

D_MODEL = 2048
BATCH = 1
SEQ = 16384
DEPTH = 2

S5_WIDTH = D_MODEL // 4
S5_GROUP = 16
S5_GROUPS = S5_WIDTH // S5_GROUP
S5_STATE = 64
S5_DT_MIN = 1e-3
S5_DT_MAX = 1e-1
POOL_WIDTH = D_MODEL // 4
POOL_WINDOWS = (2, 4, 8, 16)
POOL_NGROUPS = len(POOL_WINDOWS)
POOL_GROUP = POOL_WIDTH // POOL_NGROUPS
GLA_HEADS = 4
GLA_V_WIDTH = D_MODEL // 2
GLA_K_WIDTH = GLA_V_WIDTH // 2
GLA_DK = GLA_K_WIDTH // GLA_HEADS
GLA_DV = GLA_V_WIDTH // GLA_HEADS
GLA_GATE_RANK = 16
GLA_TAU = 16.0
GLA_CHUNK = 64
MIX_WIDTH = S5_WIDTH + POOL_WIDTH + GLA_V_WIDTH
IN_SPLITS = (S5_WIDTH, POOL_WIDTH, GLA_K_WIDTH, GLA_K_WIDTH, GLA_V_WIDTH, GLA_GATE_RANK, GLA_V_WIDTH)
IN_WIDTH = sum(IN_SPLITS)
D_FF = 5632
CONV_WIDTH = 3
PLE_DIM = 256
EPS = 1e-6

kernel_name = "hymba_s5_pool_gla_hybrid"


def rmsnorm(x, w):
    xf = x.astype(jnp.float32)
    y = xf * lax.rsqrt(jnp.mean(xf * xf, axis=-1, keepdims=True) + EPS) * w.astype(jnp.float32)
    return y.astype(x.dtype)


def _linear_recurrence_op(e1, e2):
    a1, b1 = e1
    a2, b2 = e2
    return a1 * a2, a2 * b1 + b2


def s5_mixer(u, a_re, a_im, log_dt, b_re, b_im, c_re, c_im, d_skip, w_glu, b_glu):
    bsz, L, _ = u.shape
    uf = u.astype(jnp.float32).reshape(bsz, L, S5_GROUPS, S5_GROUP)
    A = lax.complex(a_re.astype(jnp.float32), a_im.astype(jnp.float32))
    dt = jnp.exp(log_dt.astype(jnp.float32))[:, None]
    A_bar = jnp.exp(A * dt)
    Bm = lax.complex(b_re.astype(jnp.float32), b_im.astype(jnp.float32))
    B_bar = ((A_bar - 1.0) / A)[..., None] * Bm
    Bu = jnp.einsum('blgc,gnc->blgn', uf.astype(jnp.complex64), B_bar)
    a = jnp.broadcast_to(A_bar, Bu.shape)
    _, states = lax.associative_scan(_linear_recurrence_op, (a, Bu), axis=1)
    Cm = lax.complex(c_re.astype(jnp.float32), c_im.astype(jnp.float32))
    y = jnp.einsum('blgn,gcn->blgc', states, Cm).real
    y = y + d_skip.astype(jnp.float32).reshape(S5_GROUPS, S5_GROUP) * uf
    y = jax.nn.gelu(y.reshape(bsz, L, S5_WIDTH))
    glu = jax.nn.sigmoid(y @ w_glu.astype(jnp.float32) + b_glu.astype(jnp.float32))
    return (y * glu).astype(u.dtype)


def pool_mixer(z, w_pool, pool_scale):
    bsz, L, _ = z.shape
    zf = z.astype(jnp.float32).reshape(bsz, L, POOL_NGROUPS, POOL_GROUP)
    csum = lax.cumsum(zf, axis=1)
    cpad = jnp.pad(csum, ((0, 0), (1, 0), (0, 0), (0, 0)))
    pos = jnp.arange(1, L + 1, dtype=jnp.float32)[None, :, None]
    outs = []
    for gi, w in enumerate(POOL_WINDOWS):
        hi = cpad[:, 1:, gi]
        lo = jnp.pad(cpad[:, :L + 1 - w, gi], ((0, 0), (w - 1, 0), (0, 0)))
        count = jnp.minimum(pos, float(w))
        outs.append((hi - lo) / count)
    pooled = jnp.stack(outs, axis=2) - zf
    mixed = jnp.einsum('blgc,gcd->blgd', pooled, w_pool.astype(jnp.float32))
    mixed = mixed.reshape(bsz, L, POOL_WIDTH) * pool_scale.astype(jnp.float32)
    return mixed.astype(z.dtype)


def gla_chunked(q, k, v, g):
    bsz, L, H, dk = q.shape
    dv = v.shape[-1]
    n = L // GLA_CHUNK

    def to_chunks(t):
        return t.reshape(bsz, n, GLA_CHUNK, H, t.shape[-1]).transpose(0, 3, 1, 2, 4)

    q, k, v, g = to_chunks(q) * (dk ** -0.5), to_chunks(k), to_chunks(v), to_chunks(g)
    b = jnp.cumsum(g, axis=3)
    b_last = b[:, :, :, -1:, :]
    q_dec = q * jnp.exp(b)
    k_dec = k * jnp.exp(-b)
    scores = jnp.einsum('bhntd,bhnsd->bhnts', q_dec, k_dec)
    causal = jnp.tril(jnp.ones((GLA_CHUNK, GLA_CHUNK), dtype=bool))
    scores = jnp.where(causal, scores, 0.0)
    o_intra = jnp.einsum('bhnts,bhnsv->bhntv', scores, v)
    kv_chunk = jnp.einsum('bhnsd,bhnsv->bhndv', k * jnp.exp(b_last - b), v)
    chunk_decay = jnp.exp(b_last[:, :, :, 0, :])

    def step(S, inp):
        dec, kv_n = inp
        return dec[..., None] * S + kv_n, S

    S0 = jnp.zeros((bsz, H, dk, dv), dtype=jnp.float32)
    _, S_prev = lax.scan(step, S0, (chunk_decay.transpose(2, 0, 1, 3), kv_chunk.transpose(2, 0, 1, 3, 4)))
    S_prev = S_prev.transpose(1, 2, 0, 3, 4)
    o_inter = jnp.einsum('bhntd,bhndv->bhntv', q_dec, S_prev)
    o = o_intra + o_inter
    return o.transpose(0, 2, 3, 1, 4).reshape(bsz, L, H, dv)


def gla_mixer(q, k, v, g_lr, r, w_a2, b_a, norm_w):
    bsz, L, _ = q.shape
    f32 = jnp.float32
    qh = q.astype(f32).reshape(bsz, L, GLA_HEADS, GLA_DK)
    kh = k.astype(f32).reshape(bsz, L, GLA_HEADS, GLA_DK)
    vh = v.astype(f32).reshape(bsz, L, GLA_HEADS, GLA_DV)
    logit = g_lr.astype(f32) @ w_a2.astype(f32) + b_a.astype(f32)
    log_alpha = (jax.nn.log_sigmoid(logit) / GLA_TAU).reshape(bsz, L, GLA_HEADS, GLA_DK)
    o = gla_chunked(qh, kh, vh, log_alpha)
    o = rmsnorm(o, norm_w).reshape(bsz, L, GLA_V_WIDTH)
    return (o * jax.nn.silu(r.astype(f32))).astype(q.dtype)


def conv_glu_ffn(a, w_up, conv_w, conv_b, w_down):
    L = a.shape[1]
    up = a @ w_up
    gate, val = up[..., :D_FF], up[..., D_FF:]
    gp = jnp.pad(gate, ((0, 0), (CONV_WIDTH - 1, 0), (0, 0)))
    gc = conv_b
    for j in range(CONV_WIDTH):
        gc = gc + gp[:, j:j + L] * conv_w[j]
    return (jax.nn.silu(gc) * val) @ w_down


def setup_inputs(seed: int = 0) -> dict:
    key = jax.random.key(seed)
    ks = jax.random.split(key, 32)
    f32 = jnp.float32

    def nrm(k, shape, scale):
        return jax.random.normal(k, shape, f32) * scale

    def gain(k, shape):
        return 1.0 + 0.02 * jax.random.normal(k, shape, f32)

    n_idx = jnp.arange(S5_STATE, dtype=f32)
    a_re = -0.5 + 0.01 * jax.random.normal(ks[4], (DEPTH, S5_GROUPS, S5_STATE), f32)
    a_im = math.pi * n_idx + 0.01 * jax.random.normal(ks[5], (DEPTH, S5_GROUPS, S5_STATE), f32)
    log_dt = jax.random.uniform(ks[6], (DEPTH, S5_GROUPS), f32, math.log(S5_DT_MIN), math.log(S5_DT_MAX))
    return {
        "x": nrm(ks[0], (BATCH, SEQ, D_MODEL), 1.0),
        "p": nrm(ks[1], (DEPTH, BATCH, SEQ, PLE_DIM), 1.0),
        "norm_mix_w": gain(ks[2], (DEPTH, D_MODEL)),
        "w_in": nrm(ks[3], (DEPTH, D_MODEL, IN_WIDTH), D_MODEL ** -0.5),
        "s5_a_re": a_re,
        "s5_a_im": a_im,
        "s5_log_dt": log_dt,
        "s5_b_re": nrm(ks[7], (DEPTH, S5_GROUPS, S5_STATE, S5_GROUP), (2 * S5_GROUP) ** -0.5),
        "s5_b_im": nrm(ks[8], (DEPTH, S5_GROUPS, S5_STATE, S5_GROUP), (2 * S5_GROUP) ** -0.5),
        "s5_c_re": nrm(ks[9], (DEPTH, S5_GROUPS, S5_GROUP, S5_STATE), S5_STATE ** -0.5),
        "s5_c_im": nrm(ks[10], (DEPTH, S5_GROUPS, S5_GROUP, S5_STATE), S5_STATE ** -0.5),
        "s5_d": nrm(ks[11], (DEPTH, S5_WIDTH), 1.0),
        "s5_w_glu": nrm(ks[12], (DEPTH, S5_WIDTH, S5_WIDTH), S5_WIDTH ** -0.5),
        "s5_b_glu": nrm(ks[13], (DEPTH, S5_WIDTH), 0.02),
        "pool_w": nrm(ks[14], (DEPTH, POOL_NGROUPS, POOL_GROUP, POOL_GROUP), POOL_GROUP ** -0.5),
        "pool_scale": gain(ks[15], (DEPTH, POOL_WIDTH)),
        "gla_w_a2": nrm(ks[16], (DEPTH, GLA_GATE_RANK, GLA_K_WIDTH), GLA_GATE_RANK ** -0.5),
        "gla_b_a": nrm(ks[17], (DEPTH, GLA_K_WIDTH), 0.1),
        "gla_norm_w": gain(ks[18], (DEPTH, GLA_DV)),
        "w_out": nrm(ks[19], (DEPTH, MIX_WIDTH, D_MODEL), MIX_WIDTH ** -0.5),
        "norm_ffn_w": gain(ks[20], (DEPTH, D_MODEL)),
        "w_up": nrm(ks[21], (DEPTH, D_MODEL, 2 * D_FF), D_MODEL ** -0.5),
        "conv_w": nrm(ks[22], (DEPTH, CONV_WIDTH, D_FF), CONV_WIDTH ** -0.5),
        "conv_b": nrm(ks[23], (DEPTH, D_FF), 0.02),
        "w_down": nrm(ks[24], (DEPTH, D_FF, D_MODEL), D_FF ** -0.5),
        "norm_ple_w": gain(ks[25], (DEPTH, D_MODEL)),
        "w_ple": nrm(ks[26], (DEPTH, PLE_DIM, D_MODEL), PLE_DIM ** -0.5),
        "w_pg": nrm(ks[27], (DEPTH, D_MODEL, D_MODEL), D_MODEL ** -0.5),
        "final_norm_w": gain(ks[28], (D_MODEL,)),
    }


def reference(x, p, norm_mix_w, w_in, s5_a_re, s5_a_im, s5_log_dt, s5_b_re, s5_b_im, s5_c_re, s5_c_im,
              s5_d, s5_w_glu, s5_b_glu, pool_w, pool_scale, gla_w_a2, gla_b_a, gla_norm_w, w_out,
              norm_ffn_w, w_up, conv_w, conv_b, w_down, norm_ple_w, w_ple, w_pg, final_norm_w):
    h = x
    split_at = list(np.cumsum(IN_SPLITS)[:-1])
    for i in range(DEPTH):
        a = rmsnorm(h, norm_mix_w[i])
        z = a @ w_in[i]
        u_s5, z_pool, q, k, v, g_lr, r = jnp.split(z, split_at, axis=-1)
        y_s5 = s5_mixer(u_s5, s5_a_re[i], s5_a_im[i], s5_log_dt[i], s5_b_re[i], s5_b_im[i],
                        s5_c_re[i], s5_c_im[i], s5_d[i], s5_w_glu[i], s5_b_glu[i])
        y_pool = pool_mixer(z_pool, pool_w[i], pool_scale[i])
        y_gla = gla_mixer(q, k, v, g_lr, r, gla_w_a2[i], gla_b_a[i], gla_norm_w[i])
        mixed = jnp.concatenate([y_s5, y_pool, y_gla], axis=-1)
        h = h + (mixed @ w_out[i]).astype(h.dtype)
        f = conv_glu_ffn(rmsnorm(h, norm_ffn_w[i]), w_up[i], conv_w[i], conv_b[i], w_down[i])
        h = h + f.astype(h.dtype)
        gate = jax.nn.sigmoid(rmsnorm(h, norm_ple_w[i]) @ w_pg[i])
        h = h + ((p[i] @ w_ple[i]) * gate).astype(h.dtype)
    return rmsnorm(h, final_norm_w)
```

```python
import functools
import math

import jax
import jax.numpy as jnp
from jax import lax
from jax.experimental import pallas as pl
from jax.experimental.pallas import tpu as pltpu

F32 = jnp.float32
BF16 = jnp.bfloat16

D_MODEL = 2048
S5_WIDTH = 512
S5_GROUP = 16
S5_GROUPS = 32
S5_STATE = 64
S5_COLS = S5_GROUPS * S5_STATE
POOL_WIDTH = 512
POOL_WINDOWS = (2, 4, 8, 16)
POOL_GROUP = 128
GLA_HEADS = 4
GLA_DK = 128
GLA_DV = 256
GLA_K_WIDTH = 512
GLA_V_WIDTH = 1024
GLA_GATE_RANK = 16
GLA_TAU = 16.0
GLA_CHUNK = 64
D_FF = 5632
PLE_DIM = 256
EPS = 1e-6

LANES = 128
SUBLANES = 8
S5_SEGMENTS = SUBLANES
S5_BUNDLE = LANES // S5_GROUP
S5_NBUNDLES = S5_GROUPS // S5_BUNDLE
S5_BCOLS = S5_BUNDLE * S5_STATE
S5_TB = 64
S5_CW = 512
POOL_HALO = 16

Z_S5, Z_POOL, Z_Q, Z_K, Z_V, Z_R, Z_G = 0, 512, 1024, 1536, 2048, 3072, 4096
Z_WIDTH = 4224
Z_TN = 1408

VMEM_LIMIT = 56 * 1024 * 1024


def _cparams(n_axes):
    return pltpu.CompilerParams(dimension_semantics=("arbitrary",) * n_axes,
                                vmem_limit_bytes=VMEM_LIMIT)


def _resident(shape):
    nd = len(shape)
    return pl.BlockSpec(shape, lambda *_: (0,) * nd, pipeline_mode=pl.Buffered(1))


def _rms(x, w):
    ms = jnp.mean(x * x, axis=-1, keepdims=True)
    return x * lax.rsqrt(ms + EPS) * w


def _sigmoid(x):
    return 1.0 / (1.0 + jnp.exp(-x))


def _dot(a, b):
    return jnp.dot(a, b, preferred_element_type=F32)


def _dot_nt(a, b):
    return lax.dot_general(a, b, (((1,), (1,)), ((), ())), preferred_element_type=F32)


def _dot_tn(a, b):
    return lax.dot_general(a, b, (((0,), (0,)), ((), ())), preferred_element_type=F32)


def _inproj_kernel(h_ref, nw_ref, w_ref, z_ref, a_ref):
    @pl.when(pl.program_id(1) == 0)
    def _():
        a_ref[...] = _rms(h_ref[...], nw_ref[...]).astype(BF16)

    z_ref[...] = _dot(a_ref[...], w_ref[...])


def _inproj(h, nw, w, tm):
    L = h.shape[0]
    return pl.pallas_call(
        _inproj_kernel,
        grid=(L // tm, Z_WIDTH // Z_TN),
        in_specs=[pl.BlockSpec((tm, D_MODEL), lambda i, j: (i, 0)),
                  pl.BlockSpec((1, D_MODEL), lambda i, j: (0, 0)),
                  pl.BlockSpec((D_MODEL, Z_TN), lambda i, j: (0, j))],
        out_specs=pl.BlockSpec((tm, Z_TN), lambda i, j: (i, j)),
        out_shape=jax.ShapeDtypeStruct((L, Z_WIDTH), F32),
        scratch_shapes=[pltpu.VMEM((tm, D_MODEL), BF16)],
        compiler_params=_cparams(2),
        name="inproj",
    )(h, nw, w)


def _gelu_tanh(x):
    return 0.5 * x * (1.0 + jnp.tanh(math.sqrt(2.0 / math.pi) * (x + 0.044715 * (x * x * x))))


def _s5_kernel(*refs, tb, pass2):
    if pass2:
        (u_ref, wb_ref, are_ref, aim_ref, xe_re_ref, xe_im_ref, ap_re_ref, ap_im_ref,
         wc_ref, d_ref, wglu_ref, bglu_ref, y_ref, ubuf, bre, bim, st_re, st_im) = refs
    else:
        (u_ref, wb_ref, are_ref, aim_ref, xe_re_ref, xe_im_ref,
         ubuf, bre, bim, st_re, st_im) = refs
    nseg = S5_SEGMENTS

    @pl.when(pl.program_id(0) == 0)
    def _init():
        if pass2:
            apr, api = ap_re_ref[...], ap_im_ref[...]
            st_re[0:1, :] = jnp.zeros((1, S5_COLS), F32)
            st_im[0:1, :] = jnp.zeros((1, S5_COLS), F32)
            for j in range(nseg - 1):
                cr, ci = st_re[j:j + 1, :], st_im[j:j + 1, :]
                st_re[j + 1:j + 2, :] = apr * cr - api * ci + xe_re_ref[j:j + 1, :]
                st_im[j + 1:j + 2, :] = apr * ci + api * cr + xe_im_ref[j:j + 1, :]
        else:
            st_re[...] = jnp.zeros((nseg, S5_COLS), F32)
            st_im[...] = jnp.zeros((nseg, S5_COLS), F32)

    for j in range(nseg):
        uj = u_ref[j]
        for b in range(S5_NBUNDLES):
            ubuf[b, pl.ds(j, tb, stride=nseg), :] = uj[:, b * LANES:(b + 1) * LANES]
    for b in range(S5_NBUNDLES):
        bu = _dot(ubuf[b].astype(BF16), wb_ref[b])
        bre[:, b * S5_BCOLS:(b + 1) * S5_BCOLS] = bu[:, :S5_BCOLS]
        bim[:, b * S5_BCOLS:(b + 1) * S5_BCOLS] = bu[:, S5_BCOLS:]

    for cc in range(S5_COLS // S5_CW):
        cols = slice(cc * S5_CW, (cc + 1) * S5_CW)
        ar = jnp.broadcast_to(are_ref[:, cols], (nseg, S5_CW))
        ai = jnp.broadcast_to(aim_ref[:, cols], (nseg, S5_CW))

        def step(t, carry, cols=cols, ar=ar, ai=ai):
            xr, xi = carry
            r0 = pl.multiple_of(t * nseg, nseg)
            nr = ar * xr - ai * xi + bre[pl.ds(r0, nseg), cols]
            ni = ar * xi + ai * xr + bim[pl.ds(r0, nseg), cols]
            if pass2:
                bre[pl.ds(r0, nseg), cols] = nr
                bim[pl.ds(r0, nseg), cols] = ni
            return nr, ni

        xr, xi = lax.fori_loop(0, tb, step, (st_re[:, cols], st_im[:, cols]), unroll=4)
        st_re[:, cols] = xr
        st_im[:, cols] = xi

    if not pass2:
        xe_re_ref[...] = st_re[...]
        xe_im_ref[...] = st_im[...]
        return

    ys = []
    for b in range(S5_NBUNDLES):
        cols = slice(b * S5_BCOLS, (b + 1) * S5_BCOLS)
        ys.append(_dot(bre[:, cols].astype(BF16), wc_ref[b, 0])
                  + _dot(bim[:, cols].astype(BF16), wc_ref[b, 1]))
    u = jnp.concatenate([ubuf[b] for b in range(S5_NBUNDLES)], axis=1)
    y = jnp.concatenate(ys, axis=1) + d_ref[...] * u
    y = _gelu_tanh(y)
    glu = _sigmoid(_dot(y.astype(BF16), wglu_ref[...]) + bglu_ref[...])
    out = y * glu
    for b in range(S5_NBUNDLES):
        ubuf[b] = out[:, b * LANES:(b + 1) * LANES]
    for j in range(nseg):
        y_ref[j] = jnp.concatenate(
            [ubuf[b, pl.ds(j, tb, stride=nseg), :] for b in range(S5_NBUNDLES)], axis=1).astype(BF16)


def _s5_mixer(z, prm, tb):
    L = z.shape[0]
    nseg = S5_SEGMENTS
    seg_len = L // nseg
    nblk = seg_len // tb
    rows = tb * nseg
    z3 = z.reshape(nseg, seg_len, Z_WIDTH)
    u_spec = pl.BlockSpec((nseg, tb, S5_WIDTH), lambda i: (0, i, Z_S5 // S5_WIDTH))
    state_shape = jax.ShapeDtypeStruct((nseg, S5_COLS), F32)
    scratch = [pltpu.VMEM((S5_NBUNDLES, rows, LANES), F32),
               pltpu.VMEM((rows, S5_COLS), F32), pltpu.VMEM((rows, S5_COLS), F32),
               pltpu.VMEM((nseg, S5_COLS), F32), pltpu.VMEM((nseg, S5_COLS), F32)]
    common = [u_spec, _resident(prm["wb"].shape), _resident((1, S5_COLS)), _resident((1, S5_COLS))]

    xe_re, xe_im = pl.pallas_call(
        functools.partial(_s5_kernel, tb=tb, pass2=False),
        grid=(nblk,),
        in_specs=common,
        out_specs=[_resident_out((nseg, S5_COLS)), _resident_out((nseg, S5_COLS))],
        out_shape=[state_shape, state_shape],
        scratch_shapes=scratch,
        compiler_params=_cparams(1),
        name="s5_states",
    )(z3, prm["wb"], prm["a_re"], prm["a_im"])

    y = pl.pallas_call(
        functools.partial(_s5_kernel, tb=tb, pass2=True),
        grid=(nblk,),
        in_specs=common + [_resident((nseg, S5_COLS)), _resident((nseg, S5_COLS)),
                           _resident((1, S5_COLS)), _resident((1, S5_COLS)),
                           _resident(prm["wc"].shape), _resident((1, S5_WIDTH)),
                           _resident((S5_WIDTH, S5_WIDTH)), _resident((1, S5_WIDTH))],
        out_specs=pl.BlockSpec((nseg, tb, S5_WIDTH), lambda i: (0, i, 0)),
        out_shape=jax.ShapeDtypeStruct((nseg, seg_len, S5_WIDTH), BF16),
        scratch_shapes=scratch,
        compiler_params=_cparams(1),
        name="s5_outputs",
    )(z3, prm["wb"], prm["a_re"], prm["a_im"], xe_re, xe_im, prm["ap_re"], prm["ap_im"],
      prm["wc"], prm["d"], prm["w_glu"], prm["b_glu"])
    return y.reshape(L, S5_WIDTH)


def _resident_out(shape):
    nd = len(shape)
    return pl.BlockSpec(shape, lambda *_: (0,) * nd)


def _s5_prepare(a_re, a_im, log_dt, b_re, b_im, c_re, c_im, d_skip, w_glu, b_glu, seg_len):
    dt = jnp.exp(log_dt)[:, None]
    mag = jnp.exp(a_re * dt)
    ab_re, ab_im = mag * jnp.cos(a_im * dt), mag * jnp.sin(a_im * dt)
    nr, ni = ab_re - 1.0, ab_im
    den = a_re * a_re + a_im * a_im
    f_re, f_im = (nr * a_re + ni * a_im) / den, (ni * a_re - nr * a_im) / den
    bb_re = f_re[..., None] * b_re - f_im[..., None] * b_im
    bb_im = f_re[..., None] * b_im + f_im[..., None] * b_re
    pr, pi = jnp.ones_like(ab_re), jnp.zeros_like(ab_re)
    sr, si, e = ab_re, ab_im, seg_len
    while e:
        if e & 1:
            pr, pi = pr * sr - pi * si, pr * si + pi * sr
        sr, si = sr * sr - si * si, 2.0 * sr * si
        e >>= 1
    eye = jnp.eye(S5_BUNDLE, dtype=F32)
    nb, gb = S5_NBUNDLES, S5_BUNDLE

    def b_slab(bb):
        return jnp.einsum("bgnc,gh->bgchn", bb.reshape(nb, gb, S5_STATE, S5_GROUP), eye).reshape(
            nb, LANES, S5_BCOLS)

    def c_slab(cm):
        return jnp.einsum("bgcn,gh->bgnhc", cm.reshape(nb, gb, S5_GROUP, S5_STATE), eye).reshape(
            nb, S5_BCOLS, LANES)

    return {
        "wb": jnp.concatenate([b_slab(bb_re), b_slab(bb_im)], axis=-1).astype(BF16),
        "wc": jnp.stack([c_slab(c_re), -c_slab(c_im)], axis=1).astype(BF16),
        "a_re": ab_re.reshape(1, S5_COLS), "a_im": ab_im.reshape(1, S5_COLS),
        "ap_re": pr.reshape(1, S5_COLS), "ap_im": pi.reshape(1, S5_COLS),
        "d": d_skip.reshape(1, S5_WIDTH), "w_glu": w_glu.astype(BF16),
        "b_glu": b_glu.reshape(1, S5_WIDTH),
    }


def _mix_kernel(zp_ref, q_ref, k_ref, v_ref, r_ref, g_ref, pw_ref, ps_ref, wa_ref, ba_ref, gnw_ref,
                tri_ref, ypool_ref, ygla_ref, zext, s_ref, *, tm):
    i = pl.program_id(0)

    @pl.when(i == 0)
    def _init():
        zext[0:POOL_HALO, :] = jnp.zeros((POOL_HALO, POOL_WIDTH), F32)
        s_ref[...] = jnp.zeros(s_ref.shape, F32)

    z = zp_ref[...]
    zext[POOL_HALO:POOL_HALO + tm, :] = z
    pos = (i * tm + 1 + lax.broadcasted_iota(jnp.int32, (tm, 1), 0)).astype(F32)
    for gi, w in enumerate(POOL_WINDOWS):
        cols = slice(gi * POOL_GROUP, (gi + 1) * POOL_GROUP)
        zc = z[:, cols]
        s = zc
        for back in range(1, w):
            s = s + zext[pl.ds(POOL_HALO - back, tm), cols]
        pooled = s / jnp.minimum(pos, float(w)) - zc
        mixed = _dot(pooled.astype(BF16), pw_ref[gi]) * ps_ref[:, cols]
        ypool_ref[:, cols] = mixed.astype(BF16)
    zext[0:POOL_HALO, :] = zext[tm:tm + POOL_HALO, :]

    logit = _dot(g_ref[...].astype(BF16), wa_ref[...]) + ba_ref[...]
    la = (jnp.minimum(logit, 0.0) - jnp.log1p(jnp.exp(-jnp.abs(logit)))) / GLA_TAU
    la_hi = la.astype(BF16)
    la_lo = (la - la_hi.astype(F32)).astype(BF16)
    tri = tri_ref[...]
    b = _dot(tri, la_hi) + _dot(tri, la_lo)
    nch = tm // GLA_CHUNK
    b_last = jnp.concatenate(
        [jnp.broadcast_to(b[(c + 1) * GLA_CHUNK - 1:(c + 1) * GLA_CHUNK, :], (GLA_CHUNK, GLA_K_WIDTH))
         for c in range(nch)], axis=0)
    q_dec = (q_ref[...] * (GLA_DK ** -0.5) * jnp.exp(b)).astype(BF16)
    k = k_ref[...]
    k_dec = (k * jnp.exp(-b)).astype(BF16)
    k_end = (k * jnp.exp(b_last - b)).astype(BF16)
    decay = jnp.exp(b_last)
    causal = (lax.broadcasted_iota(jnp.int32, (GLA_CHUNK, GLA_CHUNK), 0)
              >= lax.broadcasted_iota(jnp.int32, (GLA_CHUNK, GLA_CHUNK), 1))
    gnw = gnw_ref[...]
    for c in range(nch):
        rows = slice(c * GLA_CHUNK, (c + 1) * GLA_CHUNK)
        for hd in range(GLA_HEADS):
            kc = slice(hd * GLA_DK, (hd + 1) * GLA_DK)
            vc = slice(hd * GLA_DV, (hd + 1) * GLA_DV)
            qd = q_dec[rows, kc]
            vv = v_ref[rows, vc].astype(BF16)
            st = s_ref[hd]
            scores = jnp.where(causal, _dot_nt(qd, k_dec[rows, kc]), 0.0)
            o = _dot(scores.astype(BF16), vv) + _dot_nt(qd, st.astype(BF16))
            s_ref[hd] = (decay[c * GLA_CHUNK:c * GLA_CHUNK + 1, kc] * st
                         + _dot_tn(vv, k_end[rows, kc]))
            o = _rms(o, gnw)
            rr = r_ref[rows, vc]
            ygla_ref[rows, vc] = (o * (rr * _sigmoid(rr))).astype(BF16)


def _mixers(z, prm, tm):
    L = z.shape[0]
    row = lambda w, col: pl.BlockSpec((tm, w), lambda i: (i, col // w))
    return pl.pallas_call(
        functools.partial(_mix_kernel, tm=tm),
        grid=(L // tm,),
        in_specs=[row(POOL_WIDTH, Z_POOL), row(GLA_K_WIDTH, Z_Q), row(GLA_K_WIDTH, Z_K),
                  row(GLA_V_WIDTH, Z_V), row(GLA_V_WIDTH, Z_R), row(LANES, Z_G),
                  _resident(prm["pool_w"].shape), _resident((1, POOL_WIDTH)),
                  _resident((LANES, GLA_K_WIDTH)), _resident((1, GLA_K_WIDTH)),
                  _resident((1, GLA_DV)), _resident((tm, tm))],
        out_specs=[pl.BlockSpec((tm, POOL_WIDTH), lambda i: (i, 0)),
                   pl.BlockSpec((tm, GLA_V_WIDTH), lambda i: (i, 0))],
        out_shape=[jax.ShapeDtypeStruct((L, POOL_WIDTH), BF16),
                   jax.ShapeDtypeStruct((L, GLA_V_WIDTH), BF16)],
        scratch_shapes=[pltpu.VMEM((tm + POOL_HALO, POOL_WIDTH), F32),
                        pltpu.VMEM((GLA_HEADS, GLA_DV, GLA_DK), F32)],
        compiler_params=_cparams(1),
        name="mixers",
    )(z, z, z, z, z, z, prm["pool_w"], prm["pool_scale"], prm["w_a2"], prm["b_a"], prm["gla_norm_w"],
      prm["tri"])


def _outproj_kernel(h_ref, ys_ref, yp_ref, yg_ref, w_ref, o_ref):
    acc = _dot(ys_ref[...], w_ref[0:S5_WIDTH, :])
    acc = acc + _dot(yp_ref[...], w_ref[S5_WIDTH:S5_WIDTH + POOL_WIDTH, :])
    acc = acc + _dot(yg_ref[...], w_ref[S5_WIDTH + POOL_WIDTH:, :])
    o_ref[...] = h_ref[...] + acc


def _outproj(h, ys, yp, yg, w, tm):
    L = h.shape[0]
    row = lambda wd: pl.BlockSpec((tm, wd), lambda i: (i, 0))
    return pl.pallas_call(
        _outproj_kernel,
        grid=(L // tm,),
        in_specs=[row(D_MODEL), row(S5_WIDTH), row(POOL_WIDTH), row(GLA_V_WIDTH), _resident(w.shape)],
        out_specs=row(D_MODEL),
        out_shape=jax.ShapeDtypeStruct((L, D_MODEL), F32),
        compiler_params=_cparams(1),
        name="outproj",
    )(h, ys, yp, yg, w)


FFN_FC = 512
FFN_NC = D_FF // FFN_FC
CONV_HALO = SUBLANES


def _ffn_kernel(h_ref, nw_ref, wg_ref, wv_ref, cw_ref, cb_ref, wd_ref, o_ref, a_ref, gbuf, carry, *, tm):
    i, c = pl.program_id(0), pl.program_id(1)

    @pl.when(c == 0)
    def _():
        a_ref[...] = _rms(h_ref[...], nw_ref[...]).astype(BF16)

    @pl.when(i == 0)
    def _():
        carry[c] = jnp.zeros((CONV_HALO, FFN_FC), F32)

    a = a_ref[...]
    gate = _dot(a, wg_ref[...])
    val = _dot(a, wv_ref[...])
    gbuf[0:CONV_HALO, :] = carry[c]
    gbuf[CONV_HALO:CONV_HALO + tm, :] = gate
    carry[c] = gate[tm - CONV_HALO:tm, :]
    cw = cw_ref[...]
    gc = cb_ref[...] + gbuf[pl.ds(CONV_HALO - 2, tm), :] * cw[0:1, :]
    gc = gc + gbuf[pl.ds(CONV_HALO - 1, tm), :] * cw[1:2, :]
    gc = gc + gate * cw[2:3, :]
    act = (gc * _sigmoid(gc) * val).astype(BF16)
    d = _dot(act, wd_ref[...])

    @pl.when(c == 0)
    def _():
        o_ref[...] = h_ref[...] + d

    @pl.when(c > 0)
    def _():
        o_ref[...] += d


def _ffn(h, nw, w_up, conv_w, conv_b, w_down, tm):
    L = h.shape[0]
    return pl.pallas_call(
        functools.partial(_ffn_kernel, tm=tm),
        grid=(L // tm, FFN_NC),
        in_specs=[pl.BlockSpec((tm, D_MODEL), lambda i, c: (i, 0)),
                  pl.BlockSpec((1, D_MODEL), lambda i, c: (0, 0)),
                  pl.BlockSpec((D_MODEL, FFN_FC), lambda i, c: (0, c)),
                  pl.BlockSpec((D_MODEL, FFN_FC), lambda i, c: (0, FFN_NC + c)),
                  pl.BlockSpec((3, FFN_FC), lambda i, c: (0, c)),
                  pl.BlockSpec((1, FFN_FC), lambda i, c: (0, c)),
                  pl.BlockSpec((FFN_FC, D_MODEL), lambda i, c: (c, 0))],
        out_specs=pl.BlockSpec((tm, D_MODEL), lambda i, c: (i, 0)),
        out_shape=jax.ShapeDtypeStruct((L, D_MODEL), F32),
        scratch_shapes=[pltpu.VMEM((tm, D_MODEL), BF16),
                        pltpu.VMEM((tm + CONV_HALO, FFN_FC), F32),
                        pltpu.VMEM((FFN_NC, CONV_HALO, FFN_FC), F32)],
        compiler_params=_cparams(2),
        name="ffn",
    )(h, nw, w_up, w_up, conv_w, conv_b, w_down)


def _ple_kernel(*refs, final):
    if final:
        h_ref, p_ref, nw_ref, wpg_ref, wple_ref, fw_ref, o_ref = refs
    else:
        h_ref, p_ref, nw_ref, wpg_ref, wple_ref, o_ref = refs
    h = h_ref[...]
    gate = _sigmoid(_dot(_rms(h, nw_ref[...]).astype(BF16), wpg_ref[...]))
    out = h + _dot(p_ref[...].astype(BF16), wple_ref[...]) * gate
    if final:
        out = _rms(out, fw_ref[...])
    o_ref[...] = out


def _ple(h, p, nw, w_pg, w_ple, final_w, tm):
    L = h.shape[0]
    final = final_w is not None
    in_specs = [pl.BlockSpec((tm, D_MODEL), lambda i: (i, 0)),
                pl.BlockSpec((tm, PLE_DIM), lambda i: (i, 0)),
                _resident((1, D_MODEL)), _resident(w_pg.shape), _resident(w_ple.shape)]
    args = [h, p, nw, w_pg, w_ple]
    if final:
        in_specs.append(_resident((1, D_MODEL)))
        args.append(final_w)
    return pl.pallas_call(
        functools.partial(_ple_kernel, final=final),
        grid=(L // tm,),
        in_specs=in_specs,
        out_specs=pl.BlockSpec((tm, D_MODEL), lambda i: (i, 0)),
        out_shape=jax.ShapeDtypeStruct((L, D_MODEL), F32),
        compiler_params=_cparams(1),
        name="ple_final" if final else "ple",
    )(*args)


def _pack_w_in(w_in):
    u, zp, q, k, v, g, r = jnp.split(w_in, [512, 1024, 1536, 2048, 3072, 3088], axis=1)
    g = jnp.pad(g, ((0, 0), (0, LANES - GLA_GATE_RANK)))
    return jnp.concatenate([u, zp, q, k, v, r, g], axis=1).astype(BF16)


def _chunk_tril(tm):
    r = jnp.arange(tm)
    same = (r[:, None] // GLA_CHUNK) == (r[None, :] // GLA_CHUNK)
    return (same & (r[:, None] >= r[None, :])).astype(BF16)


def kernel(x, p, norm_mix_w, w_in, s5_a_re, s5_a_im, s5_log_dt, s5_b_re, s5_b_im, s5_c_re, s5_c_im, s5_d, s5_w_glu, s5_b_glu, pool_w, pool_scale, gla_w_a2, gla_b_a, gla_norm_w, w_out, norm_ffn_w, w_up, conv_w, conv_b, w_down, norm_ple_w, w_ple, w_pg, final_norm_w):
    bsz, L, _ = x.shape
    assert bsz == 1 and L % (S5_SEGMENTS * S5_TB) == 0
    depth = w_in.shape[0]
    tm = min(512, L)
    tm_mix = min(256, L)
    tri = _chunk_tril(tm_mix)
    h = x.reshape(L, D_MODEL)
    for i in range(depth):
        z = _inproj(h, norm_mix_w[i].reshape(1, D_MODEL), _pack_w_in(w_in[i]), tm)
        s5_prm = _s5_prepare(s5_a_re[i], s5_a_im[i], s5_log_dt[i], s5_b_re[i], s5_b_im[i], s5_c_re[i],
                             s5_c_im[i], s5_d[i], s5_w_glu[i], s5_b_glu[i], L // S5_SEGMENTS)
        y_s5 = _s5_mixer(z, s5_prm, S5_TB)
        mix_prm = {
            "pool_w": pool_w[i].astype(BF16), "pool_scale": pool_scale[i].reshape(1, POOL_WIDTH),
            "w_a2": jnp.pad(gla_w_a2[i], ((0, LANES - GLA_GATE_RANK), (0, 0))).astype(BF16),
            "b_a": gla_b_a[i].reshape(1, GLA_K_WIDTH), "gla_norm_w": gla_norm_w[i].reshape(1, GLA_DV),
            "tri": tri,
        }
        y_pool, y_gla = _mixers(z, mix_prm, tm_mix)
        h = _outproj(h, y_s5, y_pool, y_gla, w_out[i].astype(BF16), tm)
        h = _ffn(h, norm_ffn_w[i].reshape(1, D_MODEL), w_up[i].astype(BF16), conv_w[i],
                 conv_b[i].reshape(1, D_FF), w_down[i].astype(BF16), tm)
        final_w = final_norm_w.reshape(1, D_MODEL) if i == depth - 1 else None
        h = _ple(h, p[i].reshape(L, PLE_DIM), norm_ple_w[i].reshape(1, D_MODEL), w_pg[i].astype(BF16),
                 w_ple[i].astype(BF16), final_w, tm)
    return h.reshape(bsz, L, D_MODEL)
```

```python
import functools
import math

import jax
import jax.numpy as jnp
from jax import lax
from jax.experimental import pallas as pl
from jax.experimental.pallas import tpu as pltpu

F32 = jnp.float32
BF16 = jnp.bfloat16

D_MODEL = 2048
S5_WIDTH = 512
S5_GROUP = 16
S5_GROUPS = 32
S5_STATE = 64
S5_COLS = S5_GROUPS * S5_STATE
POOL_WIDTH = 512
POOL_WINDOWS = (2, 4, 8, 16)
POOL_GROUP = 128
GLA_HEADS = 4
GLA_DK = 128
GLA_DV = 256
GLA_K_WIDTH = 512
GLA_V_WIDTH = 1024
GLA_GATE_RANK = 16
GLA_TAU = 16.0
GLA_CHUNK = 64
D_FF = 5632
PLE_DIM = 256
EPS = 1e-6

LANES = 128
SUBLANES = 8
S5_SEGMENTS = SUBLANES
S5_BUNDLE = LANES // S5_GROUP
S5_NBUNDLES = S5_GROUPS // S5_BUNDLE
S5_BCOLS = S5_BUNDLE * S5_STATE
S5_TB = 64
POOL_HALO = 16

Z_S5, Z_POOL, Z_Q, Z_K, Z_V, Z_R, Z_G = 0, 512, 1024, 1536, 2048, 3072, 4096
Z_WIDTH = 4224
MM_TN = 512

VMEM_LIMIT = 56 * 1024 * 1024


def _cparams(n_axes):
    return pltpu.CompilerParams(dimension_semantics=("arbitrary",) * n_axes,
                                vmem_limit_bytes=VMEM_LIMIT)


def _resident(shape):
    nd = len(shape)
    return pl.BlockSpec(shape, lambda *_: (0,) * nd, pipeline_mode=pl.Buffered(1))


def _rms(x, w):
    ms = jnp.mean(x * x, axis=-1, keepdims=True)
    return x * lax.rsqrt(ms + EPS) * w


def _sigmoid(x):
    return 1.0 / (1.0 + jnp.exp(-x))


def _dot(a, b):
    return jnp.dot(a, b, preferred_element_type=F32)


def _dot_nt(a, b):
    return lax.dot_general(a, b, (((1,), (1,)), ((), ())), preferred_element_type=F32)


def _dot_tn(a, b):
    return lax.dot_general(a, b, (((0,), (0,)), ((), ())), preferred_element_type=F32)


def _col_chunks(n):
    return [(c0, min(c0 + MM_TN, n)) for c0 in range(0, n, MM_TN)]


def _inproj_kernel(h_ref, nw_ref, w_ref, z_ref, a_ref):
    a_ref[...] = _rms(h_ref[...], nw_ref[...]).astype(BF16)
    a = a_ref[...]
    for c0, c1 in _col_chunks(Z_WIDTH):
        z_ref[:, c0:c1] = _dot(a, w_ref[:, c0:c1])


def _inproj(h, nw, w, layer, tm):
    L = h.shape[0]
    return pl.pallas_call(
        _inproj_kernel,
        grid=(L // tm,),
        in_specs=[pl.BlockSpec((tm, D_MODEL), lambda i: (i, 0)),
                  pl.BlockSpec((None, 1, D_MODEL), lambda i: (layer, 0, 0)),
                  pl.BlockSpec((None, D_MODEL, Z_WIDTH), lambda i: (layer, 0, 0),
                               pipeline_mode=pl.Buffered(1))],
        out_specs=pl.BlockSpec((tm, Z_WIDTH), lambda i: (i, 0)),
        out_shape=jax.ShapeDtypeStruct((L, Z_WIDTH), F32),
        scratch_shapes=[pltpu.VMEM((tm, D_MODEL), BF16)],
        compiler_params=_cparams(1),
        name="inproj",
    )(h, nw, w)


def _gelu_tanh(x):
    return 0.5 * x * (1.0 + jnp.tanh(math.sqrt(2.0 / math.pi) * (x + 0.044715 * (x * x * x))))


def _s5_kernel(*refs, tb, pass2):
    if pass2:
        (u_ref, wb_ref, are_ref, aim_ref, xe_re_ref, xe_im_ref, ap_re_ref, ap_im_ref,
         wc_ref, d_ref, wglu_ref, bglu_ref, y_ref, ubuf, xs, st_re, st_im) = refs
    else:
        (u_ref, wb_ref, are_ref, aim_ref, xe_re_ref, xe_im_ref,
         ubuf, xs, st_re, st_im) = refs
    nseg = S5_SEGMENTS

    @pl.when(pl.program_id(0) == 0)
    def _init():
        if pass2:
            apr, api = ap_re_ref[...], ap_im_ref[...]
            st_re[0:1, :] = jnp.zeros((1, S5_COLS), F32)
            st_im[0:1, :] = jnp.zeros((1, S5_COLS), F32)
            for j in range(nseg - 1):
                cr, ci = st_re[j:j + 1, :], st_im[j:j + 1, :]
                st_re[j + 1:j + 2, :] = apr * cr - api * ci + xe_re_ref[j:j + 1, :]
                st_im[j + 1:j + 2, :] = apr * ci + api * cr + xe_im_ref[j:j + 1, :]
        else:
            st_re[...] = jnp.zeros((nseg, S5_COLS), F32)
            st_im[...] = jnp.zeros((nseg, S5_COLS), F32)

    for j in range(nseg):
        uj = u_ref[j]
        for b in range(S5_NBUNDLES):
            ubuf[b, pl.ds(j, tb, stride=nseg), :] = uj[:, b * LANES:(b + 1) * LANES]
    for b in range(S5_NBUNDLES):
        xs[:, 2 * b * S5_BCOLS:2 * (b + 1) * S5_BCOLS] = _dot(ubuf[b].astype(BF16), wb_ref[b])

    for b in range(S5_NBUNDLES):
        cols = slice(b * S5_BCOLS, (b + 1) * S5_BCOLS)
        cre = slice(2 * b * S5_BCOLS, (2 * b + 1) * S5_BCOLS)
        cim = slice((2 * b + 1) * S5_BCOLS, (2 * b + 2) * S5_BCOLS)
        ar = jnp.broadcast_to(are_ref[:, cols], (nseg, S5_BCOLS))
        ai = jnp.broadcast_to(aim_ref[:, cols], (nseg, S5_BCOLS))
        xr, xi = st_re[:, cols], st_im[:, cols]
        for t in range(tb):
            rows = slice(t * nseg, (t + 1) * nseg)
            xr, xi = (ar * xr - ai * xi + xs[rows, cre], ar * xi + ai * xr + xs[rows, cim])
            if pass2:
                xs[rows, cre] = xr
                xs[rows, cim] = xi
        st_re[:, cols] = xr
        st_im[:, cols] = xi

    if not pass2:
        xe_re_ref[...] = st_re[...]
        xe_im_ref[...] = st_im[...]
        return

    ys = [_dot(xs[:, 2 * b * S5_BCOLS:2 * (b + 1) * S5_BCOLS].astype(BF16), wc_ref[b])
          for b in range(S5_NBUNDLES)]
    u = jnp.concatenate([ubuf[b] for b in range(S5_NBUNDLES)], axis=1)
    y = jnp.concatenate(ys, axis=1) + d_ref[...] * u
    y = _gelu_tanh(y)
    glu = _sigmoid(_dot(y.astype(BF16), wglu_ref[...]) + bglu_ref[...])
    out = y * glu
    for b in range(S5_NBUNDLES):
        ubuf[b] = out[:, b * LANES:(b + 1) * LANES]
    for j in range(nseg):
        y_ref[j] = jnp.concatenate(
            [ubuf[b, pl.ds(j, tb, stride=nseg), :] for b in range(S5_NBUNDLES)], axis=1).astype(BF16)


def _s5_mixer(z, prm, tb):
    L = z.shape[0]
    nseg = S5_SEGMENTS
    seg_len = L // nseg
    nblk = seg_len // tb
    rows = tb * nseg
    z3 = z.reshape(nseg, seg_len, Z_WIDTH)
    u_spec = pl.BlockSpec((nseg, tb, S5_WIDTH), lambda i: (0, i, Z_S5 // S5_WIDTH))
    state_shape = jax.ShapeDtypeStruct((nseg, S5_COLS), F32)
    scratch = [pltpu.VMEM((S5_NBUNDLES, rows, LANES), F32),
               pltpu.VMEM((rows, 2 * S5_COLS), F32),
               pltpu.VMEM((nseg, S5_COLS), F32), pltpu.VMEM((nseg, S5_COLS), F32)]
    common = [u_spec, _resident(prm["wb"].shape), _resident((1, S5_COLS)), _resident((1, S5_COLS))]

    xe_re, xe_im = pl.pallas_call(
        functools.partial(_s5_kernel, tb=tb, pass2=False),
        grid=(nblk,),
        in_specs=common,
        out_specs=[_resident_out((nseg, S5_COLS)), _resident_out((nseg, S5_COLS))],
        out_shape=[state_shape, state_shape],
        scratch_shapes=scratch,
        compiler_params=_cparams(1),
        name="s5_states",
    )(z3, prm["wb"], prm["a_re"], prm["a_im"])

    y = pl.pallas_call(
        functools.partial(_s5_kernel, tb=tb, pass2=True),
        grid=(nblk,),
        in_specs=common + [_resident((nseg, S5_COLS)), _resident((nseg, S5_COLS)),
                           _resident((1, S5_COLS)), _resident((1, S5_COLS)),
                           _resident(prm["wc"].shape), _resident((1, S5_WIDTH)),
                           _resident((S5_WIDTH, S5_WIDTH)), _resident((1, S5_WIDTH))],
        out_specs=pl.BlockSpec((nseg, tb, S5_WIDTH), lambda i: (0, i, 0)),
        out_shape=jax.ShapeDtypeStruct((nseg, seg_len, S5_WIDTH), BF16),
        scratch_shapes=scratch,
        compiler_params=_cparams(1),
        name="s5_outputs",
    )(z3, prm["wb"], prm["a_re"], prm["a_im"], xe_re, xe_im, prm["ap_re"], prm["ap_im"],
      prm["wc"], prm["d"], prm["w_glu"], prm["b_glu"])
    return y.reshape(L, S5_WIDTH)


def _resident_out(shape):
    nd = len(shape)
    return pl.BlockSpec(shape, lambda *_: (0,) * nd)


def _s5_prepare(a_re, a_im, log_dt, b_re, b_im, c_re, c_im, d_skip, w_glu, b_glu, seg_len):
    dt = jnp.exp(log_dt)[:, None]
    mag = jnp.exp(a_re * dt)
    ab_re, ab_im = mag * jnp.cos(a_im * dt), mag * jnp.sin(a_im * dt)
    nr, ni = ab_re - 1.0, ab_im
    den = a_re * a_re + a_im * a_im
    f_re, f_im = (nr * a_re + ni * a_im) / den, (ni * a_re - nr * a_im) / den
    bb_re = f_re[..., None] * b_re - f_im[..., None] * b_im
    bb_im = f_re[..., None] * b_im + f_im[..., None] * b_re
    pr, pi = jnp.ones_like(ab_re), jnp.zeros_like(ab_re)
    sr, si, e = ab_re, ab_im, seg_len
    while e:
        if e & 1:
            pr, pi = pr * sr - pi * si, pr * si + pi * sr
        sr, si = sr * sr - si * si, 2.0 * sr * si
        e >>= 1
    eye = jnp.eye(S5_BUNDLE, dtype=F32)
    nb, gb = S5_NBUNDLES, S5_BUNDLE

    def b_slab(bb):
        return jnp.einsum("bgnc,gh->bgchn", bb.reshape(nb, gb, S5_STATE, S5_GROUP), eye).reshape(
            nb, LANES, S5_BCOLS)

    def c_slab(cm):
        return jnp.einsum("bgcn,gh->bgnhc", cm.reshape(nb, gb, S5_GROUP, S5_STATE), eye).reshape(
            nb, S5_BCOLS, LANES)

    return {
        "wb": jnp.concatenate([b_slab(bb_re), b_slab(bb_im)], axis=-1).astype(BF16),
        "wc": jnp.concatenate([c_slab(c_re), -c_slab(c_im)], axis=1).astype(BF16),
        "a_re": ab_re.reshape(1, S5_COLS), "a_im": ab_im.reshape(1, S5_COLS),
        "ap_re": pr.reshape(1, S5_COLS), "ap_im": pi.reshape(1, S5_COLS),
        "d": d_skip.reshape(1, S5_WIDTH), "w_glu": w_glu.astype(BF16),
        "b_glu": b_glu.reshape(1, S5_WIDTH),
    }


def _mix_kernel(zp_ref, q_ref, k_ref, v_ref, r_ref, g_ref, pw_ref, ps_ref, wa_ref, ba_ref, gnw_ref,
                tri_ref, ypool_ref, ygla_ref, zext, s_ref, *, tm):
    i = pl.program_id(0)

    @pl.when(i == 0)
    def _init():
        zext[0:POOL_HALO, :] = jnp.zeros((POOL_HALO, POOL_WIDTH), F32)
        s_ref[...] = jnp.zeros(s_ref.shape, F32)

    zext[POOL_HALO:POOL_HALO + tm, :] = zp_ref[...]
    pos = (i * tm + 1 + lax.broadcasted_iota(jnp.int32, (tm, 1), 0)).astype(F32)
    for gi, w in enumerate(POOL_WINDOWS):
        cols = slice(gi * POOL_GROUP, (gi + 1) * POOL_GROUP)
        ze = zext[:, cols]
        s, span = ze, 1
        while span < w:
            s = s + pltpu.roll(s, span, axis=0)
            span *= 2
        zc = ze[POOL_HALO:, :]
        pooled = s[POOL_HALO:, :] / jnp.minimum(pos, float(w)) - zc
        mixed = _dot(pooled.astype(BF16), pw_ref[gi]) * ps_ref[:, cols]
        ypool_ref[:, cols] = mixed.astype(BF16)
    zext[0:POOL_HALO, :] = zext[tm:tm + POOL_HALO, :]

    logit = _dot(g_ref[...].astype(BF16), wa_ref[...]) + ba_ref[...]
    la = (jnp.minimum(logit, 0.0) - jnp.log(1.0 + jnp.exp(-jnp.abs(logit)))) / GLA_TAU
    la_hi = la.astype(BF16)
    la_lo = (la - la_hi.astype(F32)).astype(BF16)
    tri = tri_ref[...]
    b = _dot(tri, la_hi) + _dot(tri, la_lo)
    nch = tm // GLA_CHUNK
    b_last = jnp.concatenate(
        [jnp.broadcast_to(b[(c + 1) * GLA_CHUNK - 1:(c + 1) * GLA_CHUNK, :], (GLA_CHUNK, GLA_K_WIDTH))
         for c in range(nch)], axis=0)
    q_dec = (q_ref[...] * (GLA_DK ** -0.5) * jnp.exp(b)).astype(BF16)
    k = k_ref[...]
    k_dec = (k * jnp.exp(-b)).astype(BF16)
    k_end = (k * jnp.exp(b_last - b)).astype(BF16)
    decay = jnp.exp(b_last)
    causal = (lax.broadcasted_iota(jnp.int32, (GLA_CHUNK, GLA_CHUNK), 0)
              >= lax.broadcasted_iota(jnp.int32, (GLA_CHUNK, GLA_CHUNK), 1))
    gnw = gnw_ref[...]
    units = [(hd, c) for hd in range(GLA_HEADS) for c in range(nch)]
    rows = lambda c: slice(c * GLA_CHUNK, (c + 1) * GLA_CHUNK)
    kcol = lambda hd: slice(hd * GLA_DK, (hd + 1) * GLA_DK)
    vcol = lambda hd: slice(hd * GLA_DV, (hd + 1) * GLA_DV)
    vv = {(hd, c): v_ref[rows(c), vcol(hd)].astype(BF16) for hd, c in units}
    scores = {(hd, c): _dot_nt(q_dec[rows(c), kcol(hd)], k_dec[rows(c), kcol(hd)]) for hd, c in units}
    kv = {(hd, c): _dot_tn(vv[hd, c], k_end[rows(c), kcol(hd)]) for hd, c in units}
    st_in = {}
    for hd in range(GLA_HEADS):
        st = s_ref[hd]
        for c in range(nch):
            st_in[hd, c] = st.astype(BF16)
            st = decay[c * GLA_CHUNK:c * GLA_CHUNK + 1, kcol(hd)] * st + kv[hd, c]
        s_ref[hd] = st
    for hd, c in units:
        sc = jnp.where(causal, scores[hd, c], 0.0).astype(BF16)
        o = _dot(sc, vv[hd, c]) + _dot_nt(q_dec[rows(c), kcol(hd)], st_in[hd, c])
        o = _rms(o, gnw)
        rr = r_ref[rows(c), vcol(hd)]
        ygla_ref[rows(c), vcol(hd)] = (o * (rr * _sigmoid(rr))).astype(BF16)


def _mixers(z, prm, tm):
    L = z.shape[0]
    row = lambda w, col: pl.BlockSpec((tm, w), lambda i: (i, col // w))
    return pl.pallas_call(
        functools.partial(_mix_kernel, tm=tm),
        grid=(L // tm,),
        in_specs=[row(POOL_WIDTH, Z_POOL), row(GLA_K_WIDTH, Z_Q), row(GLA_K_WIDTH, Z_K),
                  row(GLA_V_WIDTH, Z_V), row(GLA_V_WIDTH, Z_R), row(LANES, Z_G),
                  _resident(prm["pool_w"].shape), _resident((1, POOL_WIDTH)),
                  _resident((LANES, GLA_K_WIDTH)), _resident((1, GLA_K_WIDTH)),
                  _resident((1, GLA_DV)), _resident((tm, tm))],
        out_specs=[pl.BlockSpec((tm, POOL_WIDTH), lambda i: (i, 0)),
                   pl.BlockSpec((tm, GLA_V_WIDTH), lambda i: (i, 0))],
        out_shape=[jax.ShapeDtypeStruct((L, POOL_WIDTH), BF16),
                   jax.ShapeDtypeStruct((L, GLA_V_WIDTH), BF16)],
        scratch_shapes=[pltpu.VMEM((tm + POOL_HALO, POOL_WIDTH), F32),
                        pltpu.VMEM((GLA_HEADS, GLA_DV, GLA_DK), F32)],
        compiler_params=_cparams(1),
        name="mixers",
    )(z, z, z, z, z, z, prm["pool_w"], prm["pool_scale"], prm["w_a2"], prm["b_a"], prm["gla_norm_w"],
      prm["tri"])


def _outproj_kernel(h_ref, ys_ref, yp_ref, yg_ref, w_ref, o_ref, m_ref):
    m_ref[:, 0:S5_WIDTH] = ys_ref[...]
    m_ref[:, S5_WIDTH:S5_WIDTH + POOL_WIDTH] = yp_ref[...]
    m_ref[:, S5_WIDTH + POOL_WIDTH:] = yg_ref[...]
    m = m_ref[...]
    for c0, c1 in _col_chunks(D_MODEL):
        o_ref[:, c0:c1] = h_ref[:, c0:c1] + _dot(m, w_ref[:, c0:c1])


def _outproj(h, ys, yp, yg, w, layer, tm):
    L = h.shape[0]
    row = lambda wd: pl.BlockSpec((tm, wd), lambda i: (i, 0))
    return pl.pallas_call(
        _outproj_kernel,
        grid=(L // tm,),
        in_specs=[row(D_MODEL), row(S5_WIDTH), row(POOL_WIDTH), row(GLA_V_WIDTH),
                  pl.BlockSpec((None, D_MODEL, D_MODEL), lambda i: (layer, 0, 0),
                               pipeline_mode=pl.Buffered(1))],
        out_specs=row(D_MODEL),
        out_shape=jax.ShapeDtypeStruct((L, D_MODEL), F32),
        scratch_shapes=[pltpu.VMEM((tm, D_MODEL), BF16)],
        compiler_params=_cparams(1),
        name="outproj",
    )(h, ys, yp, yg, w)


FFN_FC = 512
FFN_NC = D_FF // FFN_FC
FFN_FN = 512
FFN_NN = D_MODEL // FFN_FN
CONV_HALO = SUBLANES


def _serpentine(i, k, n):
    k = jnp.clip(k, 0, n - 1)
    return jnp.where(i % 2 == 0, k, n - 1 - k)


def _ffn_kernel(h_ref, nw_ref, wg_ref, wv_ref, cw_ref, cb_ref, wd_ref, o_ref, a_ref, act_ref, graw, carry,
                *, tm):
    i, s = pl.program_id(0), pl.program_id(1)

    @pl.when(s == 0)
    def _():
        a_ref[...] = _rms(h_ref[...], nw_ref[...]).astype(BF16)

    @pl.when(s < FFN_NC)
    def _up():
        c = _serpentine(i, s, FFN_NC)
        cols = pl.ds(pl.multiple_of(c * FFN_FC, FFN_FC), FFN_FC)
        a = a_ref[...]
        graw[0:CONV_HALO, :] = jnp.where(i > 0, carry[c], 0.0)
        graw[CONV_HALO:CONV_HALO + tm, :] = _dot(a, wg_ref[...])
        carry[c] = graw[tm:tm + CONV_HALO, :]
        cw = cw_ref[:, cols]
        g = graw[...]
        back = lambda k: pltpu.roll(g, k, axis=0)[CONV_HALO:, :]
        gc = cb_ref[:, cols] + back(2) * cw[0:1, :]
        gc = gc + back(1) * cw[1:2, :]
        gc = gc + g[CONV_HALO:, :] * cw[2:3, :]
        act_ref[:, cols] = (gc * _sigmoid(gc) * _dot(a, wv_ref[...])).astype(BF16)

    @pl.when(s >= FFN_NC)
    def _down():
        n = _serpentine(i, s - FFN_NC, FFN_NN)
        cols = pl.ds(pl.multiple_of(n * FFN_FN, FFN_FN), FFN_FN)
        o_ref[...] = h_ref[:, cols] + _dot(act_ref[...], wd_ref[...])


def _ffn(h, nw, w_up, conv_w, conv_b, w_down, layer, tm):
    L = h.shape[0]
    up_c = lambda i, s: _serpentine(i, s, FFN_NC)
    down_n = lambda i, s: _serpentine(i, s - FFN_NC, FFN_NN)
    return pl.pallas_call(
        functools.partial(_ffn_kernel, tm=tm),
        grid=(L // tm, FFN_NC + FFN_NN),
        in_specs=[pl.BlockSpec((tm, D_MODEL), lambda i, s: (i, 0)),
                  pl.BlockSpec((None, 1, D_MODEL), lambda i, s: (layer, 0, 0)),
                  pl.BlockSpec((None, None, D_MODEL, FFN_FC), lambda i, s: (layer, up_c(i, s), 0, 0)),
                  pl.BlockSpec((None, None, D_MODEL, FFN_FC),
                               lambda i, s: (layer, FFN_NC + up_c(i, s), 0, 0)),
                  pl.BlockSpec((None, 3, D_FF), lambda i, s: (layer, 0, 0)),
                  pl.BlockSpec((None, 1, D_FF), lambda i, s: (layer, 0, 0)),
                  pl.BlockSpec((None, None, D_FF, FFN_FN), lambda i, s: (layer, down_n(i, s), 0, 0))],
        out_specs=pl.BlockSpec((tm, FFN_FN), lambda i, s: (i, down_n(i, s))),
        out_shape=jax.ShapeDtypeStruct((L, D_MODEL), F32),
        scratch_shapes=[pltpu.VMEM((tm, D_MODEL), BF16),
                        pltpu.VMEM((tm, D_FF), BF16),
                        pltpu.VMEM((tm + CONV_HALO, FFN_FC), F32),
                        pltpu.VMEM((FFN_NC, CONV_HALO, FFN_FC), F32)],
        compiler_params=_cparams(2),
        name="ffn",
    )(h, nw, w_up, w_up, conv_w, conv_b, w_down)


def _ple_kernel(*refs, final):
    if final:
        h_ref, p_ref, nw_ref, wpg_ref, wple_ref, fw_ref, o_ref, a_ref = refs
    else:
        h_ref, p_ref, nw_ref, wpg_ref, wple_ref, o_ref, a_ref = refs
    a_ref[...] = _rms(h_ref[...], nw_ref[...]).astype(BF16)
    a = a_ref[...]
    pe = p_ref[...].astype(BF16)
    for c0, c1 in _col_chunks(D_MODEL):
        gate = _sigmoid(_dot(a, wpg_ref[:, c0:c1]))
        o_ref[:, c0:c1] = h_ref[:, c0:c1] + _dot(pe, wple_ref[:, c0:c1]) * gate
    if final:
        o_ref[...] = _rms(o_ref[...], fw_ref[...])


def _ple(h, p, nw, w_pg, w_ple, final_w, layer, tm):
    L = h.shape[0]
    final = final_w is not None
    in_specs = [pl.BlockSpec((tm, D_MODEL), lambda i: (i, 0)),
                pl.BlockSpec((None, tm, PLE_DIM), lambda i: (layer, i, 0)),
                pl.BlockSpec((None, 1, D_MODEL), lambda i: (layer, 0, 0)),
                pl.BlockSpec((None, D_MODEL, D_MODEL), lambda i: (layer, 0, 0),
                             pipeline_mode=pl.Buffered(1)),
                pl.BlockSpec((None, PLE_DIM, D_MODEL), lambda i: (layer, 0, 0),
                             pipeline_mode=pl.Buffered(1))]
    args = [h, p, nw, w_pg, w_ple]
    if final:
        in_specs.append(_resident((1, D_MODEL)))
        args.append(final_w)
    return pl.pallas_call(
        functools.partial(_ple_kernel, final=final),
        grid=(L // tm,),
        in_specs=in_specs,
        out_specs=pl.BlockSpec((tm, D_MODEL), lambda i: (i, 0)),
        out_shape=jax.ShapeDtypeStruct((L, D_MODEL), F32),
        scratch_shapes=[pltpu.VMEM((tm, D_MODEL), BF16)],
        compiler_params=_cparams(1),
        name="ple_final" if final else "ple",
    )(*args)


def _pack_w_in(w_in):
    u, zp, q, k, v, g, r = jnp.split(w_in.astype(BF16), [512, 1024, 1536, 2048, 3072, 3088], axis=-1)
    g = jnp.pad(g, ((0, 0), (0, 0), (0, LANES - GLA_GATE_RANK)))
    return jnp.concatenate([u, zp, q, k, v, r, g], axis=-1)


def _chunk_tril(tm):
    r = jnp.arange(tm)
    same = (r[:, None] // GLA_CHUNK) == (r[None, :] // GLA_CHUNK)
    return (same & (r[:, None] >= r[None, :])).astype(BF16)


def kernel(x, p, norm_mix_w, w_in, s5_a_re, s5_a_im, s5_log_dt, s5_b_re, s5_b_im, s5_c_re, s5_c_im, s5_d, s5_w_glu, s5_b_glu, pool_w, pool_scale, gla_w_a2, gla_b_a, gla_norm_w, w_out, norm_ffn_w, w_up, conv_w, conv_b, w_down, norm_ple_w, w_ple, w_pg, final_norm_w):
    bsz, L, _ = x.shape
    assert bsz == 1 and L % (S5_SEGMENTS * S5_TB) == 0
    depth = w_in.shape[0]
    tm = min(512, L)
    tm_mix = min(512, L)
    tri = _chunk_tril(tm_mix)
    w_up_c = w_up.astype(BF16).reshape(depth, D_MODEL, 2 * FFN_NC, FFN_FC).transpose(0, 2, 1, 3)
    w_down_c = w_down.astype(BF16).reshape(depth, D_FF, FFN_NN, FFN_FN).transpose(0, 2, 1, 3)
    nw_ffn = norm_ffn_w.reshape(depth, 1, D_MODEL)
    conv_b3 = conv_b.reshape(depth, 1, D_FF)
    w_in_p = _pack_w_in(w_in)
    w_out_b, w_pg_b, w_ple_b = w_out.astype(BF16), w_pg.astype(BF16), w_ple.astype(BF16)
    nw_mix = norm_mix_w.reshape(depth, 1, D_MODEL)
    nw_ple = norm_ple_w.reshape(depth, 1, D_MODEL)
    p3 = p.reshape(depth, L, PLE_DIM)
    h = x.reshape(L, D_MODEL)
    for i in range(depth):
        z = _inproj(h, nw_mix, w_in_p, i, tm)
        s5_prm = _s5_prepare(s5_a_re[i], s5_a_im[i], s5_log_dt[i], s5_b_re[i], s5_b_im[i], s5_c_re[i],
                             s5_c_im[i], s5_d[i], s5_w_glu[i], s5_b_glu[i], L // S5_SEGMENTS)
        y_s5 = _s5_mixer(z, s5_prm, S5_TB)
        mix_prm = {
            "pool_w": pool_w[i].astype(BF16), "pool_scale": pool_scale[i].reshape(1, POOL_WIDTH),
            "w_a2": jnp.pad(gla_w_a2[i], ((0, LANES - GLA_GATE_RANK), (0, 0))).astype(BF16),
            "b_a": gla_b_a[i].reshape(1, GLA_K_WIDTH), "gla_norm_w": gla_norm_w[i].reshape(1, GLA_DV),
            "tri": tri,
        }
        y_pool, y_gla = _mixers(z, mix_prm, tm_mix)
        h = _outproj(h, y_s5, y_pool, y_gla, w_out_b, i, tm)
        h = _ffn(h, nw_ffn, w_up_c, conv_w, conv_b3, w_down_c, i, tm)
        final_w = final_norm_w.reshape(1, D_MODEL) if i == depth - 1 else None
        h = _ple(h, p3, nw_ple, w_pg_b, w_ple_b, final_w, i, tm)
    return h.reshape(bsz, L, D_MODEL)
```

```python
import functools
import math

import jax
import jax.numpy as jnp
from jax import lax
from jax.experimental import pallas as pl
from jax.experimental.pallas import tpu as pltpu

F32 = jnp.float32
BF16 = jnp.bfloat16

D_MODEL = 2048
S5_WIDTH = 512
S5_GROUP = 16
S5_GROUPS = 32
S5_STATE = 64
S5_COLS = S5_GROUPS * S5_STATE
POOL_WIDTH = 512
POOL_WINDOWS = (2, 4, 8, 16)
POOL_GROUP = 128
GLA_HEADS = 4
GLA_DK = 128
GLA_DV = 256
GLA_K_WIDTH = 512
GLA_V_WIDTH = 1024
GLA_GATE_RANK = 16
GLA_TAU = 16.0
GLA_CHUNK = 64
D_FF = 5632
PLE_DIM = 256
EPS = 1e-6

LANES = 128
SUBLANES = 8
S5_SEGMENTS = SUBLANES
S5_BUNDLE = LANES // S5_GROUP
S5_NBUNDLES = S5_GROUPS // S5_BUNDLE
S5_BCOLS = S5_BUNDLE * S5_STATE
S5_TB = 64
POOL_HALO = 16

Z_S5, Z_POOL, Z_Q, Z_K, Z_V, Z_R, Z_G = 0, 512, 1024, 1536, 2048, 3072, 4096
Z_WIDTH = 4224
MM_TN = 512

VMEM_LIMIT = 56 * 1024 * 1024


def _cparams(n_axes):
    return pltpu.CompilerParams(dimension_semantics=("arbitrary",) * n_axes,
                                vmem_limit_bytes=VMEM_LIMIT)


def _resident(shape):
    nd = len(shape)
    return pl.BlockSpec(shape, lambda *_: (0,) * nd, pipeline_mode=pl.Buffered(1))


def _rms(x, w):
    ms = jnp.mean(x * x, axis=-1, keepdims=True)
    return x * lax.rsqrt(ms + EPS) * w


def _sigmoid(x):
    return 1.0 / (1.0 + jnp.exp(-x))


def _dot(a, b):
    return jnp.dot(a, b, preferred_element_type=F32)


def _dot_nt(a, b):
    return lax.dot_general(a, b, (((1,), (1,)), ((), ())), preferred_element_type=F32)


def _dot_tn(a, b):
    return lax.dot_general(a, b, (((0,), (0,)), ((), ())), preferred_element_type=F32)


def _col_chunks(n):
    return [(c0, min(c0 + MM_TN, n)) for c0 in range(0, n, MM_TN)]


def _rms_split(h, w, width):
    rinv = lax.rsqrt(jnp.mean(h * h, axis=-1, keepdims=True) + EPS)
    return (h * w).astype(BF16), jnp.broadcast_to(rinv, (h.shape[0], width))


def _inproj_kernel(h_ref, nw_ref, w_ref, z_ref, a_ref):
    a_ref[...], rinv = _rms_split(h_ref[...], nw_ref[...], MM_TN)
    a = a_ref[...]
    for c0, c1 in _col_chunks(Z_WIDTH):
        z_ref[:, c0:c1] = rinv[:, :c1 - c0] * _dot(a, w_ref[:, c0:c1])


def _inproj(h, nw, w, layer, tm):
    L = h.shape[0]
    return pl.pallas_call(
        _inproj_kernel,
        grid=(L // tm,),
        in_specs=[pl.BlockSpec((tm, D_MODEL), lambda i: (i, 0)),
                  pl.BlockSpec((None, 1, D_MODEL), lambda i: (layer, 0, 0)),
                  pl.BlockSpec((None, D_MODEL, Z_WIDTH), lambda i: (layer, 0, 0),
                               pipeline_mode=pl.Buffered(1))],
        out_specs=pl.BlockSpec((tm, Z_WIDTH), lambda i: (i, 0)),
        out_shape=jax.ShapeDtypeStruct((L, Z_WIDTH), F32),
        scratch_shapes=[pltpu.VMEM((tm, D_MODEL), BF16)],
        compiler_params=_cparams(1),
        name="inproj",
    )(h, nw, w)


def _gelu_tanh(x):
    return 0.5 * x * (1.0 + jnp.tanh(math.sqrt(2.0 / math.pi) * (x + 0.044715 * (x * x * x))))


def _s5_kernel(*refs, tb, pass2):
    if pass2:
        (u_ref, wb_ref, are_ref, aim_ref, xe_re_ref, xe_im_ref, ap_re_ref, ap_im_ref,
         wc_ref, d_ref, wglu_ref, bglu_ref, y_ref, ubuf, xs, st_re, st_im) = refs
    else:
        (u_ref, wb_ref, are_ref, aim_ref, xe_re_ref, xe_im_ref,
         ubuf, xs, st_re, st_im) = refs
    nseg = S5_SEGMENTS

    @pl.when(pl.program_id(0) == 0)
    def _init():
        if pass2:
            apr, api = ap_re_ref[...], ap_im_ref[...]
            st_re[0:1, :] = jnp.zeros((1, S5_COLS), F32)
            st_im[0:1, :] = jnp.zeros((1, S5_COLS), F32)
            for j in range(nseg - 1):
                cr, ci = st_re[j:j + 1, :], st_im[j:j + 1, :]
                st_re[j + 1:j + 2, :] = apr * cr - api * ci + xe_re_ref[j:j + 1, :]
                st_im[j + 1:j + 2, :] = apr * ci + api * cr + xe_im_ref[j:j + 1, :]
        else:
            st_re[...] = jnp.zeros((nseg, S5_COLS), F32)
            st_im[...] = jnp.zeros((nseg, S5_COLS), F32)

    for j in range(nseg):
        uj = u_ref[j]
        for b in range(S5_NBUNDLES):
            ubuf[b, pl.ds(j, tb, stride=nseg), :] = uj[:, b * LANES:(b + 1) * LANES]
    for b in range(S5_NBUNDLES):
        xs[:, 2 * b * S5_BCOLS:2 * (b + 1) * S5_BCOLS] = _dot(ubuf[b].astype(BF16), wb_ref[b])

    for b in range(S5_NBUNDLES):
        cols = slice(b * S5_BCOLS, (b + 1) * S5_BCOLS)
        cre = slice(2 * b * S5_BCOLS, (2 * b + 1) * S5_BCOLS)
        cim = slice((2 * b + 1) * S5_BCOLS, (2 * b + 2) * S5_BCOLS)
        ar = jnp.broadcast_to(are_ref[:, cols], (nseg, S5_BCOLS))
        ai = jnp.broadcast_to(aim_ref[:, cols], (nseg, S5_BCOLS))
        xr, xi = st_re[:, cols], st_im[:, cols]
        for t in range(tb):
            rows = slice(t * nseg, (t + 1) * nseg)
            xr, xi = (ar * xr - ai * xi + xs[rows, cre], ar * xi + ai * xr + xs[rows, cim])
            if pass2:
                xs[rows, cre] = xr
                xs[rows, cim] = xi
        st_re[:, cols] = xr
        st_im[:, cols] = xi

    if not pass2:
        xe_re_ref[...] = st_re[...]
        xe_im_ref[...] = st_im[...]
        return

    ys = [_dot(xs[:, 2 * b * S5_BCOLS:2 * (b + 1) * S5_BCOLS].astype(BF16), wc_ref[b])
          for b in range(S5_NBUNDLES)]
    u = jnp.concatenate([ubuf[b] for b in range(S5_NBUNDLES)], axis=1)
    y = jnp.concatenate(ys, axis=1) + d_ref[...] * u
    y = _gelu_tanh(y)
    glu = _sigmoid(_dot(y.astype(BF16), wglu_ref[...]) + bglu_ref[...])
    out = y * glu
    for b in range(S5_NBUNDLES):
        ubuf[b] = out[:, b * LANES:(b + 1) * LANES]
    for j in range(nseg):
        y_ref[j] = jnp.concatenate(
            [ubuf[b, pl.ds(j, tb, stride=nseg), :] for b in range(S5_NBUNDLES)], axis=1).astype(BF16)


def _s5_mixer(z, prm, tb):
    L = z.shape[0]
    nseg = S5_SEGMENTS
    seg_len = L // nseg
    nblk = seg_len // tb
    rows = tb * nseg
    z3 = z.reshape(nseg, seg_len, Z_WIDTH)
    u_spec = pl.BlockSpec((nseg, tb, S5_WIDTH), lambda i: (0, i, Z_S5 // S5_WIDTH))
    state_shape = jax.ShapeDtypeStruct((nseg, S5_COLS), F32)
    scratch = [pltpu.VMEM((S5_NBUNDLES, rows, LANES), F32),
               pltpu.VMEM((rows, 2 * S5_COLS), F32),
               pltpu.VMEM((nseg, S5_COLS), F32), pltpu.VMEM((nseg, S5_COLS), F32)]
    common = [u_spec, _resident(prm["wb"].shape), _resident((1, S5_COLS)), _resident((1, S5_COLS))]

    xe_re, xe_im = pl.pallas_call(
        functools.partial(_s5_kernel, tb=tb, pass2=False),
        grid=(nblk,),
        in_specs=common,
        out_specs=[_resident_out((nseg, S5_COLS)), _resident_out((nseg, S5_COLS))],
        out_shape=[state_shape, state_shape],
        scratch_shapes=scratch,
        compiler_params=_cparams(1),
        name="s5_states",
    )(z3, prm["wb"], prm["a_re"], prm["a_im"])

    y = pl.pallas_call(
        functools.partial(_s5_kernel, tb=tb, pass2=True),
        grid=(nblk,),
        in_specs=common + [_resident((nseg, S5_COLS)), _resident((nseg, S5_COLS)),
                           _resident((1, S5_COLS)), _resident((1, S5_COLS)),
                           _resident(prm["wc"].shape), _resident((1, S5_WIDTH)),
                           _resident((S5_WIDTH, S5_WIDTH)), _resident((1, S5_WIDTH))],
        out_specs=pl.BlockSpec((nseg, tb, S5_WIDTH), lambda i: (0, i, 0)),
        out_shape=jax.ShapeDtypeStruct((nseg, seg_len, S5_WIDTH), BF16),
        scratch_shapes=scratch,
        compiler_params=_cparams(1),
        name="s5_outputs",
    )(z3, prm["wb"], prm["a_re"], prm["a_im"], xe_re, xe_im, prm["ap_re"], prm["ap_im"],
      prm["wc"], prm["d"], prm["w_glu"], prm["b_glu"])
    return y.reshape(L, S5_WIDTH)


def _resident_out(shape):
    nd = len(shape)
    return pl.BlockSpec(shape, lambda *_: (0,) * nd)


def _s5_prepare(a_re, a_im, log_dt, b_re, b_im, c_re, c_im, d_skip, w_glu, b_glu, seg_len):
    dt = jnp.exp(log_dt)[:, None]
    mag = jnp.exp(a_re * dt)
    ab_re, ab_im = mag * jnp.cos(a_im * dt), mag * jnp.sin(a_im * dt)
    nr, ni = ab_re - 1.0, ab_im
    den = a_re * a_re + a_im * a_im
    f_re, f_im = (nr * a_re + ni * a_im) / den, (ni * a_re - nr * a_im) / den
    bb_re = f_re[..., None] * b_re - f_im[..., None] * b_im
    bb_im = f_re[..., None] * b_im + f_im[..., None] * b_re
    pr, pi = jnp.ones_like(ab_re), jnp.zeros_like(ab_re)
    sr, si, e = ab_re, ab_im, seg_len
    while e:
        if e & 1:
            pr, pi = pr * sr - pi * si, pr * si + pi * sr
        sr, si = sr * sr - si * si, 2.0 * sr * si
        e >>= 1
    eye = jnp.eye(S5_BUNDLE, dtype=F32)
    nb, gb = S5_NBUNDLES, S5_BUNDLE

    def b_slab(bb):
        return jnp.einsum("bgnc,gh->bgchn", bb.reshape(nb, gb, S5_STATE, S5_GROUP), eye).reshape(
            nb, LANES, S5_BCOLS)

    def c_slab(cm):
        return jnp.einsum("bgcn,gh->bgnhc", cm.reshape(nb, gb, S5_GROUP, S5_STATE), eye).reshape(
            nb, S5_BCOLS, LANES)

    return {
        "wb": jnp.concatenate([b_slab(bb_re), b_slab(bb_im)], axis=-1).astype(BF16),
        "wc": jnp.concatenate([c_slab(c_re), -c_slab(c_im)], axis=1).astype(BF16),
        "a_re": ab_re.reshape(1, S5_COLS), "a_im": ab_im.reshape(1, S5_COLS),
        "ap_re": pr.reshape(1, S5_COLS), "ap_im": pi.reshape(1, S5_COLS),
        "d": d_skip.reshape(1, S5_WIDTH), "w_glu": w_glu.astype(BF16),
        "b_glu": b_glu.reshape(1, S5_WIDTH),
    }


def _mix_kernel(zp_ref, q_ref, k_ref, v_ref, r_ref, g_ref, pw_ref, ps_ref, wa_ref, ba_ref, gnw_ref,
                tri_ref, ypool_ref, ygla_ref, zext, s_ref, *, tm):
    i = pl.program_id(0)

    @pl.when(i == 0)
    def _init():
        zext[0:POOL_HALO, :] = jnp.zeros((POOL_HALO, POOL_WIDTH), F32)
        s_ref[...] = jnp.zeros(s_ref.shape, F32)

    zext[POOL_HALO:POOL_HALO + tm, :] = zp_ref[...]
    pos = (i * tm + 1 + lax.broadcasted_iota(jnp.int32, (tm, 1), 0)).astype(F32)
    for gi, w in enumerate(POOL_WINDOWS):
        cols = slice(gi * POOL_GROUP, (gi + 1) * POOL_GROUP)
        ze = zext[:, cols]
        s, span = ze, 1
        while span < w:
            s = s + pltpu.roll(s, span, axis=0)
            span *= 2
        zc = ze[POOL_HALO:, :]
        pooled = s[POOL_HALO:, :] / jnp.minimum(pos, float(w)) - zc
        mixed = _dot(pooled.astype(BF16), pw_ref[gi]) * ps_ref[:, cols]
        ypool_ref[:, cols] = mixed.astype(BF16)
    zext[0:POOL_HALO, :] = zext[tm:tm + POOL_HALO, :]

    logit = _dot(g_ref[...].astype(BF16), wa_ref[...]) + ba_ref[...]
    la = (jnp.minimum(logit, 0.0) - jnp.log(1.0 + jnp.exp(-jnp.abs(logit)))) / GLA_TAU
    la_hi = la.astype(BF16)
    la_lo = (la - la_hi.astype(F32)).astype(BF16)
    tri = tri_ref[...]
    b = _dot(tri, la_hi) + _dot(tri, la_lo)
    nch = tm // GLA_CHUNK
    b_last = jnp.concatenate(
        [jnp.broadcast_to(b[(c + 1) * GLA_CHUNK - 1:(c + 1) * GLA_CHUNK, :], (GLA_CHUNK, GLA_K_WIDTH))
         for c in range(nch)], axis=0)
    q_dec = (q_ref[...] * (GLA_DK ** -0.5) * jnp.exp(b)).astype(BF16)
    k = k_ref[...]
    k_dec = (k * jnp.exp(-b)).astype(BF16)
    k_end = (k * jnp.exp(b_last - b)).astype(BF16)
    decay = jnp.exp(b_last)
    causal = (lax.broadcasted_iota(jnp.int32, (GLA_CHUNK, GLA_CHUNK), 0)
              >= lax.broadcasted_iota(jnp.int32, (GLA_CHUNK, GLA_CHUNK), 1))
    gnw = gnw_ref[...]
    units = [(hd, c) for hd in range(GLA_HEADS) for c in range(nch)]
    rows = lambda c: slice(c * GLA_CHUNK, (c + 1) * GLA_CHUNK)
    kcol = lambda hd: slice(hd * GLA_DK, (hd + 1) * GLA_DK)
    vcol = lambda hd: slice(hd * GLA_DV, (hd + 1) * GLA_DV)
    vv = {(hd, c): v_ref[rows(c), vcol(hd)].astype(BF16) for hd, c in units}
    scores = {(hd, c): _dot_nt(q_dec[rows(c), kcol(hd)], k_dec[rows(c), kcol(hd)]) for hd, c in units}
    kv = {(hd, c): _dot_tn(vv[hd, c], k_end[rows(c), kcol(hd)]) for hd, c in units}
    st_in = {}
    for hd in range(GLA_HEADS):
        st = s_ref[hd]
        for c in range(nch):
            st_in[hd, c] = st.astype(BF16)
            st = decay[c * GLA_CHUNK:c * GLA_CHUNK + 1, kcol(hd)] * st + kv[hd, c]
        s_ref[hd] = st
    for hd, c in units:
        sc = jnp.where(causal, scores[hd, c], 0.0).astype(BF16)
        o = _dot(sc, vv[hd, c]) + _dot_nt(q_dec[rows(c), kcol(hd)], st_in[hd, c])
        o = _rms(o, gnw)
        rr = r_ref[rows(c), vcol(hd)]
        ygla_ref[rows(c), vcol(hd)] = (o * (rr * _sigmoid(rr))).astype(BF16)


def _mixers(z, prm, tm):
    L = z.shape[0]
    row = lambda w, col: pl.BlockSpec((tm, w), lambda i: (i, col // w))
    return pl.pallas_call(
        functools.partial(_mix_kernel, tm=tm),
        grid=(L // tm,),
        in_specs=[row(POOL_WIDTH, Z_POOL), row(GLA_K_WIDTH, Z_Q), row(GLA_K_WIDTH, Z_K),
                  row(GLA_V_WIDTH, Z_V), row(GLA_V_WIDTH, Z_R), row(LANES, Z_G),
                  _resident(prm["pool_w"].shape), _resident((1, POOL_WIDTH)),
                  _resident((LANES, GLA_K_WIDTH)), _resident((1, GLA_K_WIDTH)),
                  _resident((1, GLA_DV)), _resident((tm, tm))],
        out_specs=[pl.BlockSpec((tm, POOL_WIDTH), lambda i: (i, 0)),
                   pl.BlockSpec((tm, GLA_V_WIDTH), lambda i: (i, 0))],
        out_shape=[jax.ShapeDtypeStruct((L, POOL_WIDTH), BF16),
                   jax.ShapeDtypeStruct((L, GLA_V_WIDTH), BF16)],
        scratch_shapes=[pltpu.VMEM((tm + POOL_HALO, POOL_WIDTH), F32),
                        pltpu.VMEM((GLA_HEADS, GLA_DV, GLA_DK), F32)],
        compiler_params=_cparams(1),
        name="mixers",
    )(z, z, z, z, z, z, prm["pool_w"], prm["pool_scale"], prm["w_a2"], prm["b_a"], prm["gla_norm_w"],
      prm["tri"])


def _outproj_kernel(h_ref, ys_ref, yp_ref, yg_ref, w_ref, o_ref, m_ref):
    m_ref[:, 0:S5_WIDTH] = ys_ref[...]
    m_ref[:, S5_WIDTH:S5_WIDTH + POOL_WIDTH] = yp_ref[...]
    m_ref[:, S5_WIDTH + POOL_WIDTH:] = yg_ref[...]
    m = m_ref[...]
    for c0, c1 in _col_chunks(D_MODEL):
        o_ref[:, c0:c1] = h_ref[:, c0:c1] + _dot(m, w_ref[:, c0:c1])


def _outproj(h, ys, yp, yg, w, layer, tm):
    L = h.shape[0]
    row = lambda wd: pl.BlockSpec((tm, wd), lambda i: (i, 0))
    return pl.pallas_call(
        _outproj_kernel,
        grid=(L // tm,),
        in_specs=[row(D_MODEL), row(S5_WIDTH), row(POOL_WIDTH), row(GLA_V_WIDTH),
                  pl.BlockSpec((None, D_MODEL, D_MODEL), lambda i: (layer, 0, 0),
                               pipeline_mode=pl.Buffered(1))],
        out_specs=row(D_MODEL),
        out_shape=jax.ShapeDtypeStruct((L, D_MODEL), F32),
        scratch_shapes=[pltpu.VMEM((tm, D_MODEL), BF16)],
        compiler_params=_cparams(1),
        name="outproj",
    )(h, ys, yp, yg, w)


FFN_FC = 512
FFN_NC = D_FF // FFN_FC
FFN_FN = 512
FFN_NN = D_MODEL // FFN_FN
CONV_HALO = SUBLANES


def _serpentine(i, k, n):
    k = jnp.clip(k, 0, n - 1)
    return jnp.where(i % 2 == 0, k, n - 1 - k)


def _ffn_kernel(h_ref, nw_ref, wg_ref, wv_ref, cw_ref, cb_ref, wd_ref, o_ref, a_ref, act_ref, graw, carry,
                *, tm):
    i, s = pl.program_id(0), pl.program_id(1)

    @pl.when(s == 0)
    def _():
        a_ref[...] = _rms(h_ref[...], nw_ref[...]).astype(BF16)

    @pl.when(s < FFN_NC)
    def _up():
        c = _serpentine(i, s, FFN_NC)
        cols = pl.ds(pl.multiple_of(c * FFN_FC, FFN_FC), FFN_FC)
        a = a_ref[...]
        graw[0:CONV_HALO, :] = jnp.where(i > 0, carry[c], 0.0)
        graw[CONV_HALO:CONV_HALO + tm, :] = _dot(a, wg_ref[...])
        carry[c] = graw[tm:tm + CONV_HALO, :]
        cw = cw_ref[:, cols]
        g = graw[...]
        back = lambda k: pltpu.roll(g, k, axis=0)[CONV_HALO:, :]
        gc = cb_ref[:, cols] + back(2) * cw[0:1, :]
        gc = gc + back(1) * cw[1:2, :]
        gc = gc + g[CONV_HALO:, :] * cw[2:3, :]
        act_ref[:, cols] = (gc * _sigmoid(gc) * _dot(a, wv_ref[...])).astype(BF16)

    @pl.when(s >= FFN_NC)
    def _down():
        n = _serpentine(i, s - FFN_NC, FFN_NN)
        cols = pl.ds(pl.multiple_of(n * FFN_FN, FFN_FN), FFN_FN)
        o_ref[...] = h_ref[:, cols] + _dot(act_ref[...], wd_ref[...])


def _ffn(h, nw, w_up, conv_w, conv_b, w_down, layer, tm):
    L = h.shape[0]
    ntiles = L // tm
    up_c = lambda i, s: _serpentine(i, s, FFN_NC)
    down_n = lambda i, s: _serpentine(i, s - FFN_NC, FFN_NN)
    return pl.pallas_call(
        functools.partial(_ffn_kernel, tm=tm),
        grid=(ntiles, FFN_NC + FFN_NN),
        in_specs=[pl.BlockSpec((tm, D_MODEL), lambda i, s: (i, 0)),
                  pl.BlockSpec((None, 1, D_MODEL), lambda i, s: (layer, 0, 0)),
                  pl.BlockSpec((None, D_MODEL, FFN_FC), lambda i, s: (layer, 0, up_c(i, s))),
                  pl.BlockSpec((None, D_MODEL, FFN_FC), lambda i, s: (layer, 0, FFN_NC + up_c(i, s))),
                  pl.BlockSpec((None, 3, D_FF), lambda i, s: (layer, 0, 0)),
                  pl.BlockSpec((None, 1, D_FF), lambda i, s: (layer, 0, 0)),
                  pl.BlockSpec((None, D_FF, FFN_FN), lambda i, s: (layer, 0, down_n(i, s)))],
        out_specs=pl.BlockSpec((tm, FFN_FN), lambda i, s: (i, down_n(i, s))),
        out_shape=jax.ShapeDtypeStruct((L, D_MODEL), F32),
        scratch_shapes=[pltpu.VMEM((tm, D_MODEL), BF16),
                        pltpu.VMEM((tm, D_FF), BF16),
                        pltpu.VMEM((tm + CONV_HALO, FFN_FC), F32),
                        pltpu.VMEM((FFN_NC, CONV_HALO, FFN_FC), F32)],
        compiler_params=_cparams(2),
        name="ffn",
    )(h, nw, w_up, w_up, conv_w, conv_b, w_down)


def _ple_kernel(*refs, final):
    if final:
        h_ref, p_ref, nw_ref, wpg_ref, wple_ref, fw_ref, o_ref, a_ref = refs
    else:
        h_ref, p_ref, nw_ref, wpg_ref, wple_ref, o_ref, a_ref = refs
    a_ref[...], rinv = _rms_split(h_ref[...], nw_ref[...], MM_TN)
    a = a_ref[...]
    pe = p_ref[...].astype(BF16)
    for c0, c1 in _col_chunks(D_MODEL):
        gate = _sigmoid(rinv * _dot(a, wpg_ref[:, c0:c1]))
        o_ref[:, c0:c1] = h_ref[:, c0:c1] + _dot(pe, wple_ref[:, c0:c1]) * gate
    if final:
        o_ref[...] = _rms(o_ref[...], fw_ref[...])


def _ple(h, p, nw, w_pg, w_ple, final_w, layer, tm):
    L = h.shape[0]
    final = final_w is not None
    in_specs = [pl.BlockSpec((tm, D_MODEL), lambda i: (i, 0)),
                pl.BlockSpec((None, tm, PLE_DIM), lambda i: (layer, i, 0)),
                pl.BlockSpec((None, 1, D_MODEL), lambda i: (layer, 0, 0)),
                pl.BlockSpec((None, D_MODEL, D_MODEL), lambda i: (layer, 0, 0),
                             pipeline_mode=pl.Buffered(1)),
                pl.BlockSpec((None, PLE_DIM, D_MODEL), lambda i: (layer, 0, 0),
                             pipeline_mode=pl.Buffered(1))]
    args = [h, p, nw, w_pg, w_ple]
    if final:
        in_specs.append(_resident((1, D_MODEL)))
        args.append(final_w)
    return pl.pallas_call(
        functools.partial(_ple_kernel, final=final),
        grid=(L // tm,),
        in_specs=in_specs,
        out_specs=pl.BlockSpec((tm, D_MODEL), lambda i: (i, 0)),
        out_shape=jax.ShapeDtypeStruct((L, D_MODEL), F32),
        scratch_shapes=[pltpu.VMEM((tm, D_MODEL), BF16)],
        compiler_params=_cparams(1),
        name="ple_final" if final else "ple",
    )(*args)


def _pack_w_in(w_in):
    u, zp, q, k, v, g, r = jnp.split(w_in.astype(BF16), [512, 1024, 1536, 2048, 3072, 3088], axis=-1)
    g = jnp.pad(g, ((0, 0), (0, 0), (0, LANES - GLA_GATE_RANK)))
    return jnp.concatenate([u, zp, q, k, v, r, g], axis=-1)


def _chunk_tril(tm):
    r = jnp.arange(tm)
    same = (r[:, None] // GLA_CHUNK) == (r[None, :] // GLA_CHUNK)
    return (same & (r[:, None] >= r[None, :])).astype(BF16)


def kernel(x, p, norm_mix_w, w_in, s5_a_re, s5_a_im, s5_log_dt, s5_b_re, s5_b_im, s5_c_re, s5_c_im, s5_d, s5_w_glu, s5_b_glu, pool_w, pool_scale, gla_w_a2, gla_b_a, gla_norm_w, w_out, norm_ffn_w, w_up, conv_w, conv_b, w_down, norm_ple_w, w_ple, w_pg, final_norm_w):
    bsz, L, _ = x.shape
    assert bsz == 1 and L % (S5_SEGMENTS * S5_TB) == 0
    depth = w_in.shape[0]
    tm = min(512, L)
    tm_mix = min(512, L)
    tri = _chunk_tril(tm_mix)
    w_up_c, w_down_c = w_up.astype(BF16), w_down.astype(BF16)
    nw_ffn = norm_ffn_w.reshape(depth, 1, D_MODEL)
    conv_b3 = conv_b.reshape(depth, 1, D_FF)
    w_in_p = _pack_w_in(w_in)
    w_out_b, w_pg_b, w_ple_b = w_out.astype(BF16), w_pg.astype(BF16), w_ple.astype(BF16)
    nw_mix = norm_mix_w.reshape(depth, 1, D_MODEL)
    nw_ple = norm_ple_w.reshape(depth, 1, D_MODEL)
    p3 = p.reshape(depth, L, PLE_DIM)
    h = x.reshape(L, D_MODEL)
    for i in range(depth):
        z = _inproj(h, nw_mix, w_in_p, i, tm)
        s5_prm = _s5_prepare(s5_a_re[i], s5_a_im[i], s5_log_dt[i], s5_b_re[i], s5_b_im[i], s5_c_re[i],
                             s5_c_im[i], s5_d[i], s5_w_glu[i], s5_b_glu[i], L // S5_SEGMENTS)
        y_s5 = _s5_mixer(z, s5_prm, S5_TB)
        mix_prm = {
            "pool_w": pool_w[i].astype(BF16), "pool_scale": pool_scale[i].reshape(1, POOL_WIDTH),
            "w_a2": jnp.pad(gla_w_a2[i], ((0, LANES - GLA_GATE_RANK), (0, 0))).astype(BF16),
            "b_a": gla_b_a[i].reshape(1, GLA_K_WIDTH), "gla_norm_w": gla_norm_w[i].reshape(1, GLA_DV),
            "tri": tri,
        }
        y_pool, y_gla = _mixers(z, mix_prm, tm_mix)
        h = _outproj(h, y_s5, y_pool, y_gla, w_out_b, i, tm)
        h = _ffn(h, nw_ffn, w_up_c, conv_w, conv_b3, w_down_c, i, tm)
        final_w = final_norm_w.reshape(1, D_MODEL) if i == depth - 1 else None
        h = _ple(h, p3, nw_ple, w_pg_b, w_ple_b, final_w, i, tm)
    return h.reshape(bsz, L, D_MODEL)
```

```python
import functools
import math

import jax
import jax.numpy as jnp
from jax import lax
from jax.experimental import pallas as pl
from jax.experimental.pallas import tpu as pltpu

F32 = jnp.float32
BF16 = jnp.bfloat16

D_MODEL = 2048
S5_WIDTH = 512
S5_GROUP = 16
S5_GROUPS = 32
S5_STATE = 64
S5_COLS = S5_GROUPS * S5_STATE
POOL_WIDTH = 512
POOL_WINDOWS = (2, 4, 8, 16)
POOL_GROUP = 128
GLA_HEADS = 4
GLA_DK = 128
GLA_DV = 256
GLA_K_WIDTH = 512
GLA_V_WIDTH = 1024
GLA_GATE_RANK = 16
GLA_TAU = 16.0
GLA_CHUNK = 64
D_FF = 5632
PLE_DIM = 256
EPS = 1e-6

LANES = 128
SUBLANES = 8
S5_SEGMENTS = SUBLANES
S5_BUNDLE = LANES // S5_GROUP
S5_NBUNDLES = S5_GROUPS // S5_BUNDLE
S5_BCOLS = S5_BUNDLE * S5_STATE
S5_TB = 64
POOL_HALO = 16

Z_S5, Z_POOL, Z_Q, Z_K, Z_V, Z_R, Z_G = 0, 512, 1024, 1536, 2048, 3072, 4096
Z_WIDTH = 4224
MM_TN = 512

VMEM_LIMIT = 56 * 1024 * 1024


def _cparams(n_axes):
    return pltpu.CompilerParams(dimension_semantics=("arbitrary",) * n_axes,
                                vmem_limit_bytes=VMEM_LIMIT)


def _resident(shape):
    nd = len(shape)
    return pl.BlockSpec(shape, lambda *_: (0,) * nd, pipeline_mode=pl.Buffered(1))


def _rms(x, w):
    ms = jnp.mean(x * x, axis=-1, keepdims=True)
    return x * lax.rsqrt(ms + EPS) * w


def _sigmoid(x):
    return 1.0 / (1.0 + jnp.exp(-x))


def _dot(a, b):
    return jnp.dot(a, b, preferred_element_type=F32)


def _dot_nt(a, b):
    return lax.dot_general(a, b, (((1,), (1,)), ((), ())), preferred_element_type=F32)


def _dot_tn(a, b):
    return lax.dot_general(a, b, (((0,), (0,)), ((), ())), preferred_element_type=F32)


def _col_chunks(n):
    return [(c0, min(c0 + MM_TN, n)) for c0 in range(0, n, MM_TN)]


def _rms_split(h, w, width):
    rinv = lax.rsqrt(jnp.mean(h * h, axis=-1, keepdims=True) + EPS)
    return (h * w).astype(BF16), jnp.broadcast_to(rinv, (h.shape[0], width))


def _inproj_kernel(h_ref, nw_ref, w_ref, z_ref, a_ref):
    a_ref[...], rinv = _rms_split(h_ref[...], nw_ref[...], MM_TN)
    a = a_ref[...]
    for c0, c1 in _col_chunks(Z_WIDTH):
        z_ref[:, c0:c1] = rinv[:, :c1 - c0] * _dot(a, w_ref[:, c0:c1])


def _inproj(h, nw, w, layer, tm):
    L = h.shape[0]
    return pl.pallas_call(
        _inproj_kernel,
        grid=(L // tm,),
        in_specs=[pl.BlockSpec((tm, D_MODEL), lambda i: (i, 0)),
                  pl.BlockSpec((None, 1, D_MODEL), lambda i: (layer, 0, 0)),
                  pl.BlockSpec((None, D_MODEL, Z_WIDTH), lambda i: (layer, 0, 0),
                               pipeline_mode=pl.Buffered(1))],
        out_specs=pl.BlockSpec((tm, Z_WIDTH), lambda i: (i, 0)),
        out_shape=jax.ShapeDtypeStruct((L, Z_WIDTH), F32),
        scratch_shapes=[pltpu.VMEM((tm, D_MODEL), BF16)],
        compiler_params=_cparams(1),
        name="inproj",
    )(h, nw, w)


def _gelu_tanh(x):
    return 0.5 * x * (1.0 + jnp.tanh(math.sqrt(2.0 / math.pi) * (x + 0.044715 * (x * x * x))))


def _s5_kernel(*refs, tb, pass2):
    if pass2:
        (u_ref, wb_ref, are_ref, aim_ref, xe_re_ref, xe_im_ref, ap_re_ref, ap_im_ref,
         wc_ref, d_ref, wglu_ref, bglu_ref, y_ref, ubuf, xs, st_re, st_im) = refs
    else:
        (u_ref, wb_ref, are_ref, aim_ref, xe_re_ref, xe_im_ref,
         ubuf, xs, st_re, st_im) = refs
    nseg = S5_SEGMENTS

    @pl.when(pl.program_id(0) == 0)
    def _init():
        if pass2:
            apr, api = ap_re_ref[...], ap_im_ref[...]
            st_re[0:1, :] = jnp.zeros((1, S5_COLS), F32)
            st_im[0:1, :] = jnp.zeros((1, S5_COLS), F32)
            for j in range(nseg - 1):
                cr, ci = st_re[j:j + 1, :], st_im[j:j + 1, :]
                st_re[j + 1:j + 2, :] = apr * cr - api * ci + xe_re_ref[j:j + 1, :]
                st_im[j + 1:j + 2, :] = apr * ci + api * cr + xe_im_ref[j:j + 1, :]
        else:
            st_re[...] = jnp.zeros((nseg, S5_COLS), F32)
            st_im[...] = jnp.zeros((nseg, S5_COLS), F32)

    for j in range(nseg):
        uj = u_ref[j]
        for b in range(S5_NBUNDLES):
            ubuf[b, pl.ds(j, tb, stride=nseg), :] = uj[:, b * LANES:(b + 1) * LANES]
    for b in range(S5_NBUNDLES):
        xs[:, 2 * b * S5_BCOLS:2 * (b + 1) * S5_BCOLS] = _dot(ubuf[b].astype(BF16), wb_ref[b])

    for b in range(S5_NBUNDLES):
        cols = slice(b * S5_BCOLS, (b + 1) * S5_BCOLS)
        cre = slice(2 * b * S5_BCOLS, (2 * b + 1) * S5_BCOLS)
        cim = slice((2 * b + 1) * S5_BCOLS, (2 * b + 2) * S5_BCOLS)
        ar = jnp.broadcast_to(are_ref[:, cols], (nseg, S5_BCOLS))
        ai = jnp.broadcast_to(aim_ref[:, cols], (nseg, S5_BCOLS))
        xr, xi = st_re[:, cols], st_im[:, cols]
        for t in range(tb):
            rows = slice(t * nseg, (t + 1) * nseg)
            xr, xi = (ar * xr - ai * xi + xs[rows, cre], ar * xi + ai * xr + xs[rows, cim])
            if pass2:
                xs[rows, cre] = xr
                xs[rows, cim] = xi
        st_re[:, cols] = xr
        st_im[:, cols] = xi

    if not pass2:
        xe_re_ref[...] = st_re[...]
        xe_im_ref[...] = st_im[...]
        return

    ys = [_dot(xs[:, 2 * b * S5_BCOLS:2 * (b + 1) * S5_BCOLS].astype(BF16), wc_ref[b])
          for b in range(S5_NBUNDLES)]
    u = jnp.concatenate([ubuf[b] for b in range(S5_NBUNDLES)], axis=1)
    y = jnp.concatenate(ys, axis=1) + d_ref[...] * u
    y = _gelu_tanh(y)
    glu = _sigmoid(_dot(y.astype(BF16), wglu_ref[...]) + bglu_ref[...])
    out = y * glu
    for b in range(S5_NBUNDLES):
        ubuf[b] = out[:, b * LANES:(b + 1) * LANES]
    for j in range(nseg):
        y_ref[j] = jnp.concatenate(
            [ubuf[b, pl.ds(j, tb, stride=nseg), :] for b in range(S5_NBUNDLES)], axis=1).astype(BF16)


def _s5_mixer(z, prm, tb):
    L = z.shape[0]
    nseg = S5_SEGMENTS
    seg_len = L // nseg
    nblk = seg_len // tb
    rows = tb * nseg
    z3 = z.reshape(nseg, seg_len, Z_WIDTH)
    u_spec = pl.BlockSpec((nseg, tb, S5_WIDTH), lambda i: (0, i, Z_S5 // S5_WIDTH))
    state_shape = jax.ShapeDtypeStruct((nseg, S5_COLS), F32)
    scratch = [pltpu.VMEM((S5_NBUNDLES, rows, LANES), F32),
               pltpu.VMEM((rows, 2 * S5_COLS), F32),
               pltpu.VMEM((nseg, S5_COLS), F32), pltpu.VMEM((nseg, S5_COLS), F32)]
    common = [u_spec, _resident(prm["wb"].shape), _resident((1, S5_COLS)), _resident((1, S5_COLS))]

    xe_re, xe_im = pl.pallas_call(
        functools.partial(_s5_kernel, tb=tb, pass2=False),
        grid=(nblk,),
        in_specs=common,
        out_specs=[_resident_out((nseg, S5_COLS)), _resident_out((nseg, S5_COLS))],
        out_shape=[state_shape, state_shape],
        scratch_shapes=scratch,
        compiler_params=_cparams(1),
        name="s5_states",
    )(z3, prm["wb"], prm["a_re"], prm["a_im"])

    y = pl.pallas_call(
        functools.partial(_s5_kernel, tb=tb, pass2=True),
        grid=(nblk,),
        in_specs=common + [_resident((nseg, S5_COLS)), _resident((nseg, S5_COLS)),
                           _resident((1, S5_COLS)), _resident((1, S5_COLS)),
                           _resident(prm["wc"].shape), _resident((1, S5_WIDTH)),
                           _resident((S5_WIDTH, S5_WIDTH)), _resident((1, S5_WIDTH))],
        out_specs=pl.BlockSpec((nseg, tb, S5_WIDTH), lambda i: (0, i, 0)),
        out_shape=jax.ShapeDtypeStruct((nseg, seg_len, S5_WIDTH), BF16),
        scratch_shapes=scratch,
        compiler_params=_cparams(1),
        name="s5_outputs",
    )(z3, prm["wb"], prm["a_re"], prm["a_im"], xe_re, xe_im, prm["ap_re"], prm["ap_im"],
      prm["wc"], prm["d"], prm["w_glu"], prm["b_glu"])
    return y.reshape(L, S5_WIDTH)


def _resident_out(shape):
    nd = len(shape)
    return pl.BlockSpec(shape, lambda *_: (0,) * nd)


def _s5_prepare(a_re, a_im, log_dt, b_re, b_im, c_re, c_im, d_skip, w_glu, b_glu, seg_len):
    dt = jnp.exp(log_dt)[:, None]
    mag = jnp.exp(a_re * dt)
    ab_re, ab_im = mag * jnp.cos(a_im * dt), mag * jnp.sin(a_im * dt)
    nr, ni = ab_re - 1.0, ab_im
    den = a_re * a_re + a_im * a_im
    f_re, f_im = (nr * a_re + ni * a_im) / den, (ni * a_re - nr * a_im) / den
    bb_re = f_re[..., None] * b_re - f_im[..., None] * b_im
    bb_im = f_re[..., None] * b_im + f_im[..., None] * b_re
    pr, pi = jnp.ones_like(ab_re), jnp.zeros_like(ab_re)
    sr, si, e = ab_re, ab_im, seg_len
    while e:
        if e & 1:
            pr, pi = pr * sr - pi * si, pr * si + pi * sr
        sr, si = sr * sr - si * si, 2.0 * sr * si
        e >>= 1
    eye = jnp.eye(S5_BUNDLE, dtype=F32)
    nb, gb = S5_NBUNDLES, S5_BUNDLE

    def b_slab(bb):
        return jnp.einsum("bgnc,gh->bgchn", bb.reshape(nb, gb, S5_STATE, S5_GROUP), eye).reshape(
            nb, LANES, S5_BCOLS)

    def c_slab(cm):
        return jnp.einsum("bgcn,gh->bgnhc", cm.reshape(nb, gb, S5_GROUP, S5_STATE), eye).reshape(
            nb, S5_BCOLS, LANES)

    return {
        "wb": jnp.concatenate([b_slab(bb_re), b_slab(bb_im)], axis=-1).astype(BF16),
        "wc": jnp.concatenate([c_slab(c_re), -c_slab(c_im)], axis=1).astype(BF16),
        "a_re": ab_re.reshape(1, S5_COLS), "a_im": ab_im.reshape(1, S5_COLS),
        "ap_re": pr.reshape(1, S5_COLS), "ap_im": pi.reshape(1, S5_COLS),
        "d": d_skip.reshape(1, S5_WIDTH), "w_glu": w_glu.astype(BF16),
        "b_glu": b_glu.reshape(1, S5_WIDTH),
    }


def _mix_kernel(zp_ref, q_ref, k_ref, v_ref, r_ref, g_ref, pw_ref, ps_ref, wa_ref, ba_ref, gnw_ref,
                tri_ref, ypool_ref, ygla_ref, zext, s_ref, *, tm):
    i = pl.program_id(0)

    @pl.when(i == 0)
    def _init():
        zext[0:POOL_HALO, :] = jnp.zeros((POOL_HALO, POOL_WIDTH), F32)
        s_ref[...] = jnp.zeros(s_ref.shape, F32)

    zext[POOL_HALO:POOL_HALO + tm, :] = zp_ref[...]
    pos = (i * tm + 1 + lax.broadcasted_iota(jnp.int32, (tm, 1), 0)).astype(F32)
    for gi, w in enumerate(POOL_WINDOWS):
        cols = slice(gi * POOL_GROUP, (gi + 1) * POOL_GROUP)
        ze = zext[:, cols]
        s, span = ze, 1
        while span < w:
            s = s + pltpu.roll(s, span, axis=0)
            span *= 2
        zc = ze[POOL_HALO:, :]
        pooled = s[POOL_HALO:, :] / jnp.minimum(pos, float(w)) - zc
        mixed = _dot(pooled.astype(BF16), pw_ref[gi]) * ps_ref[:, cols]
        ypool_ref[:, cols] = mixed.astype(BF16)
    zext[0:POOL_HALO, :] = zext[tm:tm + POOL_HALO, :]

    logit = _dot(g_ref[...].astype(BF16), wa_ref[...]) + ba_ref[...]
    la = (jnp.minimum(logit, 0.0) - jnp.log(1.0 + jnp.exp(-jnp.abs(logit)))) / GLA_TAU
    la_hi = la.astype(BF16)
    la_lo = (la - la_hi.astype(F32)).astype(BF16)
    tri = tri_ref[...]
    b = _dot(tri, la_hi) + _dot(tri, la_lo)
    nch = tm // GLA_CHUNK
    b_last = jnp.concatenate(
        [jnp.broadcast_to(b[(c + 1) * GLA_CHUNK - 1:(c + 1) * GLA_CHUNK, :], (GLA_CHUNK, GLA_K_WIDTH))
         for c in range(nch)], axis=0)
    q_dec = (q_ref[...] * (GLA_DK ** -0.5) * jnp.exp(b)).astype(BF16)
    k = k_ref[...]
    k_dec = (k * jnp.exp(-b)).astype(BF16)
    k_end = (k * jnp.exp(b_last - b)).astype(BF16)
    decay = jnp.exp(b_last)
    causal = (lax.broadcasted_iota(jnp.int32, (GLA_CHUNK, GLA_CHUNK), 0)
              >= lax.broadcasted_iota(jnp.int32, (GLA_CHUNK, GLA_CHUNK), 1))
    gnw = gnw_ref[...]
    units = [(hd, c) for hd in range(GLA_HEADS) for c in range(nch)]
    rows = lambda c: slice(c * GLA_CHUNK, (c + 1) * GLA_CHUNK)
    kcol = lambda hd: slice(hd * GLA_DK, (hd + 1) * GLA_DK)
    vcol = lambda hd: slice(hd * GLA_DV, (hd + 1) * GLA_DV)
    vv = {(hd, c): v_ref[rows(c), vcol(hd)].astype(BF16) for hd, c in units}
    scores = {(hd, c): _dot_nt(q_dec[rows(c), kcol(hd)], k_dec[rows(c), kcol(hd)]) for hd, c in units}
    kv = {(hd, c): _dot_tn(vv[hd, c], k_end[rows(c), kcol(hd)]) for hd, c in units}
    st_in = {}
    for hd in range(GLA_HEADS):
        st = s_ref[hd]
        for c in range(nch):
            st_in[hd, c] = st.astype(BF16)
            st = decay[c * GLA_CHUNK:c * GLA_CHUNK + 1, kcol(hd)] * st + kv[hd, c]
        s_ref[hd] = st
    for hd, c in units:
        sc = jnp.where(causal, scores[hd, c], 0.0).astype(BF16)
        o = _dot(sc, vv[hd, c]) + _dot_nt(q_dec[rows(c), kcol(hd)], st_in[hd, c])
        o = _rms(o, gnw)
        rr = r_ref[rows(c), vcol(hd)]
        ygla_ref[rows(c), vcol(hd)] = (o * (rr * _sigmoid(rr))).astype(BF16)


def _mixers(z, prm, tm):
    L = z.shape[0]
    row = lambda w, col: pl.BlockSpec((tm, w), lambda i: (i, col // w))
    return pl.pallas_call(
        functools.partial(_mix_kernel, tm=tm),
        grid=(L // tm,),
        in_specs=[row(POOL_WIDTH, Z_POOL), row(GLA_K_WIDTH, Z_Q), row(GLA_K_WIDTH, Z_K),
                  row(GLA_V_WIDTH, Z_V), row(GLA_V_WIDTH, Z_R), row(LANES, Z_G),
                  _resident(prm["pool_w"].shape), _resident((1, POOL_WIDTH)),
                  _resident((LANES, GLA_K_WIDTH)), _resident((1, GLA_K_WIDTH)),
                  _resident((1, GLA_DV)), _resident((tm, tm))],
        out_specs=[pl.BlockSpec((tm, POOL_WIDTH), lambda i: (i, 0)),
                   pl.BlockSpec((tm, GLA_V_WIDTH), lambda i: (i, 0))],
        out_shape=[jax.ShapeDtypeStruct((L, POOL_WIDTH), BF16),
                   jax.ShapeDtypeStruct((L, GLA_V_WIDTH), BF16)],
        scratch_shapes=[pltpu.VMEM((tm + POOL_HALO, POOL_WIDTH), F32),
                        pltpu.VMEM((GLA_HEADS, GLA_DV, GLA_DK), F32)],
        compiler_params=_cparams(1),
        name="mixers",
    )(z, z, z, z, z, z, prm["pool_w"], prm["pool_scale"], prm["w_a2"], prm["b_a"], prm["gla_norm_w"],
      prm["tri"])


def _outproj_kernel(h_ref, ys_ref, yp_ref, yg_ref, w_ref, nw_ref, o_ref, a_ref, m_ref):
    m_ref[:, 0:S5_WIDTH] = ys_ref[...]
    m_ref[:, S5_WIDTH:S5_WIDTH + POOL_WIDTH] = yp_ref[...]
    m_ref[:, S5_WIDTH + POOL_WIDTH:] = yg_ref[...]
    m = m_ref[...]
    for c0, c1 in _col_chunks(D_MODEL):
        o_ref[:, c0:c1] = h_ref[:, c0:c1] + _dot(m, w_ref[:, c0:c1])
    a_ref[...] = _rms(o_ref[...], nw_ref[...]).astype(BF16)


def _outproj(h, ys, yp, yg, w, nw, layer, tm):
    L = h.shape[0]
    row = lambda wd: pl.BlockSpec((tm, wd), lambda i: (i, 0))
    return pl.pallas_call(
        _outproj_kernel,
        grid=(L // tm,),
        in_specs=[row(D_MODEL), row(S5_WIDTH), row(POOL_WIDTH), row(GLA_V_WIDTH),
                  pl.BlockSpec((None, D_MODEL, D_MODEL), lambda i: (layer, 0, 0),
                               pipeline_mode=pl.Buffered(1)),
                  pl.BlockSpec((None, 1, D_MODEL), lambda i: (layer, 0, 0))],
        out_specs=[row(D_MODEL), row(D_MODEL)],
        out_shape=[jax.ShapeDtypeStruct((L, D_MODEL), F32), jax.ShapeDtypeStruct((L, D_MODEL), BF16)],
        scratch_shapes=[pltpu.VMEM((tm, D_MODEL), BF16)],
        compiler_params=_cparams(1),
        name="outproj",
    )(h, ys, yp, yg, w, nw)


FFN_FC = 512
FFN_NC = D_FF // FFN_FC
FFN_FN = 512
FFN_NN = D_MODEL // FFN_FN
CONV_HALO = SUBLANES


def _serpentine(i, k, n):
    k = jnp.clip(k, 0, n - 1)
    return jnp.where(i % 2 == 0, k, n - 1 - k)


def _ffn_kernel(a_ref, h_ref, wg_ref, wv_ref, cw_ref, cb_ref, wd_ref, o_ref, act_ref, graw, carry, *, tm):
    i, s = pl.program_id(0), pl.program_id(1)

    @pl.when(s < FFN_NC)
    def _up():
        c = _serpentine(i, s, FFN_NC)
        cols = pl.ds(pl.multiple_of(c * FFN_FC, FFN_FC), FFN_FC)
        a = a_ref[...]
        graw[0:CONV_HALO, :] = jnp.where(i > 0, carry[c], 0.0)
        graw[CONV_HALO:CONV_HALO + tm, :] = _dot(a, wg_ref[...])
        carry[c] = graw[tm:tm + CONV_HALO, :]
        cw = cw_ref[:, cols]
        g = graw[...]
        back = lambda k: pltpu.roll(g, k, axis=0)[CONV_HALO:, :]
        gc = cb_ref[:, cols] + back(2) * cw[0:1, :]
        gc = gc + back(1) * cw[1:2, :]
        gc = gc + g[CONV_HALO:, :] * cw[2:3, :]
        act_ref[:, cols] = (gc * _sigmoid(gc) * _dot(a, wv_ref[...])).astype(BF16)

    @pl.when(s >= FFN_NC)
    def _down():
        o_ref[...] = h_ref[...] + _dot(act_ref[...], wd_ref[...])


def _ffn(a, h, w_up, conv_w, conv_b, w_down, layer, tm):
    L = h.shape[0]
    up_c = lambda i, s: _serpentine(i, s, FFN_NC)
    down_n = lambda i, s: _serpentine(i, s - FFN_NC, FFN_NN)
    return pl.pallas_call(
        functools.partial(_ffn_kernel, tm=tm),
        grid=(L // tm, FFN_NC + FFN_NN),
        in_specs=[pl.BlockSpec((tm, D_MODEL), lambda i, s: (i, 0)),
                  pl.BlockSpec((tm, FFN_FN), lambda i, s: (i, down_n(i, s))),
                  pl.BlockSpec((None, D_MODEL, FFN_FC), lambda i, s: (layer, 0, up_c(i, s))),
                  pl.BlockSpec((None, D_MODEL, FFN_FC), lambda i, s: (layer, 0, FFN_NC + up_c(i, s))),
                  pl.BlockSpec((None, 3, D_FF), lambda i, s: (layer, 0, 0)),
                  pl.BlockSpec((None, 1, D_FF), lambda i, s: (layer, 0, 0)),
                  pl.BlockSpec((None, D_FF, FFN_FN), lambda i, s: (layer, 0, down_n(i, s)))],
        out_specs=pl.BlockSpec((tm, FFN_FN), lambda i, s: (i, down_n(i, s))),
        out_shape=jax.ShapeDtypeStruct((L, D_MODEL), F32),
        scratch_shapes=[pltpu.VMEM((tm, D_FF), BF16),
                        pltpu.VMEM((tm + CONV_HALO, FFN_FC), F32),
                        pltpu.VMEM((FFN_NC, CONV_HALO, FFN_FC), F32)],
        compiler_params=_cparams(2),
        name="ffn",
    )(a, h, w_up, w_up, conv_w, conv_b, w_down)


def _ple_kernel(*refs, final):
    if final:
        h_ref, p_ref, nw_ref, wpg_ref, wple_ref, fw_ref, o_ref, a_ref = refs
    else:
        h_ref, p_ref, nw_ref, wpg_ref, wple_ref, o_ref, a_ref = refs
    a_ref[...], rinv = _rms_split(h_ref[...], nw_ref[...], MM_TN)
    a = a_ref[...]
    pe = p_ref[...].astype(BF16)
    for c0, c1 in _col_chunks(D_MODEL):
        gate = _sigmoid(rinv * _dot(a, wpg_ref[:, c0:c1]))
        o_ref[:, c0:c1] = h_ref[:, c0:c1] + _dot(pe, wple_ref[:, c0:c1]) * gate
    if final:
        o_ref[...] = _rms(o_ref[...], fw_ref[...])


def _ple(h, p, nw, w_pg, w_ple, final_w, layer, tm):
    L = h.shape[0]
    final = final_w is not None
    in_specs = [pl.BlockSpec((tm, D_MODEL), lambda i: (i, 0)),
                pl.BlockSpec((None, tm, PLE_DIM), lambda i: (layer, i, 0)),
                pl.BlockSpec((None, 1, D_MODEL), lambda i: (layer, 0, 0)),
                pl.BlockSpec((None, D_MODEL, D_MODEL), lambda i: (layer, 0, 0),
                             pipeline_mode=pl.Buffered(1)),
                pl.BlockSpec((None, PLE_DIM, D_MODEL), lambda i: (layer, 0, 0),
                             pipeline_mode=pl.Buffered(1))]
    args = [h, p, nw, w_pg, w_ple]
    if final:
        in_specs.append(_resident((1, D_MODEL)))
        args.append(final_w)
    return pl.pallas_call(
        functools.partial(_ple_kernel, final=final),
        grid=(L // tm,),
        in_specs=in_specs,
        out_specs=pl.BlockSpec((tm, D_MODEL), lambda i: (i, 0)),
        out_shape=jax.ShapeDtypeStruct((L, D_MODEL), F32),
        scratch_shapes=[pltpu.VMEM((tm, D_MODEL), BF16)],
        compiler_params=_cparams(1),
        name="ple_final" if final else "ple",
    )(*args)


def _pack_w_in(w_in):
    u, zp, q, k, v, g, r = jnp.split(w_in.astype(BF16), [512, 1024, 1536, 2048, 3072, 3088], axis=-1)
    g = jnp.pad(g, ((0, 0), (0, 0), (0, LANES - GLA_GATE_RANK)))
    return jnp.concatenate([u, zp, q, k, v, r, g], axis=-1)


def _chunk_tril(tm):
    r = jnp.arange(tm)
    same = (r[:, None] // GLA_CHUNK) == (r[None, :] // GLA_CHUNK)
    return (same & (r[:, None] >= r[None, :])).astype(BF16)


def kernel(x, p, norm_mix_w, w_in, s5_a_re, s5_a_im, s5_log_dt, s5_b_re, s5_b_im, s5_c_re, s5_c_im, s5_d, s5_w_glu, s5_b_glu, pool_w, pool_scale, gla_w_a2, gla_b_a, gla_norm_w, w_out, norm_ffn_w, w_up, conv_w, conv_b, w_down, norm_ple_w, w_ple, w_pg, final_norm_w):
    bsz, L, _ = x.shape
    assert bsz == 1 and L % (S5_SEGMENTS * S5_TB) == 0
    depth = w_in.shape[0]
    tm = min(512, L)
    tm_ffn = min(1024, L)
    tm_mix = min(512, L)
    tri = _chunk_tril(tm_mix)
    w_up_c, w_down_c = w_up.astype(BF16), w_down.astype(BF16)
    nw_ffn = norm_ffn_w.reshape(depth, 1, D_MODEL)
    conv_b3 = conv_b.reshape(depth, 1, D_FF)
    w_in_p = _pack_w_in(w_in)
    w_out_b, w_pg_b, w_ple_b = w_out.astype(BF16), w_pg.astype(BF16), w_ple.astype(BF16)
    nw_mix = norm_mix_w.reshape(depth, 1, D_MODEL)
    nw_ple = norm_ple_w.reshape(depth, 1, D_MODEL)
    p3 = p.reshape(depth, L, PLE_DIM)
    h = x.reshape(L, D_MODEL)
    for i in range(depth):
        z = _inproj(h, nw_mix, w_in_p, i, tm)
        s5_prm = _s5_prepare(s5_a_re[i], s5_a_im[i], s5_log_dt[i], s5_b_re[i], s5_b_im[i], s5_c_re[i],
                             s5_c_im[i], s5_d[i], s5_w_glu[i], s5_b_glu[i], L // S5_SEGMENTS)
        y_s5 = _s5_mixer(z, s5_prm, S5_TB)
        mix_prm = {
            "pool_w": pool_w[i].astype(BF16), "pool_scale": pool_scale[i].reshape(1, POOL_WIDTH),
            "w_a2": jnp.pad(gla_w_a2[i], ((0, LANES - GLA_GATE_RANK), (0, 0))).astype(BF16),
            "b_a": gla_b_a[i].reshape(1, GLA_K_WIDTH), "gla_norm_w": gla_norm_w[i].reshape(1, GLA_DV),
            "tri": tri,
        }
        y_pool, y_gla = _mixers(z, mix_prm, tm_mix)
        h, a = _outproj(h, y_s5, y_pool, y_gla, w_out_b, nw_ffn, i, tm)
        h = _ffn(a, h, w_up_c, conv_w, conv_b3, w_down_c, i, tm_ffn)
        final_w = final_norm_w.reshape(1, D_MODEL) if i == depth - 1 else None
        h = _ple(h, p3, nw_ple, w_pg_b, w_ple_b, final_w, i, tm)
    return h.reshape(bsz, L, D_MODEL)
```

```python
import functools
import math

import jax
import jax.numpy as jnp
from jax import lax
from jax.experimental import pallas as pl
from jax.experimental.pallas import tpu as pltpu

F32 = jnp.float32
BF16 = jnp.bfloat16

D_MODEL = 2048
S5_WIDTH = 512
S5_GROUP = 16
S5_GROUPS = 32
S5_STATE = 64
S5_COLS = S5_GROUPS * S5_STATE
POOL_WIDTH = 512
POOL_WINDOWS = (2, 4, 8, 16)
POOL_GROUP = 128
GLA_HEADS = 4
GLA_DK = 128
GLA_DV = 256
GLA_K_WIDTH = 512
GLA_V_WIDTH = 1024
GLA_GATE_RANK = 16
GLA_TAU = 16.0
GLA_CHUNK = 64
D_FF = 5632
PLE_DIM = 256
EPS = 1e-6

LANES = 128
SUBLANES = 8
S5_SEGMENTS = SUBLANES
S5_BUNDLE = LANES // S5_GROUP
S5_NBUNDLES = S5_GROUPS // S5_BUNDLE
S5_BCOLS = S5_BUNDLE * S5_STATE
S5_TB = 64
POOL_HALO = 16

Z_S5, Z_POOL, Z_Q, Z_K, Z_V, Z_R, Z_G = 0, 512, 1024, 1536, 2048, 3072, 4096
Z_WIDTH = 4224
MM_TN = 512

VMEM_LIMIT = 56 * 1024 * 1024


def _cparams(n_axes):
    return pltpu.CompilerParams(dimension_semantics=("arbitrary",) * n_axes,
                                vmem_limit_bytes=VMEM_LIMIT)


def _resident(shape):
    nd = len(shape)
    return pl.BlockSpec(shape, lambda *_: (0,) * nd, pipeline_mode=pl.Buffered(1))


def _layer_block(arr, layer):
    nd = arr.ndim
    return pl.BlockSpec((None,) + arr.shape[1:], lambda *_: (layer,) + (0,) * (nd - 1),
                        pipeline_mode=pl.Buffered(1))


def _rms(x, w):
    ms = jnp.mean(x * x, axis=-1, keepdims=True)
    return x * lax.rsqrt(ms + EPS) * w


def _sigmoid(x):
    return 1.0 / (1.0 + jnp.exp(-x))


def _dot(a, b):
    return jnp.dot(a, b, preferred_element_type=F32)


def _dot_nt(a, b):
    return lax.dot_general(a, b, (((1,), (1,)), ((), ())), preferred_element_type=F32)


def _dot_tn(a, b):
    return lax.dot_general(a, b, (((0,), (0,)), ((), ())), preferred_element_type=F32)


def _col_chunks(n):
    return [(c0, min(c0 + MM_TN, n)) for c0 in range(0, n, MM_TN)]


MXU_K = 256


def _rms_split(h_ref, w_ref, a_ref, width):
    sq = None
    for k0 in range(0, h_ref.shape[1], MXU_K):
        hk = h_ref[:, k0:k0 + MXU_K]
        a_ref[:, k0:k0 + MXU_K] = (hk * w_ref[:, k0:k0 + MXU_K]).astype(BF16)
        sq = hk * hk if sq is None else sq + hk * hk
    rinv = lax.rsqrt(jnp.sum(sq, axis=-1, keepdims=True) * (1.0 / h_ref.shape[1]) + EPS)
    return jnp.broadcast_to(rinv, (h_ref.shape[0], width))


def _inproj_kernel(h_ref, nw_ref, wm_ref, wr_ref, wg_ref, z_ref, a_ref):
    rinv = _rms_split(h_ref, nw_ref, a_ref, MM_TN)
    a = a_ref[...]
    for c0, c1 in _col_chunks(Z_R):
        z_ref[:, c0:c1] = rinv * _dot(a, wm_ref[:, c0:c1])
    for c0, c1 in _col_chunks(GLA_V_WIDTH):
        z_ref[:, Z_R + c0:Z_R + c1] = rinv * _dot(a, wr_ref[:, c0:c1])
    z_ref[:, Z_G:] = rinv[:, :LANES] * _dot(a, wg_ref[...])


def _inproj(h, nw, w_main, w_r, w_g, layer, tm):
    L = h.shape[0]
    whole = lambda w: pl.BlockSpec((None,) + w.shape[1:], lambda i: (layer, 0, 0),
                                   pipeline_mode=pl.Buffered(1))
    return pl.pallas_call(
        _inproj_kernel,
        grid=(L // tm,),
        in_specs=[pl.BlockSpec((tm, D_MODEL), lambda i: (i, 0)),
                  pl.BlockSpec((None, 1, D_MODEL), lambda i: (layer, 0, 0)),
                  whole(w_main), whole(w_r), whole(w_g)],
        out_specs=pl.BlockSpec((tm, Z_WIDTH), lambda i: (i, 0)),
        out_shape=jax.ShapeDtypeStruct((L, Z_WIDTH), F32),
        scratch_shapes=[pltpu.VMEM((tm, D_MODEL), BF16)],
        compiler_params=_cparams(1),
        name="inproj",
    )(h, nw, w_main, w_r, w_g)


def _gelu_tanh(x):
    return 0.5 * x * (1.0 + jnp.tanh(math.sqrt(2.0 / math.pi) * (x + 0.044715 * (x * x * x))))


def _s5_kernel(*refs, tb, pass2):
    if pass2:
        (u_ref, wb_ref, are_ref, aim_ref, xe_re_ref, xe_im_ref, ap_re_ref, ap_im_ref,
         wc_ref, d_ref, wglu_ref, bglu_ref, y_ref, ubuf, xs, st_re, st_im) = refs
    else:
        (u_ref, wb_ref, are_ref, aim_ref, xe_re_ref, xe_im_ref,
         ubuf, st_re, st_im) = refs
    nseg = S5_SEGMENTS

    @pl.when(pl.program_id(0) == 0)
    def _init():
        if pass2:
            apr, api = ap_re_ref[...], ap_im_ref[...]
            st_re[0:1, :] = jnp.zeros((1, S5_COLS), F32)
            st_im[0:1, :] = jnp.zeros((1, S5_COLS), F32)
            for j in range(nseg - 1):
                cr, ci = st_re[j:j + 1, :], st_im[j:j + 1, :]
                st_re[j + 1:j + 2, :] = apr * cr - api * ci + xe_re_ref[j:j + 1, :]
                st_im[j + 1:j + 2, :] = apr * ci + api * cr + xe_im_ref[j:j + 1, :]
        else:
            st_re[...] = jnp.zeros((nseg, S5_COLS), F32)
            st_im[...] = jnp.zeros((nseg, S5_COLS), F32)

    for j in range(nseg):
        uj = u_ref[j]
        for b in range(S5_NBUNDLES):
            ubuf[b, pl.ds(j, tb, stride=nseg), :] = uj[:, b * LANES:(b + 1) * LANES]
    for b in range(S5_NBUNDLES):
        bu = _dot(ubuf[b].astype(BF16), wb_ref[b])
        cols = slice(b * S5_BCOLS, (b + 1) * S5_BCOLS)
        cre = slice(2 * b * S5_BCOLS, (2 * b + 1) * S5_BCOLS)
        cim = slice((2 * b + 1) * S5_BCOLS, (2 * b + 2) * S5_BCOLS)
        ar = jnp.broadcast_to(are_ref[:, cols], (nseg, S5_BCOLS))
        ai = jnp.broadcast_to(aim_ref[:, cols], (nseg, S5_BCOLS))
        xr, xi = st_re[:, cols], st_im[:, cols]
        for t in range(tb):
            rows = slice(t * nseg, (t + 1) * nseg)
            xr, xi = (ar * xr - ai * xi + bu[rows, :S5_BCOLS], ar * xi + ai * xr + bu[rows, S5_BCOLS:])
            if pass2:
                xs[rows, cre] = xr
                xs[rows, cim] = xi
        st_re[:, cols] = xr
        st_im[:, cols] = xi

    if not pass2:
        xe_re_ref[...] = st_re[...]
        xe_im_ref[...] = st_im[...]
        return

    ys = [_dot(xs[:, 2 * b * S5_BCOLS:2 * (b + 1) * S5_BCOLS].astype(BF16), wc_ref[b])
          for b in range(S5_NBUNDLES)]
    u = jnp.concatenate([ubuf[b] for b in range(S5_NBUNDLES)], axis=1)
    y = jnp.concatenate(ys, axis=1) + d_ref[...] * u
    y = _gelu_tanh(y)
    glu = _sigmoid(_dot(y.astype(BF16), wglu_ref[...]) + bglu_ref[...])
    out = y * glu
    for b in range(S5_NBUNDLES):
        ubuf[b] = out[:, b * LANES:(b + 1) * LANES]
    for j in range(nseg):
        y_ref[j] = jnp.concatenate(
            [ubuf[b, pl.ds(j, tb, stride=nseg), :] for b in range(S5_NBUNDLES)], axis=1).astype(BF16)


def _s5_mixer(z, prm, layer, tb):
    L = z.shape[0]
    lb = lambda name: _layer_block(prm[name], layer)
    nseg = S5_SEGMENTS
    seg_len = L // nseg
    nblk = seg_len // tb
    rows = tb * nseg
    z3 = z.reshape(nseg, seg_len, Z_WIDTH)
    u_spec = pl.BlockSpec((nseg, tb, S5_WIDTH), lambda i: (0, i, Z_S5 // S5_WIDTH))
    state_shape = jax.ShapeDtypeStruct((nseg, S5_COLS), F32)
    ubuf = pltpu.VMEM((S5_NBUNDLES, rows, LANES), F32)
    state = [pltpu.VMEM((nseg, S5_COLS), F32), pltpu.VMEM((nseg, S5_COLS), F32)]
    common = [u_spec, lb("wb"), lb("a_re"), lb("a_im")]

    xe_re, xe_im = pl.pallas_call(
        functools.partial(_s5_kernel, tb=tb, pass2=False),
        grid=(nblk,),
        in_specs=common,
        out_specs=[_resident_out((nseg, S5_COLS)), _resident_out((nseg, S5_COLS))],
        out_shape=[state_shape, state_shape],
        scratch_shapes=[ubuf] + state,
        compiler_params=_cparams(1),
        name="s5_states",
    )(z3, prm["wb"], prm["a_re"], prm["a_im"])

    y = pl.pallas_call(
        functools.partial(_s5_kernel, tb=tb, pass2=True),
        grid=(nblk,),
        in_specs=common + [_resident((nseg, S5_COLS)), _resident((nseg, S5_COLS)),
                           lb("ap_re"), lb("ap_im"), lb("wc"), lb("d"), lb("w_glu"), lb("b_glu")],
        out_specs=pl.BlockSpec((nseg, tb, S5_WIDTH), lambda i: (0, i, 0)),
        out_shape=jax.ShapeDtypeStruct((nseg, seg_len, S5_WIDTH), BF16),
        scratch_shapes=[ubuf, pltpu.VMEM((rows, 2 * S5_COLS), F32)] + state,
        compiler_params=_cparams(1),
        name="s5_outputs",
    )(z3, prm["wb"], prm["a_re"], prm["a_im"], xe_re, xe_im, prm["ap_re"], prm["ap_im"],
      prm["wc"], prm["d"], prm["w_glu"], prm["b_glu"])
    return y.reshape(L, S5_WIDTH)


def _resident_out(shape):
    nd = len(shape)
    return pl.BlockSpec(shape, lambda *_: (0,) * nd)


def _s5_prepare(a_re, a_im, log_dt, b_re, b_im, c_re, c_im, d_skip, w_glu, b_glu, seg_len):
    dt = jnp.exp(log_dt)[:, None]
    mag = jnp.exp(a_re * dt)
    ab_re, ab_im = mag * jnp.cos(a_im * dt), mag * jnp.sin(a_im * dt)
    nr, ni = ab_re - 1.0, ab_im
    den = a_re * a_re + a_im * a_im
    f_re, f_im = (nr * a_re + ni * a_im) / den, (ni * a_re - nr * a_im) / den
    bb_re = f_re[..., None] * b_re - f_im[..., None] * b_im
    bb_im = f_re[..., None] * b_im + f_im[..., None] * b_re
    pr, pi = jnp.ones_like(ab_re), jnp.zeros_like(ab_re)
    sr, si, e = ab_re, ab_im, seg_len
    while e:
        if e & 1:
            pr, pi = pr * sr - pi * si, pr * si + pi * sr
        sr, si = sr * sr - si * si, 2.0 * sr * si
        e >>= 1
    eye = jnp.eye(S5_BUNDLE, dtype=F32)
    nb, gb = S5_NBUNDLES, S5_BUNDLE

    def b_slab(bb):
        return jnp.einsum("bgnc,gh->bgchn", bb.reshape(nb, gb, S5_STATE, S5_GROUP), eye).reshape(
            nb, LANES, S5_BCOLS)

    def c_slab(cm):
        return jnp.einsum("bgcn,gh->bgnhc", cm.reshape(nb, gb, S5_GROUP, S5_STATE), eye).reshape(
            nb, S5_BCOLS, LANES)

    return {
        "wb": jnp.concatenate([b_slab(bb_re), b_slab(bb_im)], axis=-1).astype(BF16),
        "wc": jnp.concatenate([c_slab(c_re), -c_slab(c_im)], axis=1).astype(BF16),
        "a_re": ab_re.reshape(1, S5_COLS), "a_im": ab_im.reshape(1, S5_COLS),
        "ap_re": pr.reshape(1, S5_COLS), "ap_im": pi.reshape(1, S5_COLS),
        "d": d_skip.reshape(1, S5_WIDTH), "w_glu": w_glu.astype(BF16),
        "b_glu": b_glu.reshape(1, S5_WIDTH),
    }


def _mix_kernel(zp_ref, q_ref, k_ref, v_ref, r_ref, g_ref, pw_ref, ps_ref, wa_ref, ba_ref, gnw_ref,
                tri_ref, ypool_ref, ygla_ref, zext, s_ref, *, tm):
    i = pl.program_id(0)

    @pl.when(i == 0)
    def _init():
        zext[0:POOL_HALO, :] = jnp.zeros((POOL_HALO, POOL_WIDTH), F32)
        s_ref[...] = jnp.zeros(s_ref.shape, F32)

    zext[POOL_HALO:POOL_HALO + tm, :] = zp_ref[...]
    pos = (i * tm + 1 + lax.broadcasted_iota(jnp.int32, (tm, 1), 0)).astype(F32)
    for gi, w in enumerate(POOL_WINDOWS):
        cols = slice(gi * POOL_GROUP, (gi + 1) * POOL_GROUP)
        ze = zext[:, cols]
        s, span = ze, 1
        while span < w:
            s = s + pltpu.roll(s, span, axis=0)
            span *= 2
        zc = ze[POOL_HALO:, :]
        pooled = s[POOL_HALO:, :] / jnp.minimum(pos, float(w)) - zc
        mixed = _dot(pooled.astype(BF16), pw_ref[gi]) * ps_ref[:, cols]
        ypool_ref[:, cols] = mixed.astype(BF16)
    zext[0:POOL_HALO, :] = zext[tm:tm + POOL_HALO, :]

    logit = _dot(g_ref[...].astype(BF16), wa_ref[...]) + ba_ref[...]
    la = (jnp.minimum(logit, 0.0) - jnp.log(1.0 + jnp.exp(-jnp.abs(logit)))) / GLA_TAU
    la_hi = la.astype(BF16)
    la_lo = (la - la_hi.astype(F32)).astype(BF16)
    tri = tri_ref[...]
    b = _dot(tri, la_hi) + _dot(tri, la_lo)
    nch = tm // GLA_CHUNK
    b_last = jnp.concatenate(
        [jnp.broadcast_to(b[(c + 1) * GLA_CHUNK - 1:(c + 1) * GLA_CHUNK, :], (GLA_CHUNK, GLA_K_WIDTH))
         for c in range(nch)], axis=0)
    q_dec = (q_ref[...] * (GLA_DK ** -0.5) * jnp.exp(b)).astype(BF16)
    k = k_ref[...]
    k_dec = (k * jnp.exp(-b)).astype(BF16)
    k_end = (k * jnp.exp(b_last - b)).astype(BF16)
    decay = jnp.exp(b_last)
    causal = (lax.broadcasted_iota(jnp.int32, (GLA_CHUNK, GLA_CHUNK), 0)
              >= lax.broadcasted_iota(jnp.int32, (GLA_CHUNK, GLA_CHUNK), 1))
    gnw = gnw_ref[...]
    units = [(hd, c) for hd in range(GLA_HEADS) for c in range(nch)]
    rows = lambda c: slice(c * GLA_CHUNK, (c + 1) * GLA_CHUNK)
    kcol = lambda hd: slice(hd * GLA_DK, (hd + 1) * GLA_DK)
    vcol = lambda hd: slice(hd * GLA_DV, (hd + 1) * GLA_DV)
    vv = {(hd, c): v_ref[rows(c), vcol(hd)].astype(BF16) for hd, c in units}
    scores = {(hd, c): _dot_nt(q_dec[rows(c), kcol(hd)], k_dec[rows(c), kcol(hd)]) for hd, c in units}
    kv = {(hd, c): _dot_tn(vv[hd, c], k_end[rows(c), kcol(hd)]) for hd, c in units}
    st_in = {}
    for hd in range(GLA_HEADS):
        st = s_ref[hd]
        for c in range(nch):
            st_in[hd, c] = st.astype(BF16)
            st = decay[c * GLA_CHUNK:c * GLA_CHUNK + 1, kcol(hd)] * st + kv[hd, c]
        s_ref[hd] = st
    for hd, c in units:
        sc = jnp.where(causal, scores[hd, c], 0.0).astype(BF16)
        o = _dot(sc, vv[hd, c]) + _dot_nt(q_dec[rows(c), kcol(hd)], st_in[hd, c])
        o = _rms(o, gnw)
        rr = r_ref[rows(c), vcol(hd)]
        ygla_ref[rows(c), vcol(hd)] = (o * (rr * _sigmoid(rr))).astype(BF16)


def _mixers(z, prm, layer, tm):
    L = z.shape[0]
    row = lambda w, col: pl.BlockSpec((tm, w), lambda i: (i, col // w))
    lb = lambda name: _layer_block(prm[name], layer)
    return pl.pallas_call(
        functools.partial(_mix_kernel, tm=tm),
        grid=(L // tm,),
        in_specs=[row(POOL_WIDTH, Z_POOL), row(GLA_K_WIDTH, Z_Q), row(GLA_K_WIDTH, Z_K),
                  row(GLA_V_WIDTH, Z_V), row(GLA_V_WIDTH, Z_R), row(LANES, Z_G),
                  lb("pool_w"), lb("pool_scale"), lb("w_a2"), lb("b_a"), lb("gla_norm_w"),
                  _resident((tm, tm))],
        out_specs=[pl.BlockSpec((tm, POOL_WIDTH), lambda i: (i, 0)),
                   pl.BlockSpec((tm, GLA_V_WIDTH), lambda i: (i, 0))],
        out_shape=[jax.ShapeDtypeStruct((L, POOL_WIDTH), BF16),
                   jax.ShapeDtypeStruct((L, GLA_V_WIDTH), BF16)],
        scratch_shapes=[pltpu.VMEM((tm + POOL_HALO, POOL_WIDTH), F32),
                        pltpu.VMEM((GLA_HEADS, GLA_DV, GLA_DK), F32)],
        compiler_params=_cparams(1),
        name="mixers",
    )(z, z, z, z, z, z, prm["pool_w"], prm["pool_scale"], prm["w_a2"], prm["b_a"], prm["gla_norm_w"],
      prm["tri"])


def _outproj_kernel(h_ref, ys_ref, yp_ref, yg_ref, w_ref, nw_ref, wup_ref, wdn_ref,
                    o_ref, a_ref, wup_b_ref, wdn_b_ref, m_ref):
    wup_b_ref[...] = wup_ref[...].astype(BF16)
    wdn_b_ref[...] = wdn_ref[...].astype(BF16)
    m_ref[:, 0:S5_WIDTH] = ys_ref[...]
    m_ref[:, S5_WIDTH:S5_WIDTH + POOL_WIDTH] = yp_ref[...]
    m_ref[:, S5_WIDTH + POOL_WIDTH:] = yg_ref[...]
    m = m_ref[...]
    for c0, c1 in _col_chunks(D_MODEL):
        o_ref[:, c0:c1] = h_ref[:, c0:c1] + _dot(m, w_ref[:, c0:c1])
    a_ref[...] = _rms(o_ref[...], nw_ref[...]).astype(BF16)


def _outproj(h, ys, yp, yg, w, nw, w_up, w_down, layer, tm):
    L = h.shape[0]
    nt = L // tm
    up_rows, dn_rows = D_MODEL // nt, D_FF // nt
    assert up_rows % (2 * SUBLANES) == 0 and dn_rows % (2 * SUBLANES) == 0
    row = lambda wd: pl.BlockSpec((tm, wd), lambda i: (i, 0))
    return pl.pallas_call(
        _outproj_kernel,
        grid=(nt,),
        in_specs=[row(D_MODEL), row(S5_WIDTH), row(POOL_WIDTH), row(GLA_V_WIDTH),
                  pl.BlockSpec((None, D_MODEL, D_MODEL), lambda i: (layer, 0, 0),
                               pipeline_mode=pl.Buffered(1)),
                  pl.BlockSpec((None, 1, D_MODEL), lambda i: (layer, 0, 0)),
                  pl.BlockSpec((None, up_rows, 2 * D_FF), lambda i: (layer, i, 0)),
                  pl.BlockSpec((None, dn_rows, D_MODEL), lambda i: (layer, i, 0))],
        out_specs=[row(D_MODEL), row(D_MODEL),
                   pl.BlockSpec((up_rows, 2 * D_FF), lambda i: (i, 0)),
                   pl.BlockSpec((dn_rows, D_MODEL), lambda i: (i, 0))],
        out_shape=[jax.ShapeDtypeStruct((L, D_MODEL), F32), jax.ShapeDtypeStruct((L, D_MODEL), BF16),
                   jax.ShapeDtypeStruct((D_MODEL, 2 * D_FF), BF16),
                   jax.ShapeDtypeStruct((D_FF, D_MODEL), BF16)],
        scratch_shapes=[pltpu.VMEM((tm, D_MODEL), BF16)],
        compiler_params=_cparams(1),
        name="outproj",
    )(h, ys, yp, yg, w, nw, w_up, w_down)


FFN_FC = 512
FFN_NC = D_FF // FFN_FC
FFN_FN = 512
FFN_NN = D_MODEL // FFN_FN
CONV_HALO = SUBLANES


def _serpentine(i, k, n):
    k = jnp.clip(k, 0, n - 1)
    return jnp.where(i % 2 == 0, k, n - 1 - k)


def _ffn_kernel(a_ref, h_ref, wg_ref, wv_ref, cw_ref, cb_ref, wd_ref, o_ref, act_ref, graw, carry, *, tm):
    i, s = pl.program_id(0), pl.program_id(1)

    @pl.when(s < FFN_NC)
    def _up():
        c = _serpentine(i, s, FFN_NC)
        cols = pl.ds(pl.multiple_of(c * FFN_FC, FFN_FC), FFN_FC)
        a = a_ref[...]
        graw[0:CONV_HALO, :] = jnp.where(i > 0, carry[c], 0.0)
        graw[CONV_HALO:CONV_HALO + tm, :] = _dot(a, wg_ref[...])
        carry[c] = graw[tm:tm + CONV_HALO, :]
        cw = cw_ref[:, cols]
        g = graw[...]
        back = lambda k: pltpu.roll(g, k, axis=0)[CONV_HALO:, :]
        gc = cb_ref[:, cols] + back(2) * cw[0:1, :]
        gc = gc + back(1) * cw[1:2, :]
        gc = gc + g[CONV_HALO:, :] * cw[2:3, :]
        act_ref[:, cols] = (gc * _sigmoid(gc) * _dot(a, wv_ref[...])).astype(BF16)

    @pl.when(s >= FFN_NC)
    def _down():
        o_ref[...] = h_ref[...] + _dot(act_ref[...], wd_ref[...])


def _ffn(a, h, w_up, conv_w, conv_b, w_down, layer, tm):
    L = h.shape[0]
    up_c = lambda i, s: _serpentine(i, s, FFN_NC)
    down_n = lambda i, s: _serpentine(i, s - FFN_NC, FFN_NN)
    return pl.pallas_call(
        functools.partial(_ffn_kernel, tm=tm),
        grid=(L // tm, FFN_NC + FFN_NN),
        in_specs=[pl.BlockSpec((tm, D_MODEL), lambda i, s: (i, 0)),
                  pl.BlockSpec((tm, FFN_FN), lambda i, s: (i, down_n(i, s))),
                  pl.BlockSpec((D_MODEL, FFN_FC), lambda i, s: (0, up_c(i, s))),
                  pl.BlockSpec((D_MODEL, FFN_FC), lambda i, s: (0, FFN_NC + up_c(i, s))),
                  pl.BlockSpec((None, 3, D_FF), lambda i, s: (layer, 0, 0)),
                  pl.BlockSpec((None, 1, D_FF), lambda i, s: (layer, 0, 0)),
                  pl.BlockSpec((D_FF, FFN_FN), lambda i, s: (0, down_n(i, s)))],
        out_specs=pl.BlockSpec((tm, FFN_FN), lambda i, s: (i, down_n(i, s))),
        out_shape=jax.ShapeDtypeStruct((L, D_MODEL), F32),
        scratch_shapes=[pltpu.VMEM((tm, D_FF), BF16),
                        pltpu.VMEM((tm + CONV_HALO, FFN_FC), F32),
                        pltpu.VMEM((FFN_NC, CONV_HALO, FFN_FC), F32)],
        compiler_params=_cparams(2),
        name="ffn",
    )(a, h, w_up, w_up, conv_w, conv_b, w_down)


def _ple_kernel(*refs, final):
    if final:
        h_ref, p_ref, nw_ref, wpg_ref, wple_ref, fw_ref, o_ref, a_ref = refs
    else:
        h_ref, p_ref, nw_ref, wpg_ref, wple_ref, o_ref, a_ref = refs
    rinv = _rms_split(h_ref, nw_ref, a_ref, MM_TN)
    a = a_ref[...]
    pe = p_ref[...].astype(BF16)
    for c0, c1 in _col_chunks(D_MODEL):
        gate = _sigmoid(rinv * _dot(a, wpg_ref[:, c0:c1]))
        o_ref[:, c0:c1] = h_ref[:, c0:c1] + _dot(pe, wple_ref[:, c0:c1]) * gate
    if final:
        o_ref[...] = _rms(o_ref[...], fw_ref[...])


def _ple(h, p, nw, w_pg, w_ple, final_w, layer, tm):
    L = h.shape[0]
    final = final_w is not None
    in_specs = [pl.BlockSpec((tm, D_MODEL), lambda i: (i, 0)),
                pl.BlockSpec((None, tm, PLE_DIM), lambda i: (layer, i, 0)),
                pl.BlockSpec((None, 1, D_MODEL), lambda i: (layer, 0, 0)),
                pl.BlockSpec((None, D_MODEL, D_MODEL), lambda i: (layer, 0, 0),
                             pipeline_mode=pl.Buffered(1)),
                pl.BlockSpec((None, PLE_DIM, D_MODEL), lambda i: (layer, 0, 0),
                             pipeline_mode=pl.Buffered(1))]
    args = [h, p, nw, w_pg, w_ple]
    if final:
        in_specs.append(_resident((1, D_MODEL)))
        args.append(final_w)
    return pl.pallas_call(
        functools.partial(_ple_kernel, final=final),
        grid=(L // tm,),
        in_specs=in_specs,
        out_specs=pl.BlockSpec((tm, D_MODEL), lambda i: (i, 0)),
        out_shape=jax.ShapeDtypeStruct((L, D_MODEL), F32),
        scratch_shapes=[pltpu.VMEM((tm, D_MODEL), BF16)],
        compiler_params=_cparams(1),
        name="ple_final" if final else "ple",
    )(*args)


def _split_w_in(w_in):
    g0, r0 = Z_R, Z_R + GLA_GATE_RANK
    w_g = jnp.pad(w_in[..., g0:r0], ((0, 0), (0, 0), (0, LANES - GLA_GATE_RANK)))
    return (w_in[..., :g0].astype(BF16), w_in[..., r0:r0 + GLA_V_WIDTH].astype(BF16), w_g.astype(BF16))


def _chunk_tril(tm):
    r = jnp.arange(tm)
    same = (r[:, None] // GLA_CHUNK) == (r[None, :] // GLA_CHUNK)
    return (same & (r[:, None] >= r[None, :])).astype(BF16)


def kernel(x, p, norm_mix_w, w_in, s5_a_re, s5_a_im, s5_log_dt, s5_b_re, s5_b_im, s5_c_re, s5_c_im, s5_d, s5_w_glu, s5_b_glu, pool_w, pool_scale, gla_w_a2, gla_b_a, gla_norm_w, w_out, norm_ffn_w, w_up, conv_w, conv_b, w_down, norm_ple_w, w_ple, w_pg, final_norm_w):
    bsz, L, _ = x.shape
    assert bsz == 1 and L % (S5_SEGMENTS * S5_TB) == 0
    depth = w_in.shape[0]
    tm = min(512, L)
    tm_ffn = min(1024, L)
    tm_mix = min(512, L)
    tri = _chunk_tril(tm_mix)
    nw_ffn = norm_ffn_w.reshape(depth, 1, D_MODEL)
    conv_b3 = conv_b.reshape(depth, 1, D_FF)
    w_in_main, w_in_r, w_in_g = _split_w_in(w_in)
    w_out_b, w_pg_b, w_ple_b = w_out.astype(BF16), w_pg.astype(BF16), w_ple.astype(BF16)
    nw_mix = norm_mix_w.reshape(depth, 1, D_MODEL)
    nw_ple = norm_ple_w.reshape(depth, 1, D_MODEL)
    p3 = p.reshape(depth, L, PLE_DIM)
    s5_prm = jax.vmap(functools.partial(_s5_prepare, seg_len=L // S5_SEGMENTS))(
        s5_a_re, s5_a_im, s5_log_dt, s5_b_re, s5_b_im, s5_c_re, s5_c_im, s5_d, s5_w_glu, s5_b_glu)
    mix_prm = {
        "pool_w": pool_w.astype(BF16), "pool_scale": pool_scale.reshape(depth, 1, POOL_WIDTH),
        "w_a2": jnp.pad(gla_w_a2, ((0, 0), (0, LANES - GLA_GATE_RANK), (0, 0))).astype(BF16),
        "b_a": gla_b_a.reshape(depth, 1, GLA_K_WIDTH), "gla_norm_w": gla_norm_w.reshape(depth, 1, GLA_DV),
        "tri": tri,
    }
    h = x.reshape(L, D_MODEL)
    for i in range(depth):
        z = _inproj(h, nw_mix, w_in_main, w_in_r, w_in_g, i, tm)
        y_s5 = _s5_mixer(z, s5_prm, i, S5_TB)
        y_pool, y_gla = _mixers(z, mix_prm, i, tm_mix)
        h, a, w_up_b, w_down_b = _outproj(h, y_s5, y_pool, y_gla, w_out_b, nw_ffn, w_up, w_down, i, tm)
        h = _ffn(a, h, w_up_b, conv_w, conv_b3, w_down_b, i, tm_ffn)
        final_w = final_norm_w.reshape(1, D_MODEL) if i == depth - 1 else None
        h = _ple(h, p3, nw_ple, w_pg_b, w_ple_b, final_w, i, tm)
    return h.reshape(bsz, L, D_MODEL)
```

```python
import functools
import math

import jax
import jax.numpy as jnp
from jax import lax
from jax.experimental import pallas as pl
from jax.experimental.pallas import tpu as pltpu

F32 = jnp.float32
BF16 = jnp.bfloat16

D_MODEL = 2048
S5_WIDTH = 512
S5_GROUP = 16
S5_GROUPS = 32
S5_STATE = 64
S5_COLS = S5_GROUPS * S5_STATE
POOL_WIDTH = 512
POOL_WINDOWS = (2, 4, 8, 16)
POOL_GROUP = 128
GLA_HEADS = 4
GLA_DK = 128
GLA_DV = 256
GLA_K_WIDTH = 512
GLA_V_WIDTH = 1024
GLA_GATE_RANK = 16
GLA_TAU = 16.0
GLA_CHUNK = 64
D_FF = 5632
PLE_DIM = 256
EPS = 1e-6

LANES = 128
SUBLANES = 8
S5_SEGMENTS = SUBLANES
S5_BUNDLE = LANES // S5_GROUP
S5_NBUNDLES = S5_GROUPS // S5_BUNDLE
S5_BCOLS = S5_BUNDLE * S5_STATE
S5_TB = 64
POOL_HALO = 16

Z_S5, Z_POOL, Z_Q, Z_K, Z_V, Z_R, Z_G = 0, 512, 1024, 1536, 2048, 3072, 4096
Z_WIDTH = 4224
MM_TN = 512

VMEM_LIMIT = 56 * 1024 * 1024


def _cparams(n_axes):
    return pltpu.CompilerParams(dimension_semantics=("arbitrary",) * n_axes,
                                vmem_limit_bytes=VMEM_LIMIT)


def _resident(shape):
    nd = len(shape)
    return pl.BlockSpec(shape, lambda *_: (0,) * nd, pipeline_mode=pl.Buffered(1))


def _layer_block(arr, layer):
    nd = arr.ndim
    return pl.BlockSpec((None,) + arr.shape[1:], lambda *_: (layer,) + (0,) * (nd - 1),
                        pipeline_mode=pl.Buffered(1))


def _rms(x, w):
    ms = jnp.mean(x * x, axis=-1, keepdims=True)
    return x * lax.rsqrt(ms + EPS) * w


def _sigmoid(x):
    return 1.0 / (1.0 + jnp.exp(-x))


def _dot(a, b):
    return jnp.dot(a, b, preferred_element_type=F32)


def _dot_nt(a, b):
    return lax.dot_general(a, b, (((1,), (1,)), ((), ())), preferred_element_type=F32)


def _dot_tn(a, b):
    return lax.dot_general(a, b, (((0,), (0,)), ((), ())), preferred_element_type=F32)


def _col_chunks(n):
    return [(c0, min(c0 + MM_TN, n)) for c0 in range(0, n, MM_TN)]


MXU_K = 256


def _rms_split(h_ref, w_ref, a_ref, width):
    sq = None
    for k0 in range(0, h_ref.shape[1], MXU_K):
        hk = h_ref[:, k0:k0 + MXU_K]
        a_ref[:, k0:k0 + MXU_K] = (hk * w_ref[:, k0:k0 + MXU_K]).astype(BF16)
        sq = hk * hk if sq is None else sq + hk * hk
    rinv = lax.rsqrt(jnp.sum(sq, axis=-1, keepdims=True) * (1.0 / h_ref.shape[1]) + EPS)
    return jnp.broadcast_to(rinv, (h_ref.shape[0], width))


def _inproj_kernel(h_ref, nw_ref, w_ref, z_ref, a_ref):
    rinv = _rms_split(h_ref, nw_ref, a_ref, MM_TN)
    a = a_ref[...]
    zt = jnp.concatenate([_dot(a, w_ref[:, Z_R + c0:Z_R + c1]) for c0, c1 in _col_chunks(Z_WIDTH - Z_R)],
                         axis=1)
    z_ref[:, Z_G:] = rinv[:, :LANES] * zt[:, :LANES]
    r = zt[:, GLA_GATE_RANK:GLA_GATE_RANK + GLA_V_WIDTH]
    for c0, c1 in _col_chunks(GLA_V_WIDTH):
        z_ref[:, Z_R + c0:Z_R + c1] = rinv * r[:, c0:c1]
    for c0, c1 in _col_chunks(Z_R):
        z_ref[:, c0:c1] = rinv * _dot(a, w_ref[:, c0:c1])


def _inproj(h, nw, w, layer, tm):
    L = h.shape[0]
    return pl.pallas_call(
        _inproj_kernel,
        grid=(L // tm,),
        in_specs=[pl.BlockSpec((tm, D_MODEL), lambda i: (i, 0)),
                  pl.BlockSpec((None, 1, D_MODEL), lambda i: (layer, 0, 0)),
                  _layer_block(w, layer)],
        out_specs=pl.BlockSpec((tm, Z_WIDTH), lambda i: (i, 0)),
        out_shape=jax.ShapeDtypeStruct((L, Z_WIDTH), F32),
        scratch_shapes=[pltpu.VMEM((tm, D_MODEL), BF16)],
        compiler_params=_cparams(1),
        name="inproj",
    )(h, nw, w)


def _gelu_tanh(x):
    return 0.5 * x * (1.0 + jnp.tanh(math.sqrt(2.0 / math.pi) * (x + 0.044715 * (x * x * x))))


def _s5_kernel(*refs, tb, pass2):
    if pass2:
        (u_ref, wb_ref, are_ref, aim_ref, ride_ref, xe_re_ref, xe_im_ref, ap_re_ref, ap_im_ref,
         wc_ref, d_ref, wglu_ref, bglu_ref, y_ref, cast_ref, ubuf, xs, st_re, st_im) = refs
    else:
        (u_ref, wb_ref, are_ref, aim_ref, ride_ref, xe_re_ref, xe_im_ref, cast_ref,
         ubuf, st_re, st_im) = refs
    nseg = S5_SEGMENTS
    cast_ref[...] = ride_ref[...].astype(BF16)

    @pl.when(pl.program_id(0) == 0)
    def _init():
        if pass2:
            apr, api = ap_re_ref[...], ap_im_ref[...]
            st_re[0:1, :] = jnp.zeros((1, S5_COLS), F32)
            st_im[0:1, :] = jnp.zeros((1, S5_COLS), F32)
            for j in range(nseg - 1):
                cr, ci = st_re[j:j + 1, :], st_im[j:j + 1, :]
                st_re[j + 1:j + 2, :] = apr * cr - api * ci + xe_re_ref[j:j + 1, :]
                st_im[j + 1:j + 2, :] = apr * ci + api * cr + xe_im_ref[j:j + 1, :]
        else:
            st_re[...] = jnp.zeros((nseg, S5_COLS), F32)
            st_im[...] = jnp.zeros((nseg, S5_COLS), F32)

    for j in range(nseg):
        uj = u_ref[j]
        for b in range(S5_NBUNDLES):
            ubuf[b, pl.ds(j, tb, stride=nseg), :] = uj[:, b * LANES:(b + 1) * LANES]
    for b in range(S5_NBUNDLES):
        bu = _dot(ubuf[b].astype(BF16), wb_ref[b])
        cols = slice(b * S5_BCOLS, (b + 1) * S5_BCOLS)
        cre = slice(2 * b * S5_BCOLS, (2 * b + 1) * S5_BCOLS)
        cim = slice((2 * b + 1) * S5_BCOLS, (2 * b + 2) * S5_BCOLS)
        ar = jnp.broadcast_to(are_ref[:, cols], (nseg, S5_BCOLS))
        ai = jnp.broadcast_to(aim_ref[:, cols], (nseg, S5_BCOLS))
        xr, xi = st_re[:, cols], st_im[:, cols]
        for t in range(tb):
            rows = slice(t * nseg, (t + 1) * nseg)
            xr, xi = (ar * xr - ai * xi + bu[rows, :S5_BCOLS], ar * xi + ai * xr + bu[rows, S5_BCOLS:])
            if pass2:
                xs[rows, cre] = xr
                xs[rows, cim] = xi
        st_re[:, cols] = xr
        st_im[:, cols] = xi

    if not pass2:
        xe_re_ref[...] = st_re[...]
        xe_im_ref[...] = st_im[...]
        return

    ys = [_dot(xs[:, 2 * b * S5_BCOLS:2 * (b + 1) * S5_BCOLS].astype(BF16), wc_ref[b])
          for b in range(S5_NBUNDLES)]
    u = jnp.concatenate([ubuf[b] for b in range(S5_NBUNDLES)], axis=1)
    y = jnp.concatenate(ys, axis=1) + d_ref[...] * u
    y = _gelu_tanh(y)
    glu = _sigmoid(_dot(y.astype(BF16), wglu_ref[...]) + bglu_ref[...])
    out = y * glu
    for b in range(S5_NBUNDLES):
        ubuf[b] = out[:, b * LANES:(b + 1) * LANES]
    for j in range(nseg):
        y_ref[j] = jnp.concatenate(
            [ubuf[b, pl.ds(j, tb, stride=nseg), :] for b in range(S5_NBUNDLES)], axis=1).astype(BF16)


def _s5_mixer(z, prm, ride1, ride2, layer, tb):
    L = z.shape[0]
    lb = lambda name: _layer_block(prm[name], layer)
    nseg = S5_SEGMENTS
    seg_len = L // nseg
    nblk = seg_len // tb
    rows = tb * nseg

    def rider(w):
        slab = w.shape[1] // nblk
        assert slab * nblk == w.shape[1] and slab % (2 * SUBLANES) == 0
        return (pl.BlockSpec((None, slab, w.shape[2]), lambda i: (layer, i, 0)),
                pl.BlockSpec((slab, w.shape[2]), lambda i: (i, 0)),
                jax.ShapeDtypeStruct(w.shape[1:], BF16))

    r1_in, r1_out, r1_shape = rider(ride1)
    r2_in, r2_out, r2_shape = rider(ride2)
    z3 = z.reshape(nseg, seg_len, Z_WIDTH)
    u_spec = pl.BlockSpec((nseg, tb, S5_WIDTH), lambda i: (0, i, Z_S5 // S5_WIDTH))
    state_shape = jax.ShapeDtypeStruct((nseg, S5_COLS), F32)
    ubuf = pltpu.VMEM((S5_NBUNDLES, rows, LANES), F32)
    state = [pltpu.VMEM((nseg, S5_COLS), F32), pltpu.VMEM((nseg, S5_COLS), F32)]
    common = [u_spec, lb("wb"), lb("a_re"), lb("a_im")]

    xe_re, xe_im, cast1 = pl.pallas_call(
        functools.partial(_s5_kernel, tb=tb, pass2=False),
        grid=(nblk,),
        in_specs=common + [r1_in],
        out_specs=[_resident_out((nseg, S5_COLS)), _resident_out((nseg, S5_COLS)), r1_out],
        out_shape=[state_shape, state_shape, r1_shape],
        scratch_shapes=[ubuf] + state,
        compiler_params=_cparams(1),
        name="s5_states",
    )(z3, prm["wb"], prm["a_re"], prm["a_im"], ride1)

    y, cast2 = pl.pallas_call(
        functools.partial(_s5_kernel, tb=tb, pass2=True),
        grid=(nblk,),
        in_specs=common + [r2_in, _resident((nseg, S5_COLS)), _resident((nseg, S5_COLS)),
                           lb("ap_re"), lb("ap_im"), lb("wc"), lb("d"), lb("w_glu"), lb("b_glu")],
        out_specs=[pl.BlockSpec((nseg, tb, S5_WIDTH), lambda i: (0, i, 0)), r2_out],
        out_shape=[jax.ShapeDtypeStruct((nseg, seg_len, S5_WIDTH), BF16), r2_shape],
        scratch_shapes=[ubuf, pltpu.VMEM((rows, 2 * S5_COLS), F32)] + state,
        compiler_params=_cparams(1),
        name="s5_outputs",
    )(z3, prm["wb"], prm["a_re"], prm["a_im"], ride2, xe_re, xe_im, prm["ap_re"], prm["ap_im"],
      prm["wc"], prm["d"], prm["w_glu"], prm["b_glu"])
    return y.reshape(L, S5_WIDTH), cast1, cast2


def _resident_out(shape):
    nd = len(shape)
    return pl.BlockSpec(shape, lambda *_: (0,) * nd)


def _s5_prepare(a_re, a_im, log_dt, b_re, b_im, c_re, c_im, d_skip, w_glu, b_glu, seg_len):
    dt = jnp.exp(log_dt)[:, None]
    mag = jnp.exp(a_re * dt)
    ab_re, ab_im = mag * jnp.cos(a_im * dt), mag * jnp.sin(a_im * dt)
    nr, ni = ab_re - 1.0, ab_im
    den = a_re * a_re + a_im * a_im
    f_re, f_im = (nr * a_re + ni * a_im) / den, (ni * a_re - nr * a_im) / den
    bb_re = f_re[..., None] * b_re - f_im[..., None] * b_im
    bb_im = f_re[..., None] * b_im + f_im[..., None] * b_re
    pr, pi = jnp.ones_like(ab_re), jnp.zeros_like(ab_re)
    sr, si, e = ab_re, ab_im, seg_len
    while e:
        if e & 1:
            pr, pi = pr * sr - pi * si, pr * si + pi * sr
        sr, si = sr * sr - si * si, 2.0 * sr * si
        e >>= 1
    eye = jnp.eye(S5_BUNDLE, dtype=F32)
    nb, gb = S5_NBUNDLES, S5_BUNDLE

    def b_slab(bb):
        return jnp.einsum("bgnc,gh->bgchn", bb.reshape(nb, gb, S5_STATE, S5_GROUP), eye).reshape(
            nb, LANES, S5_BCOLS)

    def c_slab(cm):
        return jnp.einsum("bgcn,gh->bgnhc", cm.reshape(nb, gb, S5_GROUP, S5_STATE), eye).reshape(
            nb, S5_BCOLS, LANES)

    return {
        "wb": jnp.concatenate([b_slab(bb_re), b_slab(bb_im)], axis=-1).astype(BF16),
        "wc": jnp.concatenate([c_slab(c_re), -c_slab(c_im)], axis=1).astype(BF16),
        "a_re": ab_re.reshape(1, S5_COLS), "a_im": ab_im.reshape(1, S5_COLS),
        "ap_re": pr.reshape(1, S5_COLS), "ap_im": pi.reshape(1, S5_COLS),
        "d": d_skip.reshape(1, S5_WIDTH), "w_glu": w_glu.astype(BF16),
        "b_glu": b_glu.reshape(1, S5_WIDTH),
    }


def _mix_kernel(zp_ref, q_ref, k_ref, v_ref, r_ref, g_ref, pw_ref, ps_ref, wa_ref, ba_ref, gnw_ref,
                tri_ref, ypool_ref, ygla_ref, zext, s_ref, *, tm):
    i = pl.program_id(0)

    @pl.when(i == 0)
    def _init():
        zext[0:POOL_HALO, :] = jnp.zeros((POOL_HALO, POOL_WIDTH), F32)
        s_ref[...] = jnp.zeros(s_ref.shape, F32)

    zext[POOL_HALO:POOL_HALO + tm, :] = zp_ref[...]
    pos = (i * tm + 1 + lax.broadcasted_iota(jnp.int32, (tm, 1), 0)).astype(F32)
    for gi, w in enumerate(POOL_WINDOWS):
        cols = slice(gi * POOL_GROUP, (gi + 1) * POOL_GROUP)
        ze = zext[:, cols]
        s, span = ze, 1
        while span < w:
            s = s + pltpu.roll(s, span, axis=0)
            span *= 2
        zc = ze[POOL_HALO:, :]
        pooled = s[POOL_HALO:, :] / jnp.minimum(pos, float(w)) - zc
        mixed = _dot(pooled.astype(BF16), pw_ref[gi]) * ps_ref[:, cols]
        ypool_ref[:, cols] = mixed.astype(BF16)
    zext[0:POOL_HALO, :] = zext[tm:tm + POOL_HALO, :]

    logit = _dot(g_ref[...].astype(BF16), wa_ref[...]) + ba_ref[...]
    la = (jnp.minimum(logit, 0.0) - jnp.log(1.0 + jnp.exp(-jnp.abs(logit)))) / GLA_TAU
    la_hi = la.astype(BF16)
    la_lo = (la - la_hi.astype(F32)).astype(BF16)
    tri = tri_ref[...]
    b = _dot(tri, la_hi) + _dot(tri, la_lo)
    nch = tm // GLA_CHUNK
    b_last = jnp.concatenate(
        [jnp.broadcast_to(b[(c + 1) * GLA_CHUNK - 1:(c + 1) * GLA_CHUNK, :], (GLA_CHUNK, GLA_K_WIDTH))
         for c in range(nch)], axis=0)
    q_dec = (q_ref[...] * (GLA_DK ** -0.5) * jnp.exp(b)).astype(BF16)
    k = k_ref[...]
    k_dec = (k * jnp.exp(-b)).astype(BF16)
    k_end = (k * jnp.exp(b_last - b)).astype(BF16)
    decay = jnp.exp(b_last)
    causal = (lax.broadcasted_iota(jnp.int32, (GLA_CHUNK, GLA_CHUNK), 0)
              >= lax.broadcasted_iota(jnp.int32, (GLA_CHUNK, GLA_CHUNK), 1))
    gnw = gnw_ref[...]
    units = [(hd, c) for hd in range(GLA_HEADS) for c in range(nch)]
    rows = lambda c: slice(c * GLA_CHUNK, (c + 1) * GLA_CHUNK)
    kcol = lambda hd: slice(hd * GLA_DK, (hd + 1) * GLA_DK)
    vcol = lambda hd: slice(hd * GLA_DV, (hd + 1) * GLA_DV)
    vv = {(hd, c): v_ref[rows(c), vcol(hd)].astype(BF16) for hd, c in units}
    scores = {(hd, c): _dot_nt(q_dec[rows(c), kcol(hd)], k_dec[rows(c), kcol(hd)]) for hd, c in units}
    kv = {(hd, c): _dot_tn(vv[hd, c], k_end[rows(c), kcol(hd)]) for hd, c in units}
    st_in = {}
    for hd in range(GLA_HEADS):
        st = s_ref[hd]
        for c in range(nch):
            st_in[hd, c] = st.astype(BF16)
            st = decay[c * GLA_CHUNK:c * GLA_CHUNK + 1, kcol(hd)] * st + kv[hd, c]
        s_ref[hd] = st
    for hd, c in units:
        sc = jnp.where(causal, scores[hd, c], 0.0).astype(BF16)
        o = _dot(sc, vv[hd, c]) + _dot_nt(q_dec[rows(c), kcol(hd)], st_in[hd, c])
        o = _rms(o, gnw)
        rr = r_ref[rows(c), vcol(hd)]
        ygla_ref[rows(c), vcol(hd)] = (o * (rr * _sigmoid(rr))).astype(BF16)


def _mixers(z, prm, layer, tm):
    L = z.shape[0]
    row = lambda w, col: pl.BlockSpec((tm, w), lambda i: (i, col // w))
    lb = lambda name: _layer_block(prm[name], layer)
    return pl.pallas_call(
        functools.partial(_mix_kernel, tm=tm),
        grid=(L // tm,),
        in_specs=[row(POOL_WIDTH, Z_POOL), row(GLA_K_WIDTH, Z_Q), row(GLA_K_WIDTH, Z_K),
                  row(GLA_V_WIDTH, Z_V), row(GLA_V_WIDTH, Z_R), row(LANES, Z_G),
                  lb("pool_w"), lb("pool_scale"), lb("w_a2"), lb("b_a"), lb("gla_norm_w"),
                  _resident((tm, tm))],
        out_specs=[pl.BlockSpec((tm, POOL_WIDTH), lambda i: (i, 0)),
                   pl.BlockSpec((tm, GLA_V_WIDTH), lambda i: (i, 0))],
        out_shape=[jax.ShapeDtypeStruct((L, POOL_WIDTH), BF16),
                   jax.ShapeDtypeStruct((L, GLA_V_WIDTH), BF16)],
        scratch_shapes=[pltpu.VMEM((tm + POOL_HALO, POOL_WIDTH), F32),
                        pltpu.VMEM((GLA_HEADS, GLA_DV, GLA_DK), F32)],
        compiler_params=_cparams(1),
        name="mixers",
    )(z, z, z, z, z, z, prm["pool_w"], prm["pool_scale"], prm["w_a2"], prm["b_a"], prm["gla_norm_w"],
      prm["tri"])


def _outproj_kernel(h_ref, ys_ref, yp_ref, yg_ref, w_ref, nw_ref, o_ref, a_ref, m_ref):
    m_ref[:, 0:S5_WIDTH] = ys_ref[...]
    m_ref[:, S5_WIDTH:S5_WIDTH + POOL_WIDTH] = yp_ref[...]
    m_ref[:, S5_WIDTH + POOL_WIDTH:] = yg_ref[...]
    m = m_ref[...]
    for c0, c1 in _col_chunks(D_MODEL):
        o_ref[:, c0:c1] = h_ref[:, c0:c1] + _dot(m, w_ref[:, c0:c1])
    a_ref[...] = _rms(o_ref[...], nw_ref[...]).astype(BF16)


def _outproj(h, ys, yp, yg, w, nw, layer, tm):
    L = h.shape[0]
    row = lambda wd: pl.BlockSpec((tm, wd), lambda i: (i, 0))
    return pl.pallas_call(
        _outproj_kernel,
        grid=(L // tm,),
        in_specs=[row(D_MODEL), row(S5_WIDTH), row(POOL_WIDTH), row(GLA_V_WIDTH),
                  pl.BlockSpec((None, D_MODEL, D_MODEL), lambda i: (layer, 0, 0),
                               pipeline_mode=pl.Buffered(1)),
                  pl.BlockSpec((None, 1, D_MODEL), lambda i: (layer, 0, 0))],
        out_specs=[row(D_MODEL), row(D_MODEL)],
        out_shape=[jax.ShapeDtypeStruct((L, D_MODEL), F32), jax.ShapeDtypeStruct((L, D_MODEL), BF16)],
        scratch_shapes=[pltpu.VMEM((tm, D_MODEL), BF16)],
        compiler_params=_cparams(1),
        name="outproj",
    )(h, ys, yp, yg, w, nw)


FFN_FC = 512
FFN_NC = D_FF // FFN_FC
FFN_FN = 512
FFN_NN = D_MODEL // FFN_FN
CONV_HALO = SUBLANES


def _serpentine(i, k, n):
    k = jnp.clip(k, 0, n - 1)
    return jnp.where(i % 2 == 0, k, n - 1 - k)


def _ffn_kernel(a_ref, h_ref, wg_ref, wv_ref, cw_ref, cb_ref, wd_ref, o_ref, act_ref, graw, carry, *, tm):
    i, s = pl.program_id(0), pl.program_id(1)

    @pl.when(s < FFN_NC)
    def _up():
        c = _serpentine(i, s, FFN_NC)
        cols = pl.ds(pl.multiple_of(c * FFN_FC, FFN_FC), FFN_FC)
        a = a_ref[...]
        graw[0:CONV_HALO, :] = jnp.where(i > 0, carry[c], 0.0)
        graw[CONV_HALO:CONV_HALO + tm, :] = _dot(a, wg_ref[...])
        carry[c] = graw[tm:tm + CONV_HALO, :]
        cw = cw_ref[:, cols]
        g = graw[...]
        back = lambda k: pltpu.roll(g, k, axis=0)[CONV_HALO:, :]
        gc = cb_ref[:, cols] + back(2) * cw[0:1, :]
        gc = gc + back(1) * cw[1:2, :]
        gc = gc + g[CONV_HALO:, :] * cw[2:3, :]
        act_ref[:, cols] = (gc * _sigmoid(gc) * _dot(a, wv_ref[...])).astype(BF16)

    @pl.when(s >= FFN_NC)
    def _down():
        o_ref[...] = h_ref[...] + _dot(act_ref[...], wd_ref[...])


def _ffn(a, h, w_up, conv_w, conv_b, w_down, layer, tm):
    L = h.shape[0]
    up_c = lambda i, s: _serpentine(i, s, FFN_NC)
    down_n = lambda i, s: _serpentine(i, s - FFN_NC, FFN_NN)
    return pl.pallas_call(
        functools.partial(_ffn_kernel, tm=tm),
        grid=(L // tm, FFN_NC + FFN_NN),
        in_specs=[pl.BlockSpec((tm, D_MODEL), lambda i, s: (i, 0)),
                  pl.BlockSpec((tm, FFN_FN), lambda i, s: (i, down_n(i, s))),
                  pl.BlockSpec((D_MODEL, FFN_FC), lambda i, s: (0, up_c(i, s))),
                  pl.BlockSpec((D_MODEL, FFN_FC), lambda i, s: (0, FFN_NC + up_c(i, s))),
                  pl.BlockSpec((None, 3, D_FF), lambda i, s: (layer, 0, 0)),
                  pl.BlockSpec((None, 1, D_FF), lambda i, s: (layer, 0, 0)),
                  pl.BlockSpec((D_FF, FFN_FN), lambda i, s: (0, down_n(i, s)))],
        out_specs=pl.BlockSpec((tm, FFN_FN), lambda i, s: (i, down_n(i, s))),
        out_shape=jax.ShapeDtypeStruct((L, D_MODEL), F32),
        scratch_shapes=[pltpu.VMEM((tm, D_FF), BF16),
                        pltpu.VMEM((tm + CONV_HALO, FFN_FC), F32),
                        pltpu.VMEM((FFN_NC, CONV_HALO, FFN_FC), F32)],
        compiler_params=_cparams(2),
        name="ffn",
    )(a, h, w_up, w_up, conv_w, conv_b, w_down)


def _ple_kernel(*refs, final):
    if final:
        h_ref, p_ref, nw_ref, wpg_ref, wple_ref, fw_ref, o_ref, a_ref = refs
    else:
        h_ref, p_ref, nw_ref, wpg_ref, wple_ref, o_ref, a_ref = refs
    rinv = _rms_split(h_ref, nw_ref, a_ref, MM_TN)
    a = a_ref[...]
    pe = p_ref[...].astype(BF16)
    for c0, c1 in _col_chunks(D_MODEL):
        gate = _sigmoid(rinv * _dot(a, wpg_ref[:, c0:c1]))
        o_ref[:, c0:c1] = h_ref[:, c0:c1] + _dot(pe, wple_ref[:, c0:c1]) * gate
    if final:
        o_ref[...] = _rms(o_ref[...], fw_ref[...])


def _ple(h, p, nw, w_pg, w_ple, final_w, layer, tm):
    L = h.shape[0]
    final = final_w is not None
    in_specs = [pl.BlockSpec((tm, D_MODEL), lambda i: (i, 0)),
                pl.BlockSpec((None, tm, PLE_DIM), lambda i: (layer, i, 0)),
                pl.BlockSpec((None, 1, D_MODEL), lambda i: (layer, 0, 0)),
                pl.BlockSpec((None, D_MODEL, D_MODEL), lambda i: (layer, 0, 0),
                             pipeline_mode=pl.Buffered(1)),
                pl.BlockSpec((None, PLE_DIM, D_MODEL), lambda i: (layer, 0, 0),
                             pipeline_mode=pl.Buffered(1))]
    args = [h, p, nw, w_pg, w_ple]
    if final:
        in_specs.append(_resident((1, D_MODEL)))
        args.append(final_w)
    return pl.pallas_call(
        functools.partial(_ple_kernel, final=final),
        grid=(L // tm,),
        in_specs=in_specs,
        out_specs=pl.BlockSpec((tm, D_MODEL), lambda i: (i, 0)),
        out_shape=jax.ShapeDtypeStruct((L, D_MODEL), F32),
        scratch_shapes=[pltpu.VMEM((tm, D_MODEL), BF16)],
        compiler_params=_cparams(1),
        name="ple_final" if final else "ple",
    )(*args)


def _chunk_tril(tm):
    r = jnp.arange(tm)
    same = (r[:, None] // GLA_CHUNK) == (r[None, :] // GLA_CHUNK)
    return (same & (r[:, None] >= r[None, :])).astype(BF16)


def kernel(x, p, norm_mix_w, w_in, s5_a_re, s5_a_im, s5_log_dt, s5_b_re, s5_b_im, s5_c_re, s5_c_im, s5_d, s5_w_glu, s5_b_glu, pool_w, pool_scale, gla_w_a2, gla_b_a, gla_norm_w, w_out, norm_ffn_w, w_up, conv_w, conv_b, w_down, norm_ple_w, w_ple, w_pg, final_norm_w):
    bsz, L, _ = x.shape
    assert bsz == 1 and L % (S5_SEGMENTS * S5_TB) == 0
    depth = w_in.shape[0]
    tm = min(512, L)
    tm_ffn = min(1024, L)
    tm_mix = min(512, L)
    tri = _chunk_tril(tm_mix)
    nw_ffn = norm_ffn_w.reshape(depth, 1, D_MODEL)
    conv_b3 = conv_b.reshape(depth, 1, D_FF)
    w_in_b = jnp.pad(w_in.astype(BF16), ((0, 0), (0, 0), (0, Z_WIDTH - w_in.shape[-1])))
    w_out_b, w_pg_b, w_ple_b = w_out.astype(BF16), w_pg.astype(BF16), w_ple.astype(BF16)
    nw_mix = norm_mix_w.reshape(depth, 1, D_MODEL)
    nw_ple = norm_ple_w.reshape(depth, 1, D_MODEL)
    p3 = p.reshape(depth, L, PLE_DIM)
    s5_prm = jax.vmap(functools.partial(_s5_prepare, seg_len=L // S5_SEGMENTS))(
        s5_a_re, s5_a_im, s5_log_dt, s5_b_re, s5_b_im, s5_c_re, s5_c_im, s5_d, s5_w_glu, s5_b_glu)
    mix_prm = {
        "pool_w": pool_w.astype(BF16), "pool_scale": pool_scale.reshape(depth, 1, POOL_WIDTH),
        "w_a2": jnp.pad(gla_w_a2, ((0, 0), (0, LANES - GLA_GATE_RANK), (0, 0))).astype(BF16),
        "b_a": gla_b_a.reshape(depth, 1, GLA_K_WIDTH), "gla_norm_w": gla_norm_w.reshape(depth, 1, GLA_DV),
        "tri": tri,
    }
    h = x.reshape(L, D_MODEL)
    for i in range(depth):
        z = _inproj(h, nw_mix, w_in_b, i, tm)
        y_s5, w_down_b, w_up_b = _s5_mixer(z, s5_prm, w_down, w_up, i, S5_TB)
        y_pool, y_gla = _mixers(z, mix_prm, i, tm_mix)
        h, a = _outproj(h, y_s5, y_pool, y_gla, w_out_b, nw_ffn, i, tm)
        h = _ffn(a, h, w_up_b, conv_w, conv_b3, w_down_b, i, tm_ffn)
        final_w = final_norm_w.reshape(1, D_MODEL) if i == depth - 1 else None
        h = _ple(h, p3, nw_ple, w_pg_b, w_ple_b, final_w, i, tm)
    return h.reshape(bsz, L, D_MODEL)
```

```python
import functools
import math

import jax
import jax.numpy as jnp
from jax import lax
from jax.experimental import pallas as pl
from jax.experimental.pallas import tpu as pltpu

F32 = jnp.float32
BF16 = jnp.bfloat16

D_MODEL = 2048
S5_WIDTH = 512
S5_GROUP = 16
S5_GROUPS = 32
S5_STATE = 64
S5_COLS = S5_GROUPS * S5_STATE
POOL_WIDTH = 512
POOL_WINDOWS = (2, 4, 8, 16)
POOL_GROUP = 128
GLA_HEADS = 4
GLA_DK = 128
GLA_DV = 256
GLA_K_WIDTH = 512
GLA_V_WIDTH = 1024
GLA_GATE_RANK = 16
GLA_TAU = 16.0
GLA_CHUNK = 64
D_FF = 5632
PLE_DIM = 256
EPS = 1e-6

LANES = 128
SUBLANES = 8
S5_SEGMENTS = SUBLANES
S5_BUNDLE = LANES // S5_GROUP
S5_NBUNDLES = S5_GROUPS // S5_BUNDLE
S5_BCOLS = S5_BUNDLE * S5_STATE
S5_TB = 128
S5_STRIDE = 4
POOL_HALO = 16

Z_S5, Z_POOL, Z_Q, Z_K, Z_V, Z_R, Z_G = 0, 512, 1024, 1536, 2048, 3072, 4096
Z_WIDTH = 4224
MM_TN = 512

VMEM_LIMIT = 56 * 1024 * 1024


def _cparams(n_axes):
    return pltpu.CompilerParams(dimension_semantics=("arbitrary",) * n_axes,
                                vmem_limit_bytes=VMEM_LIMIT)


def _resident(shape):
    nd = len(shape)
    return pl.BlockSpec(shape, lambda *_: (0,) * nd, pipeline_mode=pl.Buffered(1))


def _layer_block(arr, layer):
    nd = arr.ndim
    return pl.BlockSpec((None,) + arr.shape[1:], lambda *_: (layer,) + (0,) * (nd - 1),
                        pipeline_mode=pl.Buffered(1))


def _rms(x, w):
    ms = jnp.mean(x * x, axis=-1, keepdims=True)
    return x * lax.rsqrt(ms + EPS) * w


def _sigmoid(x):
    return 0.5 * (1.0 + jnp.tanh(0.5 * x))


def _dot(a, b):
    return jnp.dot(a, b, preferred_element_type=F32)


def _dot_nt(a, b):
    return lax.dot_general(a, b, (((1,), (1,)), ((), ())), preferred_element_type=F32)


def _dot_tn(a, b):
    return lax.dot_general(a, b, (((0,), (0,)), ((), ())), preferred_element_type=F32)


def _col_chunks(n):
    return [(c0, min(c0 + MM_TN, n)) for c0 in range(0, n, MM_TN)]


MXU_K = 256


def _rms_split(h_ref, w_ref, a_ref, width):
    sq = None
    for k0 in range(0, h_ref.shape[1], MXU_K):
        hk = h_ref[:, k0:k0 + MXU_K]
        a_ref[:, k0:k0 + MXU_K] = (hk * w_ref[:, k0:k0 + MXU_K]).astype(BF16)
        sq = hk * hk if sq is None else sq + hk * hk
    rinv = lax.rsqrt(jnp.sum(sq, axis=-1, keepdims=True) * (1.0 / h_ref.shape[1]) + EPS)
    return jnp.broadcast_to(rinv, (h_ref.shape[0], width))


def _inproj_kernel(h_ref, nw_ref, w_ref, z_ref, a_ref):
    rinv = _rms_split(h_ref, nw_ref, a_ref, MM_TN)
    a = a_ref[...]
    zt = jnp.concatenate([_dot(a, w_ref[:, Z_R + c0:Z_R + c1]) for c0, c1 in _col_chunks(Z_WIDTH - Z_R)],
                         axis=1)
    z_ref[:, Z_G:] = rinv[:, :LANES] * zt[:, :LANES]
    r = zt[:, GLA_GATE_RANK:GLA_GATE_RANK + GLA_V_WIDTH]
    for c0, c1 in _col_chunks(GLA_V_WIDTH):
        z_ref[:, Z_R + c0:Z_R + c1] = rinv * r[:, c0:c1]
    for c0, c1 in _col_chunks(Z_R):
        z_ref[:, c0:c1] = rinv * _dot(a, w_ref[:, c0:c1])


def _inproj(h, nw, w, layer, tm):
    L = h.shape[0]
    return pl.pallas_call(
        _inproj_kernel,
        grid=(L // tm,),
        in_specs=[pl.BlockSpec((tm, D_MODEL), lambda i: (i, 0)),
                  pl.BlockSpec((None, 1, D_MODEL), lambda i: (layer, 0, 0)),
                  _layer_block(w, layer)],
        out_specs=pl.BlockSpec((tm, Z_WIDTH), lambda i: (i, 0)),
        out_shape=jax.ShapeDtypeStruct((L, Z_WIDTH), F32),
        scratch_shapes=[pltpu.VMEM((tm, D_MODEL), BF16)],
        compiler_params=_cparams(1),
        name="inproj",
    )(h, nw, w)


def _gelu_tanh(x):
    return 0.5 * x * (1.0 + jnp.tanh(math.sqrt(2.0 / math.pi) * (x + 0.044715 * (x * x * x))))


def _s5_kernel(*refs, tb, pass2):
    if pass2:
        (u_ref, wa_ref, are_ref, aim_ref, ride_ref, xe_re_ref, xe_im_ref, ap_re_ref, ap_im_ref,
         wy_ref, d_ref, wglu_ref, bglu_ref, y_ref, cast_ref, ubuf, xb, st_re, st_im) = refs
    else:
        (u_ref, wa_ref, are_ref, aim_ref, ride_ref, xe_re_ref, xe_im_ref, cast_ref,
         ubuf, st_re, st_im) = refs
    nseg, s = S5_SEGMENTS, S5_STRIDE
    nsub = tb // s
    prow = nsub * nseg
    cast_ref[...] = ride_ref[...].astype(BF16)

    @pl.when(pl.program_id(0) == 0)
    def _init():
        if pass2:
            apr, api = ap_re_ref[...], ap_im_ref[...]
            st_re[0:1, :] = jnp.zeros((1, S5_COLS), F32)
            st_im[0:1, :] = jnp.zeros((1, S5_COLS), F32)
            for j in range(nseg - 1):
                cr, ci = st_re[j:j + 1, :], st_im[j:j + 1, :]
                st_re[j + 1:j + 2, :] = apr * cr - api * ci + xe_re_ref[j:j + 1, :]
                st_im[j + 1:j + 2, :] = apr * ci + api * cr + xe_im_ref[j:j + 1, :]
        else:
            st_re[...] = jnp.zeros((nseg, S5_COLS), F32)
            st_im[...] = jnp.zeros((nseg, S5_COLS), F32)

    for j in range(nseg):
        uj = u_ref[j]
        for b in range(S5_NBUNDLES):
            ubuf[b, pl.ds(j, tb, stride=nseg), :] = uj[:, b * LANES:(b + 1) * LANES]
    ys = []
    for b in range(S5_NBUNDLES):
        v = ubuf[b].reshape(nsub, s * nseg, LANES)
        lhs = jnp.concatenate([v[:, k * nseg:(k + 1) * nseg, :].reshape(prow, LANES) for k in range(s)],
                              axis=1).astype(BF16)
        bu = _dot(lhs, wa_ref[b])
        cols = slice(b * S5_BCOLS, (b + 1) * S5_BCOLS)
        cre = slice(2 * b * S5_BCOLS, (2 * b + 1) * S5_BCOLS)
        cim = slice((2 * b + 1) * S5_BCOLS, (2 * b + 2) * S5_BCOLS)
        ar = jnp.broadcast_to(are_ref[:, cols], (nseg, S5_BCOLS))
        ai = jnp.broadcast_to(aim_ref[:, cols], (nseg, S5_BCOLS))
        xr, xi = st_re[:, cols], st_im[:, cols]
        for m in range(nsub):
            rows = slice(m * nseg, (m + 1) * nseg)
            if pass2:
                xb[rows, cre] = xr
                xb[rows, cim] = xi
            xr, xi = (ar * xr - ai * xi + bu[rows, :S5_BCOLS], ar * xi + ai * xr + bu[rows, S5_BCOLS:])
        st_re[:, cols] = xr
        st_im[:, cols] = xi
        if pass2:
            xp = xb[:, 2 * b * S5_BCOLS:2 * (b + 1) * S5_BCOLS].astype(BF16)
            yp = _dot(jnp.concatenate([lhs, xp], axis=1), wy_ref[b])
            ys.append(jnp.concatenate(
                [yp[:, k * LANES:(k + 1) * LANES].reshape(nsub, nseg, LANES) for k in range(s)],
                axis=1).reshape(tb * nseg, LANES))

    if not pass2:
        xe_re_ref[...] = st_re[...]
        xe_im_ref[...] = st_im[...]
        return

    u = jnp.concatenate([ubuf[b] for b in range(S5_NBUNDLES)], axis=1)
    y = jnp.concatenate(ys, axis=1) + d_ref[...] * u
    y = _gelu_tanh(y)
    glu = _sigmoid(_dot(y.astype(BF16), wglu_ref[...]) + bglu_ref[...])
    out = y * glu
    for b in range(S5_NBUNDLES):
        ubuf[b] = out[:, b * LANES:(b + 1) * LANES]
    for j in range(nseg):
        y_ref[j] = jnp.concatenate(
            [ubuf[b, pl.ds(j, tb, stride=nseg), :] for b in range(S5_NBUNDLES)], axis=1).astype(BF16)


def _s5_mixer(z, prm, ride1, ride2, layer, tb):
    L = z.shape[0]
    lb = lambda name: _layer_block(prm[name], layer)
    nseg = S5_SEGMENTS
    seg_len = L // nseg
    nblk = seg_len // tb
    rows = tb * nseg

    def rider(w):
        slab = w.shape[1] // nblk
        assert slab * nblk == w.shape[1] and slab % (2 * SUBLANES) == 0
        return (pl.BlockSpec((None, slab, w.shape[2]), lambda i: (layer, i, 0)),
                pl.BlockSpec((slab, w.shape[2]), lambda i: (i, 0)),
                jax.ShapeDtypeStruct(w.shape[1:], BF16))

    r1_in, r1_out, r1_shape = rider(ride1)
    r2_in, r2_out, r2_shape = rider(ride2)
    z3 = z.reshape(nseg, seg_len, Z_WIDTH)
    u_spec = pl.BlockSpec((nseg, tb, S5_WIDTH), lambda i: (0, i, Z_S5 // S5_WIDTH))
    state_shape = jax.ShapeDtypeStruct((nseg, S5_COLS), F32)
    ubuf = pltpu.VMEM((S5_NBUNDLES, rows, LANES), F32)
    state = [pltpu.VMEM((nseg, S5_COLS), F32), pltpu.VMEM((nseg, S5_COLS), F32)]
    common = [u_spec, lb("wa"), lb("as_re"), lb("as_im")]

    xe_re, xe_im, cast1 = pl.pallas_call(
        functools.partial(_s5_kernel, tb=tb, pass2=False),
        grid=(nblk,),
        in_specs=common + [r1_in],
        out_specs=[_resident_out((nseg, S5_COLS)), _resident_out((nseg, S5_COLS)), r1_out],
        out_shape=[state_shape, state_shape, r1_shape],
        scratch_shapes=[ubuf] + state,
        compiler_params=_cparams(1),
        name="s5_states",
    )(z3, prm["wa"], prm["as_re"], prm["as_im"], ride1)

    y, cast2 = pl.pallas_call(
        functools.partial(_s5_kernel, tb=tb, pass2=True),
        grid=(nblk,),
        in_specs=common + [r2_in, _resident((nseg, S5_COLS)), _resident((nseg, S5_COLS)),
                           lb("ap_re"), lb("ap_im"), lb("wy"), lb("d"), lb("w_glu"), lb("b_glu")],
        out_specs=[pl.BlockSpec((nseg, tb, S5_WIDTH), lambda i: (0, i, 0)), r2_out],
        out_shape=[jax.ShapeDtypeStruct((nseg, seg_len, S5_WIDTH), BF16), r2_shape],
        scratch_shapes=[ubuf, pltpu.VMEM((rows // S5_STRIDE, 2 * S5_COLS), F32)] + state,
        compiler_params=_cparams(1),
        name="s5_outputs",
    )(z3, prm["wa"], prm["as_re"], prm["as_im"], ride2, xe_re, xe_im, prm["ap_re"], prm["ap_im"],
      prm["wy"], prm["d"], prm["w_glu"], prm["b_glu"])
    return y.reshape(L, S5_WIDTH), cast1, cast2


def _resident_out(shape):
    nd = len(shape)
    return pl.BlockSpec(shape, lambda *_: (0,) * nd)


def _s5_prepare(a_re, a_im, log_dt, b_re, b_im, c_re, c_im, d_skip, w_glu, b_glu, seg_len):
    dt = jnp.exp(log_dt)[:, None]
    mag = jnp.exp(a_re * dt)
    ab_re, ab_im = mag * jnp.cos(a_im * dt), mag * jnp.sin(a_im * dt)
    nr, ni = ab_re - 1.0, ab_im
    den = a_re * a_re + a_im * a_im
    f_re, f_im = (nr * a_re + ni * a_im) / den, (ni * a_re - nr * a_im) / den
    bb_re = f_re[..., None] * b_re - f_im[..., None] * b_im
    bb_im = f_re[..., None] * b_im + f_im[..., None] * b_re
    pr, pi = jnp.ones_like(ab_re), jnp.zeros_like(ab_re)
    sr, si, e = ab_re, ab_im, seg_len
    while e:
        if e & 1:
            pr, pi = pr * sr - pi * si, pr * si + pi * sr
        sr, si = sr * sr - si * si, 2.0 * sr * si
        e >>= 1
    eye = jnp.eye(S5_BUNDLE, dtype=F32)
    nb, gb = S5_NBUNDLES, S5_BUNDLE

    def b_slab(bb):
        return jnp.einsum("bgnc,gh->bgchn", bb.reshape(nb, gb, S5_STATE, S5_GROUP), eye).reshape(
            nb, LANES, S5_BCOLS)

    def c_slab(cm):
        return jnp.einsum("bgcn,gh->bgnhc", cm.reshape(nb, gb, S5_GROUP, S5_STATE), eye).reshape(
            nb, S5_BCOLS, LANES)

    def k_slab(km):
        return jnp.einsum("bgoc,gh->bgcho", km.reshape(nb, gb, S5_GROUP, S5_GROUP), eye).reshape(
            nb, LANES, LANES)

    s = S5_STRIDE
    pw = [(jnp.ones_like(ab_re), jnp.zeros_like(ab_re))]
    for _ in range(s):
        qr, qi = pw[-1]
        pw.append((qr * ab_re - qi * ab_im, qr * ab_im + qi * ab_re))
    wa = []
    for k in range(s):
        qr, qi = pw[s - 1 - k]
        wa.append(jnp.concatenate([b_slab(qr[..., None] * bb_re - qi[..., None] * bb_im),
                                   b_slab(qr[..., None] * bb_im + qi[..., None] * bb_re)], axis=-1))
    hr = [c_re * qr[:, None, :] - c_im * qi[:, None, :] for qr, qi in pw]
    hi = [c_re * qi[:, None, :] + c_im * qr[:, None, :] for qr, qi in pw]
    ws = jnp.concatenate([jnp.concatenate([c_slab(hr[k + 1]), -c_slab(hi[k + 1])], axis=1)
                          for k in range(s)], axis=2)
    klag = [k_slab(jnp.einsum("gon,gnc->goc", hr[l], bb_re) - jnp.einsum("gon,gnc->goc", hi[l], bb_im))
            for l in range(s)]
    zero = jnp.zeros_like(klag[0])
    wl = jnp.concatenate([jnp.concatenate([klag[k - j] if k >= j else zero for k in range(s)], axis=2)
                          for j in range(s)], axis=1)
    return {
        "wa": jnp.concatenate(wa, axis=1).astype(BF16),
        "wy": jnp.concatenate([wl, ws], axis=1).astype(BF16),
        "as_re": pw[s][0].reshape(1, S5_COLS), "as_im": pw[s][1].reshape(1, S5_COLS),
        "ap_re": pr.reshape(1, S5_COLS), "ap_im": pi.reshape(1, S5_COLS),
        "d": d_skip.reshape(1, S5_WIDTH), "w_glu": w_glu.astype(BF16),
        "b_glu": b_glu.reshape(1, S5_WIDTH),
    }


def _mix_kernel(zp_ref, q_ref, k_ref, v_ref, r_ref, g_ref, pw_ref, ps_ref, wa_ref, ba_ref, gnw_ref,
                tri_ref, ypool_ref, ygla_ref, zext, s_ref, *, tm):
    i = pl.program_id(0)

    @pl.when(i == 0)
    def _init():
        zext[0:POOL_HALO, :] = jnp.zeros((POOL_HALO, POOL_WIDTH), F32)
        s_ref[...] = jnp.zeros(s_ref.shape, F32)

    zext[POOL_HALO:POOL_HALO + tm, :] = zp_ref[...]
    pos = (i * tm + 1 + lax.broadcasted_iota(jnp.int32, (tm, 1), 0)).astype(F32)
    for gi, w in enumerate(POOL_WINDOWS):
        cols = slice(gi * POOL_GROUP, (gi + 1) * POOL_GROUP)
        ze = zext[:, cols]
        s, span = ze, 1
        while span < w:
            s = s + pltpu.roll(s, span, axis=0)
            span *= 2
        zc = ze[POOL_HALO:, :]
        pooled = s[POOL_HALO:, :] / jnp.minimum(pos, float(w)) - zc
        mixed = _dot(pooled.astype(BF16), pw_ref[gi]) * ps_ref[:, cols]
        ypool_ref[:, cols] = mixed.astype(BF16)
    zext[0:POOL_HALO, :] = zext[tm:tm + POOL_HALO, :]

    logit = _dot(g_ref[...].astype(BF16), wa_ref[...]) + ba_ref[...]
    la = (jnp.minimum(logit, 0.0) - jnp.log(1.0 + jnp.exp(-jnp.abs(logit)))) / GLA_TAU
    la_hi = la.astype(BF16)
    la_lo = (la - la_hi.astype(F32)).astype(BF16)
    tri = tri_ref[...]
    b = _dot(tri, la_hi) + _dot(tri, la_lo)
    nch = tm // GLA_CHUNK
    b_last = jnp.concatenate(
        [jnp.broadcast_to(b[(c + 1) * GLA_CHUNK - 1:(c + 1) * GLA_CHUNK, :], (GLA_CHUNK, GLA_K_WIDTH))
         for c in range(nch)], axis=0)
    q_dec = (q_ref[...] * (GLA_DK ** -0.5) * jnp.exp(b)).astype(BF16)
    k = k_ref[...]
    k_dec = (k * jnp.exp(-b)).astype(BF16)
    k_end = (k * jnp.exp(b_last - b)).astype(BF16)
    decay = jnp.exp(b_last)
    causal = (lax.broadcasted_iota(jnp.int32, (GLA_CHUNK, GLA_CHUNK), 0)
              >= lax.broadcasted_iota(jnp.int32, (GLA_CHUNK, GLA_CHUNK), 1))
    gnw = gnw_ref[...]
    units = [(hd, c) for hd in range(GLA_HEADS) for c in range(nch)]
    rows = lambda c: slice(c * GLA_CHUNK, (c + 1) * GLA_CHUNK)
    kcol = lambda hd: slice(hd * GLA_DK, (hd + 1) * GLA_DK)
    vcol = lambda hd: slice(hd * GLA_DV, (hd + 1) * GLA_DV)
    vv = {(hd, c): v_ref[rows(c), vcol(hd)].astype(BF16) for hd, c in units}
    scores = {(hd, c): _dot_nt(q_dec[rows(c), kcol(hd)], k_dec[rows(c), kcol(hd)]) for hd, c in units}
    kv = {(hd, c): _dot_tn(vv[hd, c], k_end[rows(c), kcol(hd)]) for hd, c in units}
    st_in = {}
    for hd in range(GLA_HEADS):
        st = s_ref[hd]
        for c in range(nch):
            st_in[hd, c] = st.astype(BF16)
            st = decay[c * GLA_CHUNK:c * GLA_CHUNK + 1, kcol(hd)] * st + kv[hd, c]
        s_ref[hd] = st
    for hd, c in units:
        sc = jnp.where(causal, scores[hd, c], 0.0).astype(BF16)
        o = _dot(sc, vv[hd, c]) + _dot_nt(q_dec[rows(c), kcol(hd)], st_in[hd, c])
        o = _rms(o, gnw)
        rr = r_ref[rows(c), vcol(hd)]
        ygla_ref[rows(c), vcol(hd)] = (o * (rr * _sigmoid(rr))).astype(BF16)


def _mixers(z, prm, layer, tm):
    L = z.shape[0]
    row = lambda w, col: pl.BlockSpec((tm, w), lambda i: (i, col // w))
    lb = lambda name: _layer_block(prm[name], layer)
    return pl.pallas_call(
        functools.partial(_mix_kernel, tm=tm),
        grid=(L // tm,),
        in_specs=[row(POOL_WIDTH, Z_POOL), row(GLA_K_WIDTH, Z_Q), row(GLA_K_WIDTH, Z_K),
                  row(GLA_V_WIDTH, Z_V), row(GLA_V_WIDTH, Z_R), row(LANES, Z_G),
                  lb("pool_w"), lb("pool_scale"), lb("w_a2"), lb("b_a"), lb("gla_norm_w"),
                  _resident((tm, tm))],
        out_specs=[pl.BlockSpec((tm, POOL_WIDTH), lambda i: (i, 0)),
                   pl.BlockSpec((tm, GLA_V_WIDTH), lambda i: (i, 0))],
        out_shape=[jax.ShapeDtypeStruct((L, POOL_WIDTH), BF16),
                   jax.ShapeDtypeStruct((L, GLA_V_WIDTH), BF16)],
        scratch_shapes=[pltpu.VMEM((tm + POOL_HALO, POOL_WIDTH), F32),
                        pltpu.VMEM((GLA_HEADS, GLA_DV, GLA_DK), F32)],
        compiler_params=_cparams(1),
        name="mixers",
    )(z, z, z, z, z, z, prm["pool_w"], prm["pool_scale"], prm["w_a2"], prm["b_a"], prm["gla_norm_w"],
      prm["tri"])


def _outproj_kernel(h_ref, ys_ref, yp_ref, yg_ref, w_ref, nw_ref, o_ref, a_ref, m_ref):
    m_ref[:, 0:S5_WIDTH] = ys_ref[...]
    m_ref[:, S5_WIDTH:S5_WIDTH + POOL_WIDTH] = yp_ref[...]
    m_ref[:, S5_WIDTH + POOL_WIDTH:] = yg_ref[...]
    m = m_ref[...]
    for c0, c1 in _col_chunks(D_MODEL):
        o_ref[:, c0:c1] = h_ref[:, c0:c1] + _dot(m, w_ref[:, c0:c1])
    a_ref[...] = _rms(o_ref[...], nw_ref[...]).astype(BF16)


def _outproj(h, ys, yp, yg, w, nw, layer, tm):
    L = h.shape[0]
    row = lambda wd: pl.BlockSpec((tm, wd), lambda i: (i, 0))
    return pl.pallas_call(
        _outproj_kernel,
        grid=(L // tm,),
        in_specs=[row(D_MODEL), row(S5_WIDTH), row(POOL_WIDTH), row(GLA_V_WIDTH),
                  pl.BlockSpec((None, D_MODEL, D_MODEL), lambda i: (layer, 0, 0),
                               pipeline_mode=pl.Buffered(1)),
                  pl.BlockSpec((None, 1, D_MODEL), lambda i: (layer, 0, 0))],
        out_specs=[row(D_MODEL), row(D_MODEL)],
        out_shape=[jax.ShapeDtypeStruct((L, D_MODEL), F32), jax.ShapeDtypeStruct((L, D_MODEL), BF16)],
        scratch_shapes=[pltpu.VMEM((tm, D_MODEL), BF16)],
        compiler_params=_cparams(1),
        name="outproj",
    )(h, ys, yp, yg, w, nw)


FFN_FC = 512
FFN_NC = D_FF // FFN_FC
FFN_FN = 512
FFN_NN = D_MODEL // FFN_FN
CONV_HALO = SUBLANES


def _serpentine(i, k, n):
    k = jnp.clip(k, 0, n - 1)
    return jnp.where(i % 2 == 0, k, n - 1 - k)


def _ffn_kernel(a_ref, h_ref, wg_ref, wv_ref, cw_ref, cb_ref, wd_ref, o_ref, act_ref, graw, carry, *, tm):
    i, s = pl.program_id(0), pl.program_id(1)

    @pl.when(s < FFN_NC)
    def _up():
        c = _serpentine(i, s, FFN_NC)
        cols = pl.ds(pl.multiple_of(c * FFN_FC, FFN_FC), FFN_FC)
        a = a_ref[...]
        graw[0:CONV_HALO, :] = jnp.where(i > 0, carry[c], 0.0)
        graw[CONV_HALO:CONV_HALO + tm, :] = _dot(a, wg_ref[...])
        carry[c] = graw[tm:tm + CONV_HALO, :]
        cw = cw_ref[:, cols]
        g = graw[...]
        back = lambda k: pltpu.roll(g, k, axis=0)[CONV_HALO:, :]
        gc = cb_ref[:, cols] + back(2) * cw[0:1, :]
        gc = gc + back(1) * cw[1:2, :]
        gc = gc + g[CONV_HALO:, :] * cw[2:3, :]
        act_ref[:, cols] = (gc * _sigmoid(gc) * _dot(a, wv_ref[...])).astype(BF16)

    @pl.when(s >= FFN_NC)
    def _down():
        o_ref[...] = h_ref[...] + _dot(act_ref[...], wd_ref[...])


def _ffn(a, h, w_up, conv_w, conv_b, w_down, layer, tm):
    L = h.shape[0]
    up_c = lambda i, s: _serpentine(i, s, FFN_NC)
    down_n = lambda i, s: _serpentine(i, s - FFN_NC, FFN_NN)
    return pl.pallas_call(
        functools.partial(_ffn_kernel, tm=tm),
        grid=(L // tm, FFN_NC + FFN_NN),
        in_specs=[pl.BlockSpec((tm, D_MODEL), lambda i, s: (i, 0)),
                  pl.BlockSpec((tm, FFN_FN), lambda i, s: (i, down_n(i, s))),
                  pl.BlockSpec((D_MODEL, FFN_FC), lambda i, s: (0, up_c(i, s))),
                  pl.BlockSpec((D_MODEL, FFN_FC), lambda i, s: (0, FFN_NC + up_c(i, s))),
                  pl.BlockSpec((None, 3, D_FF), lambda i, s: (layer, 0, 0)),
                  pl.BlockSpec((None, 1, D_FF), lambda i, s: (layer, 0, 0)),
                  pl.BlockSpec((D_FF, FFN_FN), lambda i, s: (0, down_n(i, s)))],
        out_specs=pl.BlockSpec((tm, FFN_FN), lambda i, s: (i, down_n(i, s))),
        out_shape=jax.ShapeDtypeStruct((L, D_MODEL), F32),
        scratch_shapes=[pltpu.VMEM((tm, D_FF), BF16),
                        pltpu.VMEM((tm + CONV_HALO, FFN_FC), F32),
                        pltpu.VMEM((FFN_NC, CONV_HALO, FFN_FC), F32)],
        compiler_params=_cparams(2),
        name="ffn",
    )(a, h, w_up, w_up, conv_w, conv_b, w_down)


def _ple_kernel(*refs, final):
    if final:
        h_ref, p_ref, nw_ref, wpg_ref, wple_ref, fw_ref, o_ref, a_ref = refs
    else:
        h_ref, p_ref, nw_ref, wpg_ref, wple_ref, o_ref, a_ref = refs
    rinv = _rms_split(h_ref, nw_ref, a_ref, MM_TN)
    a = a_ref[...]
    pe = p_ref[...].astype(BF16)
    for c0, c1 in _col_chunks(D_MODEL):
        gate = _sigmoid(rinv * _dot(a, wpg_ref[:, c0:c1]))
        o_ref[:, c0:c1] = h_ref[:, c0:c1] + _dot(pe, wple_ref[:, c0:c1]) * gate
    if final:
        o_ref[...] = _rms(o_ref[...], fw_ref[...])


def _ple(h, p, nw, w_pg, w_ple, final_w, layer, tm):
    L = h.shape[0]
    final = final_w is not None
    in_specs = [pl.BlockSpec((tm, D_MODEL), lambda i: (i, 0)),
                pl.BlockSpec((None, tm, PLE_DIM), lambda i: (layer, i, 0)),
                pl.BlockSpec((None, 1, D_MODEL), lambda i: (layer, 0, 0)),
                pl.BlockSpec((None, D_MODEL, D_MODEL), lambda i: (layer, 0, 0),
                             pipeline_mode=pl.Buffered(1)),
                pl.BlockSpec((None, PLE_DIM, D_MODEL), lambda i: (layer, 0, 0),
                             pipeline_mode=pl.Buffered(1))]
    args = [h, p, nw, w_pg, w_ple]
    if final:
        in_specs.append(_resident((1, D_MODEL)))
        args.append(final_w)
    return pl.pallas_call(
        functools.partial(_ple_kernel, final=final),
        grid=(L // tm,),
        in_specs=in_specs,
        out_specs=pl.BlockSpec((tm, D_MODEL), lambda i: (i, 0)),
        out_shape=jax.ShapeDtypeStruct((L, D_MODEL), F32),
        scratch_shapes=[pltpu.VMEM((tm, D_MODEL), BF16)],
        compiler_params=_cparams(1),
        name="ple_final" if final else "ple",
    )(*args)


def _chunk_tril(tm):
    r = jnp.arange(tm)
    same = (r[:, None] // GLA_CHUNK) == (r[None, :] // GLA_CHUNK)
    return (same & (r[:, None] >= r[None, :])).astype(BF16)


def kernel(x, p, norm_mix_w, w_in, s5_a_re, s5_a_im, s5_log_dt, s5_b_re, s5_b_im, s5_c_re, s5_c_im, s5_d, s5_w_glu, s5_b_glu, pool_w, pool_scale, gla_w_a2, gla_b_a, gla_norm_w, w_out, norm_ffn_w, w_up, conv_w, conv_b, w_down, norm_ple_w, w_ple, w_pg, final_norm_w):
    bsz, L, _ = x.shape
    tb_s5 = min(S5_TB, L // S5_SEGMENTS)
    assert bsz == 1 and L % (S5_SEGMENTS * tb_s5) == 0 and tb_s5 % S5_STRIDE == 0
    depth = w_in.shape[0]
    tm = min(512, L)
    tm_ffn = min(1024, L)
    tm_mix = min(512, L)
    tri = _chunk_tril(tm_mix)
    nw_ffn = norm_ffn_w.reshape(depth, 1, D_MODEL)
    conv_b3 = conv_b.reshape(depth, 1, D_FF)
    w_in_b = jnp.pad(w_in.astype(BF16), ((0, 0), (0, 0), (0, Z_WIDTH - w_in.shape[-1])))
    w_out_b, w_pg_b, w_ple_b = w_out.astype(BF16), w_pg.astype(BF16), w_ple.astype(BF16)
    nw_mix = norm_mix_w.reshape(depth, 1, D_MODEL)
    nw_ple = norm_ple_w.reshape(depth, 1, D_MODEL)
    p3 = p.reshape(depth, L, PLE_DIM)
    s5_prm = jax.vmap(functools.partial(_s5_prepare, seg_len=L // S5_SEGMENTS))(
        s5_a_re, s5_a_im, s5_log_dt, s5_b_re, s5_b_im, s5_c_re, s5_c_im, s5_d, s5_w_glu, s5_b_glu)
    mix_prm = {
        "pool_w": pool_w.astype(BF16), "pool_scale": pool_scale.reshape(depth, 1, POOL_WIDTH),
        "w_a2": jnp.pad(gla_w_a2, ((0, 0), (0, LANES - GLA_GATE_RANK), (0, 0))).astype(BF16),
        "b_a": gla_b_a.reshape(depth, 1, GLA_K_WIDTH), "gla_norm_w": gla_norm_w.reshape(depth, 1, GLA_DV),
        "tri": tri,
    }
    h = x.reshape(L, D_MODEL)
    for i in range(depth):
        z = _inproj(h, nw_mix, w_in_b, i, tm)
        y_s5, w_down_b, w_up_b = _s5_mixer(z, s5_prm, w_down, w_up, i, tb_s5)
        y_pool, y_gla = _mixers(z, mix_prm, i, tm_mix)
        h, a = _outproj(h, y_s5, y_pool, y_gla, w_out_b, nw_ffn, i, tm)
        h = _ffn(a, h, w_up_b, conv_w, conv_b3, w_down_b, i, tm_ffn)
        final_w = final_norm_w.reshape(1, D_MODEL) if i == depth - 1 else None
        h = _ple(h, p3, nw_ple, w_pg_b, w_ple_b, final_w, i, tm)
    return h.reshape(bsz, L, D_MODEL)
```

```python
import functools
import math

import jax
import jax.numpy as jnp
from jax import lax
from jax.experimental import pallas as pl
from jax.experimental.pallas import tpu as pltpu

F32 = jnp.float32
BF16 = jnp.bfloat16

D_MODEL = 2048
S5_WIDTH = 512
S5_GROUP = 16
S5_GROUPS = 32
S5_STATE = 64
S5_COLS = S5_GROUPS * S5_STATE
POOL_WIDTH = 512
POOL_WINDOWS = (2, 4, 8, 16)
POOL_GROUP = 128
GLA_HEADS = 4
GLA_DK = 128
GLA_DV = 256
GLA_K_WIDTH = 512
GLA_V_WIDTH = 1024
GLA_GATE_RANK = 16
GLA_TAU = 16.0
GLA_CHUNK = 64
D_FF = 5632
PLE_DIM = 256
EPS = 1e-6

LANES = 128
SUBLANES = 8
S5_SEGMENTS = SUBLANES
S5_BUNDLE = LANES // S5_GROUP
S5_NBUNDLES = S5_GROUPS // S5_BUNDLE
S5_BCOLS = S5_BUNDLE * S5_STATE
S5_TB = 128
S5_STRIDE = 4
POOL_HALO = 16

Z_S5, Z_POOL, Z_Q, Z_K, Z_V, Z_R, Z_G = 0, 512, 1024, 1536, 2048, 3072, 4096
Z_WIDTH = 4224
MM_TN = 512

VMEM_LIMIT = 56 * 1024 * 1024


def _cparams(n_axes):
    return pltpu.CompilerParams(dimension_semantics=("arbitrary",) * n_axes,
                                vmem_limit_bytes=VMEM_LIMIT)


def _resident(shape):
    nd = len(shape)
    return pl.BlockSpec(shape, lambda *_: (0,) * nd, pipeline_mode=pl.Buffered(1))


def _layer_block(arr, layer):
    nd = arr.ndim
    return pl.BlockSpec((None,) + arr.shape[1:], lambda *_: (layer,) + (0,) * (nd - 1),
                        pipeline_mode=pl.Buffered(1))


def _rms(x, w):
    ms = jnp.mean(x * x, axis=-1, keepdims=True)
    return x * lax.rsqrt(ms + EPS) * w


def _sigmoid(x):
    return 0.5 * (1.0 + jnp.tanh(0.5 * x))


def _dot(a, b):
    return jnp.dot(a, b, preferred_element_type=F32)


def _dot_nt(a, b):
    return lax.dot_general(a, b, (((1,), (1,)), ((), ())), preferred_element_type=F32)


def _dot_tn(a, b):
    return lax.dot_general(a, b, (((0,), (0,)), ((), ())), preferred_element_type=F32)


def _col_chunks(n):
    return [(c0, min(c0 + MM_TN, n)) for c0 in range(0, n, MM_TN)]


MXU_K = 256


def _rms_split(h_ref, w_ref, a_ref, width):
    sq = None
    for k0 in range(0, h_ref.shape[1], MXU_K):
        hk = h_ref[:, k0:k0 + MXU_K]
        a_ref[:, k0:k0 + MXU_K] = (hk * w_ref[:, k0:k0 + MXU_K]).astype(BF16)
        sq = hk * hk if sq is None else sq + hk * hk
    rinv = lax.rsqrt(jnp.sum(sq, axis=-1, keepdims=True) * (1.0 / h_ref.shape[1]) + EPS)
    return jnp.broadcast_to(rinv, (h_ref.shape[0], width))


def _inproj_kernel(h_ref, nw_ref, w_ref, z_ref, a_ref):
    rinv = _rms_split(h_ref, nw_ref, a_ref, MM_TN)
    a = a_ref[...]
    zt = jnp.concatenate([_dot(a, w_ref[:, Z_R + c0:Z_R + c1]) for c0, c1 in _col_chunks(Z_WIDTH - Z_R)],
                         axis=1)
    z_ref[:, Z_G:] = rinv[:, :LANES] * zt[:, :LANES]
    r = zt[:, GLA_GATE_RANK:GLA_GATE_RANK + GLA_V_WIDTH]
    for c0, c1 in _col_chunks(GLA_V_WIDTH):
        z_ref[:, Z_R + c0:Z_R + c1] = rinv * r[:, c0:c1]
    for c0, c1 in _col_chunks(Z_R):
        z_ref[:, c0:c1] = rinv * _dot(a, w_ref[:, c0:c1])


def _inproj(h, nw, w, layer, tm):
    L = h.shape[0]
    return pl.pallas_call(
        _inproj_kernel,
        grid=(L // tm,),
        in_specs=[pl.BlockSpec((tm, D_MODEL), lambda i: (i, 0)),
                  pl.BlockSpec((None, 1, D_MODEL), lambda i: (layer, 0, 0)),
                  _layer_block(w, layer)],
        out_specs=pl.BlockSpec((tm, Z_WIDTH), lambda i: (i, 0)),
        out_shape=jax.ShapeDtypeStruct((L, Z_WIDTH), F32),
        scratch_shapes=[pltpu.VMEM((tm, D_MODEL), BF16)],
        compiler_params=_cparams(1),
        name="inproj",
    )(h, nw, w)


def _gelu_tanh(x):
    return 0.5 * x * (1.0 + jnp.tanh(math.sqrt(2.0 / math.pi) * (x + 0.044715 * (x * x * x))))


def _s5_kernel(*refs, tb, pass2):
    if pass2:
        (u_ref, wa_ref, are_ref, aim_ref, ride_ref, xe_re_ref, xe_im_ref, ap_re_ref, ap_im_ref,
         wy_ref, d_ref, wglu_ref, bglu_ref, y_ref, cast_ref, ubuf, xb, st_re, st_im) = refs
    else:
        (u_ref, wa_ref, are_ref, aim_ref, ride_ref, xe_re_ref, xe_im_ref, cast_ref,
         ubuf, st_re, st_im) = refs
    nseg, s = S5_SEGMENTS, S5_STRIDE
    nsub = tb // s
    prow = nsub * nseg
    cast_ref[...] = ride_ref[...].astype(BF16)

    @pl.when(pl.program_id(0) == 0)
    def _init():
        if pass2:
            apr, api = ap_re_ref[...], ap_im_ref[...]
            st_re[0:1, :] = jnp.zeros((1, S5_COLS), F32)
            st_im[0:1, :] = jnp.zeros((1, S5_COLS), F32)
            for j in range(nseg - 1):
                cr, ci = st_re[j:j + 1, :], st_im[j:j + 1, :]
                st_re[j + 1:j + 2, :] = apr * cr - api * ci + xe_re_ref[j:j + 1, :]
                st_im[j + 1:j + 2, :] = apr * ci + api * cr + xe_im_ref[j:j + 1, :]
        else:
            st_re[...] = jnp.zeros((nseg, S5_COLS), F32)
            st_im[...] = jnp.zeros((nseg, S5_COLS), F32)

    for j in range(nseg):
        uj = u_ref[j]
        for b in range(S5_NBUNDLES):
            ubuf[b, pl.ds(j, tb, stride=nseg), :] = uj[:, b * LANES:(b + 1) * LANES]
    ys = []
    for b in range(S5_NBUNDLES):
        v = ubuf[b].reshape(nsub, s * nseg, LANES)
        lhs = jnp.concatenate([v[:, k * nseg:(k + 1) * nseg, :].reshape(prow, LANES) for k in range(s)],
                              axis=1).astype(BF16)
        bu = _dot(lhs, wa_ref[b])
        cols = slice(b * S5_BCOLS, (b + 1) * S5_BCOLS)
        cre = slice(2 * b * S5_BCOLS, (2 * b + 1) * S5_BCOLS)
        cim = slice((2 * b + 1) * S5_BCOLS, (2 * b + 2) * S5_BCOLS)
        ar = jnp.broadcast_to(are_ref[:, cols], (nseg, S5_BCOLS))
        ai = jnp.broadcast_to(aim_ref[:, cols], (nseg, S5_BCOLS))
        xr, xi = st_re[:, cols], st_im[:, cols]
        for m in range(nsub):
            rows = slice(m * nseg, (m + 1) * nseg)
            if pass2:
                xb[rows, cre] = xr
                xb[rows, cim] = xi
            xr, xi = (ar * xr - ai * xi + bu[rows, :S5_BCOLS], ar * xi + ai * xr + bu[rows, S5_BCOLS:])
        st_re[:, cols] = xr
        st_im[:, cols] = xi
        if pass2:
            xp = xb[:, 2 * b * S5_BCOLS:2 * (b + 1) * S5_BCOLS].astype(BF16)
            yp = _dot(jnp.concatenate([lhs, xp], axis=1), wy_ref[b])
            ys.append(jnp.concatenate(
                [yp[:, k * LANES:(k + 1) * LANES].reshape(nsub, nseg, LANES) for k in range(s)],
                axis=1).reshape(tb * nseg, LANES))

    if not pass2:
        xe_re_ref[...] = st_re[...]
        xe_im_ref[...] = st_im[...]
        return

    u = jnp.concatenate([ubuf[b] for b in range(S5_NBUNDLES)], axis=1)
    y = jnp.concatenate(ys, axis=1) + d_ref[...] * u
    y = _gelu_tanh(y)
    glu = _sigmoid(_dot(y.astype(BF16), wglu_ref[...]) + bglu_ref[...])
    out = y * glu
    for b in range(S5_NBUNDLES):
        ubuf[b] = out[:, b * LANES:(b + 1) * LANES]
    for j in range(nseg):
        y_ref[j] = jnp.concatenate(
            [ubuf[b, pl.ds(j, tb, stride=nseg), :] for b in range(S5_NBUNDLES)], axis=1).astype(BF16)


def _s5_mixer(z, prm, ride1, ride2, layer, tb):
    L = z.shape[0]
    lb = lambda name: _layer_block(prm[name], layer)
    nseg = S5_SEGMENTS
    seg_len = L // nseg
    nblk = seg_len // tb
    rows = tb * nseg

    def rider(w):
        slab = w.shape[1] // nblk
        assert slab * nblk == w.shape[1] and slab % (2 * SUBLANES) == 0
        return (pl.BlockSpec((None, slab, w.shape[2]), lambda i: (layer, i, 0)),
                pl.BlockSpec((slab, w.shape[2]), lambda i: (i, 0)),
                jax.ShapeDtypeStruct(w.shape[1:], BF16))

    r1_in, r1_out, r1_shape = rider(ride1)
    r2_in, r2_out, r2_shape = rider(ride2)
    z3 = z.reshape(nseg, seg_len, Z_WIDTH)
    u_spec = pl.BlockSpec((nseg, tb, S5_WIDTH), lambda i: (0, i, Z_S5 // S5_WIDTH))
    state_shape = jax.ShapeDtypeStruct((nseg, S5_COLS), F32)
    ubuf = pltpu.VMEM((S5_NBUNDLES, rows, LANES), F32)
    state = [pltpu.VMEM((nseg, S5_COLS), F32), pltpu.VMEM((nseg, S5_COLS), F32)]
    common = [u_spec, lb("wa"), lb("as_re"), lb("as_im")]

    xe_re, xe_im, cast1 = pl.pallas_call(
        functools.partial(_s5_kernel, tb=tb, pass2=False),
        grid=(nblk,),
        in_specs=common + [r1_in],
        out_specs=[_resident_out((nseg, S5_COLS)), _resident_out((nseg, S5_COLS)), r1_out],
        out_shape=[state_shape, state_shape, r1_shape],
        scratch_shapes=[ubuf] + state,
        compiler_params=_cparams(1),
        name="s5_states",
    )(z3, prm["wa"], prm["as_re"], prm["as_im"], ride1)

    y, cast2 = pl.pallas_call(
        functools.partial(_s5_kernel, tb=tb, pass2=True),
        grid=(nblk,),
        in_specs=common + [r2_in, _resident((nseg, S5_COLS)), _resident((nseg, S5_COLS)),
                           lb("ap_re"), lb("ap_im"), lb("wy"), lb("d"), lb("w_glu"), lb("b_glu")],
        out_specs=[pl.BlockSpec((nseg, tb, S5_WIDTH), lambda i: (0, i, 0)), r2_out],
        out_shape=[jax.ShapeDtypeStruct((nseg, seg_len, S5_WIDTH), BF16), r2_shape],
        scratch_shapes=[ubuf, pltpu.VMEM((rows // S5_STRIDE, 2 * S5_COLS), F32)] + state,
        compiler_params=_cparams(1),
        name="s5_outputs",
    )(z3, prm["wa"], prm["as_re"], prm["as_im"], ride2, xe_re, xe_im, prm["ap_re"], prm["ap_im"],
      prm["wy"], prm["d"], prm["w_glu"], prm["b_glu"])
    return y.reshape(L, S5_WIDTH), cast1, cast2


def _resident_out(shape):
    nd = len(shape)
    return pl.BlockSpec(shape, lambda *_: (0,) * nd)


def _s5_prepare(a_re, a_im, log_dt, b_re, b_im, c_re, c_im, d_skip, w_glu, b_glu, seg_len):
    dt = jnp.exp(log_dt)[:, None]
    mag = jnp.exp(a_re * dt)
    ab_re, ab_im = mag * jnp.cos(a_im * dt), mag * jnp.sin(a_im * dt)
    nr, ni = ab_re - 1.0, ab_im
    den = a_re * a_re + a_im * a_im
    f_re, f_im = (nr * a_re + ni * a_im) / den, (ni * a_re - nr * a_im) / den
    bb_re = f_re[..., None] * b_re - f_im[..., None] * b_im
    bb_im = f_re[..., None] * b_im + f_im[..., None] * b_re
    pr, pi = jnp.ones_like(ab_re), jnp.zeros_like(ab_re)
    sr, si, e = ab_re, ab_im, seg_len
    while e:
        if e & 1:
            pr, pi = pr * sr - pi * si, pr * si + pi * sr
        sr, si = sr * sr - si * si, 2.0 * sr * si
        e >>= 1
    s, nb, gb = S5_STRIDE, S5_NBUNDLES, S5_BUNDLE
    eye = jnp.eye(gb, dtype=F32)
    qr, qi = [jnp.ones_like(ab_re)], [jnp.zeros_like(ab_re)]
    for _ in range(s):
        qr, qi = qr + [qr[-1] * ab_re - qi[-1] * ab_im], qi + [qr[-1] * ab_im + qi[-1] * ab_re]
    p_re, p_im = jnp.stack(qr), jnp.stack(qi)

    d_re = jnp.stack([qr[s - 1 - k] for k in range(s)])[..., None]
    d_im = jnp.stack([qi[s - 1 - k] for k in range(s)])[..., None]

    def b_slab(x):
        x = x.reshape(s, nb, gb, S5_STATE, S5_GROUP).transpose(1, 0, 2, 4, 3)
        x = x[:, :, :, :, None, :] * eye[None, None, :, None, :, None]
        return x.reshape(nb, s * LANES, S5_BCOLS)

    wa = jnp.concatenate([b_slab(d_re * bb_re - d_im * bb_im), b_slab(d_re * bb_im + d_im * bb_re)], axis=-1)

    h_re = c_re * p_re[:, :, None, :] - c_im * p_im[:, :, None, :]
    h_im = c_re * p_im[:, :, None, :] + c_im * p_re[:, :, None, :]

    def c_slab(x):
        x = x.reshape(s, nb, gb, S5_GROUP, S5_STATE).transpose(1, 2, 4, 0, 3)
        x = x[:, :, :, :, None, :] * eye[None, :, None, None, :, None]
        return x.reshape(nb, S5_BCOLS, s * LANES)

    ws = jnp.concatenate([c_slab(h_re[1:]), -c_slab(h_im[1:])], axis=1)
    k_lag = (jnp.einsum("lgon,gnc->lgoc", h_re[:s], bb_re)
             - jnp.einsum("lgon,gnc->lgoc", h_im[:s], bb_im))
    zero = jnp.zeros_like(k_lag[0])
    k_jk = jnp.stack([jnp.stack([k_lag[k - j] if k >= j else zero for k in range(s)])
                      for j in range(s)])
    k_jk = k_jk.reshape(s, s, nb, gb, S5_GROUP, S5_GROUP).transpose(2, 0, 3, 5, 1, 4)
    wl = k_jk[:, :, :, :, :, None, :] * eye[None, None, :, None, None, :, None]
    wl = wl.reshape(nb, s * LANES, s * LANES)
    return {
        "wa": wa.astype(BF16),
        "wy": jnp.concatenate([wl, ws], axis=1).astype(BF16),
        "as_re": p_re[s].reshape(1, S5_COLS), "as_im": p_im[s].reshape(1, S5_COLS),
        "ap_re": pr.reshape(1, S5_COLS), "ap_im": pi.reshape(1, S5_COLS),
        "d": d_skip.reshape(1, S5_WIDTH), "w_glu": w_glu.astype(BF16),
        "b_glu": b_glu.reshape(1, S5_WIDTH),
    }


def _mix_kernel(zp_ref, q_ref, k_ref, v_ref, r_ref, g_ref, pw_ref, ps_ref, wa_ref, ba_ref, gnw_ref,
                tri_ref, ypool_ref, ygla_ref, zext, s_ref, *, tm):
    i = pl.program_id(0)

    @pl.when(i == 0)
    def _init():
        zext[0:POOL_HALO, :] = jnp.zeros((POOL_HALO, POOL_WIDTH), F32)
        s_ref[...] = jnp.zeros(s_ref.shape, F32)

    zext[POOL_HALO:POOL_HALO + tm, :] = zp_ref[...]
    pos = (i * tm + 1 + lax.broadcasted_iota(jnp.int32, (tm, 1), 0)).astype(F32)
    for gi, w in enumerate(POOL_WINDOWS):
        cols = slice(gi * POOL_GROUP, (gi + 1) * POOL_GROUP)
        ze = zext[:, cols]
        s, span = ze, 1
        while span < w:
            s = s + pltpu.roll(s, span, axis=0)
            span *= 2
        zc = ze[POOL_HALO:, :]
        pooled = s[POOL_HALO:, :] / jnp.minimum(pos, float(w)) - zc
        mixed = _dot(pooled.astype(BF16), pw_ref[gi]) * ps_ref[:, cols]
        ypool_ref[:, cols] = mixed.astype(BF16)
    zext[0:POOL_HALO, :] = zext[tm:tm + POOL_HALO, :]

    logit = _dot(g_ref[...].astype(BF16), wa_ref[...]) + ba_ref[...]
    la = (jnp.minimum(logit, 0.0) - jnp.log(1.0 + jnp.exp(-jnp.abs(logit)))) / GLA_TAU
    la_hi = la.astype(BF16)
    la_lo = (la - la_hi.astype(F32)).astype(BF16)
    tri = tri_ref[...]
    b = _dot(tri, la_hi) + _dot(tri, la_lo)
    nch = tm // GLA_CHUNK
    b_last = jnp.concatenate(
        [jnp.broadcast_to(b[(c + 1) * GLA_CHUNK - 1:(c + 1) * GLA_CHUNK, :], (GLA_CHUNK, GLA_K_WIDTH))
         for c in range(nch)], axis=0)
    q_dec = (q_ref[...] * (GLA_DK ** -0.5) * jnp.exp(b)).astype(BF16)
    k = k_ref[...]
    k_dec = (k * jnp.exp(-b)).astype(BF16)
    k_end = (k * jnp.exp(b_last - b)).astype(BF16)
    decay = jnp.exp(b_last)
    causal = (lax.broadcasted_iota(jnp.int32, (GLA_CHUNK, GLA_CHUNK), 0)
              >= lax.broadcasted_iota(jnp.int32, (GLA_CHUNK, GLA_CHUNK), 1))
    gnw = gnw_ref[...]
    units = [(hd, c) for hd in range(GLA_HEADS) for c in range(nch)]
    rows = lambda c: slice(c * GLA_CHUNK, (c + 1) * GLA_CHUNK)
    kcol = lambda hd: slice(hd * GLA_DK, (hd + 1) * GLA_DK)
    vcol = lambda hd: slice(hd * GLA_DV, (hd + 1) * GLA_DV)
    vv = {(hd, c): v_ref[rows(c), vcol(hd)].astype(BF16) for hd, c in units}
    scores = {(hd, c): _dot_nt(q_dec[rows(c), kcol(hd)], k_dec[rows(c), kcol(hd)]) for hd, c in units}
    kv = {(hd, c): _dot_tn(vv[hd, c], k_end[rows(c), kcol(hd)]) for hd, c in units}
    st_in = {}
    for hd in range(GLA_HEADS):
        st = s_ref[hd]
        for c in range(nch):
            st_in[hd, c] = st.astype(BF16)
            st = decay[c * GLA_CHUNK:c * GLA_CHUNK + 1, kcol(hd)] * st + kv[hd, c]
        s_ref[hd] = st
    for hd, c in units:
        sc = jnp.where(causal, scores[hd, c], 0.0).astype(BF16)
        o = _dot(sc, vv[hd, c]) + _dot_nt(q_dec[rows(c), kcol(hd)], st_in[hd, c])
        o = _rms(o, gnw)
        rr = r_ref[rows(c), vcol(hd)]
        ygla_ref[rows(c), vcol(hd)] = (o * (rr * _sigmoid(rr))).astype(BF16)


def _mixers(z, prm, layer, tm):
    L = z.shape[0]
    row = lambda w, col: pl.BlockSpec((tm, w), lambda i: (i, col // w))
    lb = lambda name: _layer_block(prm[name], layer)
    return pl.pallas_call(
        functools.partial(_mix_kernel, tm=tm),
        grid=(L // tm,),
        in_specs=[row(POOL_WIDTH, Z_POOL), row(GLA_K_WIDTH, Z_Q), row(GLA_K_WIDTH, Z_K),
                  row(GLA_V_WIDTH, Z_V), row(GLA_V_WIDTH, Z_R), row(LANES, Z_G),
                  lb("pool_w"), lb("pool_scale"), lb("w_a2"), lb("b_a"), lb("gla_norm_w"),
                  _resident((tm, tm))],
        out_specs=[pl.BlockSpec((tm, POOL_WIDTH), lambda i: (i, 0)),
                   pl.BlockSpec((tm, GLA_V_WIDTH), lambda i: (i, 0))],
        out_shape=[jax.ShapeDtypeStruct((L, POOL_WIDTH), BF16),
                   jax.ShapeDtypeStruct((L, GLA_V_WIDTH), BF16)],
        scratch_shapes=[pltpu.VMEM((tm + POOL_HALO, POOL_WIDTH), F32),
                        pltpu.VMEM((GLA_HEADS, GLA_DV, GLA_DK), F32)],
        compiler_params=_cparams(1),
        name="mixers",
    )(z, z, z, z, z, z, prm["pool_w"], prm["pool_scale"], prm["w_a2"], prm["b_a"], prm["gla_norm_w"],
      prm["tri"])


def _outproj_kernel(h_ref, ys_ref, yp_ref, yg_ref, w_ref, nw_ref, o_ref, a_ref, m_ref):
    m_ref[:, 0:S5_WIDTH] = ys_ref[...]
    m_ref[:, S5_WIDTH:S5_WIDTH + POOL_WIDTH] = yp_ref[...]
    m_ref[:, S5_WIDTH + POOL_WIDTH:] = yg_ref[...]
    m = m_ref[...]
    for c0, c1 in _col_chunks(D_MODEL):
        o_ref[:, c0:c1] = h_ref[:, c0:c1] + _dot(m, w_ref[:, c0:c1])
    a_ref[...] = _rms(o_ref[...], nw_ref[...]).astype(BF16)


def _outproj(h, ys, yp, yg, w, nw, layer, tm):
    L = h.shape[0]
    row = lambda wd: pl.BlockSpec((tm, wd), lambda i: (i, 0))
    return pl.pallas_call(
        _outproj_kernel,
        grid=(L // tm,),
        in_specs=[row(D_MODEL), row(S5_WIDTH), row(POOL_WIDTH), row(GLA_V_WIDTH),
                  pl.BlockSpec((None, D_MODEL, D_MODEL), lambda i: (layer, 0, 0),
                               pipeline_mode=pl.Buffered(1)),
                  pl.BlockSpec((None, 1, D_MODEL), lambda i: (layer, 0, 0))],
        out_specs=[row(D_MODEL), row(D_MODEL)],
        out_shape=[jax.ShapeDtypeStruct((L, D_MODEL), F32), jax.ShapeDtypeStruct((L, D_MODEL), BF16)],
        scratch_shapes=[pltpu.VMEM((tm, D_MODEL), BF16)],
        compiler_params=_cparams(1),
        name="outproj",
    )(h, ys, yp, yg, w, nw)


FFN_FC = 512
FFN_NC = D_FF // FFN_FC
FFN_FN = 512
FFN_NN = D_MODEL // FFN_FN
CONV_HALO = SUBLANES


def _serpentine(i, k, n):
    k = jnp.clip(k, 0, n - 1)
    return jnp.where(i % 2 == 0, k, n - 1 - k)


def _ffn_kernel(a_ref, h_ref, wg_ref, wv_ref, cw_ref, cb_ref, wd_ref, o_ref, act_ref, graw, carry, *, tm):
    i, s = pl.program_id(0), pl.program_id(1)

    @pl.when(s < FFN_NC)
    def _up():
        c = _serpentine(i, s, FFN_NC)
        cols = pl.ds(pl.multiple_of(c * FFN_FC, FFN_FC), FFN_FC)
        a = a_ref[...]
        graw[0:CONV_HALO, :] = jnp.where(i > 0, carry[c], 0.0)
        graw[CONV_HALO:CONV_HALO + tm, :] = _dot(a, wg_ref[...])
        carry[c] = graw[tm:tm + CONV_HALO, :]
        cw = cw_ref[:, cols]
        g = graw[...]
        back = lambda k: pltpu.roll(g, k, axis=0)[CONV_HALO:, :]
        gc = cb_ref[:, cols] + back(2) * cw[0:1, :]
        gc = gc + back(1) * cw[1:2, :]
        gc = gc + g[CONV_HALO:, :] * cw[2:3, :]
        act_ref[:, cols] = (gc * _sigmoid(gc) * _dot(a, wv_ref[...])).astype(BF16)

    @pl.when(s >= FFN_NC)
    def _down():
        o_ref[...] = h_ref[...] + _dot(act_ref[...], wd_ref[...])


def _ffn(a, h, w_up, conv_w, conv_b, w_down, layer, tm):
    L = h.shape[0]
    up_c = lambda i, s: _serpentine(i, s, FFN_NC)
    down_n = lambda i, s: _serpentine(i, s - FFN_NC, FFN_NN)
    return pl.pallas_call(
        functools.partial(_ffn_kernel, tm=tm),
        grid=(L // tm, FFN_NC + FFN_NN),
        in_specs=[pl.BlockSpec((tm, D_MODEL), lambda i, s: (i, 0)),
                  pl.BlockSpec((tm, FFN_FN), lambda i, s: (i, down_n(i, s))),
                  pl.BlockSpec((D_MODEL, FFN_FC), lambda i, s: (0, up_c(i, s))),
                  pl.BlockSpec((D_MODEL, FFN_FC), lambda i, s: (0, FFN_NC + up_c(i, s))),
                  pl.BlockSpec((None, 3, D_FF), lambda i, s: (layer, 0, 0)),
                  pl.BlockSpec((None, 1, D_FF), lambda i, s: (layer, 0, 0)),
                  pl.BlockSpec((D_FF, FFN_FN), lambda i, s: (0, down_n(i, s)))],
        out_specs=pl.BlockSpec((tm, FFN_FN), lambda i, s: (i, down_n(i, s))),
        out_shape=jax.ShapeDtypeStruct((L, D_MODEL), F32),
        scratch_shapes=[pltpu.VMEM((tm, D_FF), BF16),
                        pltpu.VMEM((tm + CONV_HALO, FFN_FC), F32),
                        pltpu.VMEM((FFN_NC, CONV_HALO, FFN_FC), F32)],
        compiler_params=_cparams(2),
        name="ffn",
    )(a, h, w_up, w_up, conv_w, conv_b, w_down)


def _ple_kernel(*refs, final):
    if final:
        h_ref, p_ref, nw_ref, wpg_ref, wple_ref, fw_ref, o_ref, a_ref = refs
    else:
        h_ref, p_ref, nw_ref, wpg_ref, wple_ref, o_ref, a_ref = refs
    rinv = _rms_split(h_ref, nw_ref, a_ref, MM_TN)
    a = a_ref[...]
    pe = p_ref[...].astype(BF16)
    for c0, c1 in _col_chunks(D_MODEL):
        gate = _sigmoid(rinv * _dot(a, wpg_ref[:, c0:c1]))
        o_ref[:, c0:c1] = h_ref[:, c0:c1] + _dot(pe, wple_ref[:, c0:c1]) * gate
    if final:
        o_ref[...] = _rms(o_ref[...], fw_ref[...])


def _ple(h, p, nw, w_pg, w_ple, final_w, layer, tm):
    L = h.shape[0]
    final = final_w is not None
    in_specs = [pl.BlockSpec((tm, D_MODEL), lambda i: (i, 0)),
                pl.BlockSpec((None, tm, PLE_DIM), lambda i: (layer, i, 0)),
                pl.BlockSpec((None, 1, D_MODEL), lambda i: (layer, 0, 0)),
                pl.BlockSpec((None, D_MODEL, D_MODEL), lambda i: (layer, 0, 0),
                             pipeline_mode=pl.Buffered(1)),
                pl.BlockSpec((None, PLE_DIM, D_MODEL), lambda i: (layer, 0, 0),
                             pipeline_mode=pl.Buffered(1))]
    args = [h, p, nw, w_pg, w_ple]
    if final:
        in_specs.append(_resident((1, D_MODEL)))
        args.append(final_w)
    return pl.pallas_call(
        functools.partial(_ple_kernel, final=final),
        grid=(L // tm,),
        in_specs=in_specs,
        out_specs=pl.BlockSpec((tm, D_MODEL), lambda i: (i, 0)),
        out_shape=jax.ShapeDtypeStruct((L, D_MODEL), F32),
        scratch_shapes=[pltpu.VMEM((tm, D_MODEL), BF16)],
        compiler_params=_cparams(1),
        name="ple_final" if final else "ple",
    )(*args)


def _chunk_tril(tm):
    r = jnp.arange(tm)
    same = (r[:, None] // GLA_CHUNK) == (r[None, :] // GLA_CHUNK)
    return (same & (r[:, None] >= r[None, :])).astype(BF16)


def kernel(x, p, norm_mix_w, w_in, s5_a_re, s5_a_im, s5_log_dt, s5_b_re, s5_b_im, s5_c_re, s5_c_im, s5_d, s5_w_glu, s5_b_glu, pool_w, pool_scale, gla_w_a2, gla_b_a, gla_norm_w, w_out, norm_ffn_w, w_up, conv_w, conv_b, w_down, norm_ple_w, w_ple, w_pg, final_norm_w):
    bsz, L, _ = x.shape
    tb_s5 = min(S5_TB, L // S5_SEGMENTS)
    assert bsz == 1 and L % (S5_SEGMENTS * tb_s5) == 0 and tb_s5 % S5_STRIDE == 0
    depth = w_in.shape[0]
    tm = min(512, L)
    tm_ffn = min(1024, L)
    tm_mix = min(512, L)
    tri = _chunk_tril(tm_mix)
    nw_ffn = norm_ffn_w.reshape(depth, 1, D_MODEL)
    conv_b3 = conv_b.reshape(depth, 1, D_FF)
    w_in_b = jnp.concatenate(
        [w_in.astype(BF16), jnp.zeros((depth, D_MODEL, Z_WIDTH - w_in.shape[-1]), BF16)], axis=-1)
    w_out_b, w_pg_b, w_ple_b = w_out.astype(BF16), w_pg.astype(BF16), w_ple.astype(BF16)
    nw_mix = norm_mix_w.reshape(depth, 1, D_MODEL)
    nw_ple = norm_ple_w.reshape(depth, 1, D_MODEL)
    p3 = p.reshape(depth, L, PLE_DIM)
    s5_prm = jax.vmap(functools.partial(_s5_prepare, seg_len=L // S5_SEGMENTS))(
        s5_a_re, s5_a_im, s5_log_dt, s5_b_re, s5_b_im, s5_c_re, s5_c_im, s5_d, s5_w_glu, s5_b_glu)
    mix_prm = {
        "pool_w": pool_w.astype(BF16), "pool_scale": pool_scale.reshape(depth, 1, POOL_WIDTH),
        "w_a2": jnp.pad(gla_w_a2, ((0, 0), (0, LANES - GLA_GATE_RANK), (0, 0))).astype(BF16),
        "b_a": gla_b_a.reshape(depth, 1, GLA_K_WIDTH), "gla_norm_w": gla_norm_w.reshape(depth, 1, GLA_DV),
        "tri": tri,
    }
    h = x.reshape(L, D_MODEL)
    for i in range(depth):
        z = _inproj(h, nw_mix, w_in_b, i, tm)
        y_s5, w_down_b, w_up_b = _s5_mixer(z, s5_prm, w_down, w_up, i, tb_s5)
        y_pool, y_gla = _mixers(z, mix_prm, i, tm_mix)
        h, a = _outproj(h, y_s5, y_pool, y_gla, w_out_b, nw_ffn, i, tm)
        h = _ffn(a, h, w_up_b, conv_w, conv_b3, w_down_b, i, tm_ffn)
        final_w = final_norm_w.reshape(1, D_MODEL) if i == depth - 1 else None
        h = _ple(h, p3, nw_ple, w_pg_b, w_ple_b, final_w, i, tm)
    return h.reshape(bsz, L, D_MODEL)
```

```python
import functools
import math

import jax
import jax.numpy as jnp
from jax import lax
from jax.experimental import pallas as pl
from jax.experimental.pallas import tpu as pltpu

F32 = jnp.float32
BF16 = jnp.bfloat16

D_MODEL = 2048
S5_WIDTH = 512
S5_GROUP = 16
S5_GROUPS = 32
S5_STATE = 64
S5_COLS = S5_GROUPS * S5_STATE
POOL_WIDTH = 512
POOL_WINDOWS = (2, 4, 8, 16)
POOL_GROUP = 128
GLA_HEADS = 4
GLA_DK = 128
GLA_DV = 256
GLA_K_WIDTH = 512
GLA_V_WIDTH = 1024
GLA_GATE_RANK = 16
GLA_TAU = 16.0
GLA_CHUNK = 64
D_FF = 5632
PLE_DIM = 256
EPS = 1e-6

LANES = 128
SUBLANES = 8
S5_SEGMENTS = SUBLANES
S5_BUNDLE = LANES // S5_GROUP
S5_NBUNDLES = S5_GROUPS // S5_BUNDLE
S5_BCOLS = S5_BUNDLE * S5_STATE
S5_TB = 128
S5_STRIDE = 4
POOL_HALO = 16

Z_S5, Z_POOL, Z_Q, Z_K, Z_V, Z_R, Z_G = 0, 512, 1024, 1536, 2048, 3072, 4096
Z_WIDTH = 4224
MM_TN = 512

VMEM_LIMIT = 56 * 1024 * 1024


def _cparams(n_axes):
    return pltpu.CompilerParams(dimension_semantics=("arbitrary",) * n_axes,
                                vmem_limit_bytes=VMEM_LIMIT)


def _resident(shape):
    nd = len(shape)
    return pl.BlockSpec(shape, lambda *_: (0,) * nd, pipeline_mode=pl.Buffered(1))


def _layer_block(arr, layer):
    nd = arr.ndim
    return pl.BlockSpec((None,) + arr.shape[1:], lambda *_: (layer,) + (0,) * (nd - 1),
                        pipeline_mode=pl.Buffered(1))


def _rms(x, w):
    ms = jnp.mean(x * x, axis=-1, keepdims=True)
    return x * lax.rsqrt(ms + EPS) * w


def _sigmoid(x):
    return 0.5 * (1.0 + jnp.tanh(0.5 * x))


def _dot(a, b):
    return jnp.dot(a, b, preferred_element_type=F32)


def _dot_nt(a, b):
    return lax.dot_general(a, b, (((1,), (1,)), ((), ())), preferred_element_type=F32)


def _dot_tn(a, b):
    return lax.dot_general(a, b, (((0,), (0,)), ((), ())), preferred_element_type=F32)


def _col_chunks(n):
    return [(c0, min(c0 + MM_TN, n)) for c0 in range(0, n, MM_TN)]


MXU_K = 256


def _rms_split(h_ref, w_ref, a_ref, width):
    sq = None
    for k0 in range(0, h_ref.shape[1], MXU_K):
        hk = h_ref[:, k0:k0 + MXU_K]
        a_ref[:, k0:k0 + MXU_K] = (hk * w_ref[:, k0:k0 + MXU_K]).astype(BF16)
        sq = hk * hk if sq is None else sq + hk * hk
    rinv = lax.rsqrt(jnp.sum(sq, axis=-1, keepdims=True) * (1.0 / h_ref.shape[1]) + EPS)
    return jnp.broadcast_to(rinv, (h_ref.shape[0], width))


def _inproj_kernel(h_ref, nw_ref, w_ref, z_ref, a_ref):
    rinv = _rms_split(h_ref, nw_ref, a_ref, MM_TN)
    a = a_ref[...]
    zt = jnp.concatenate([_dot(a, w_ref[:, Z_R + c0:Z_R + c1]) for c0, c1 in _col_chunks(Z_WIDTH - Z_R)],
                         axis=1)
    z_ref[:, Z_G:] = rinv[:, :LANES] * zt[:, :LANES]
    r = zt[:, GLA_GATE_RANK:GLA_GATE_RANK + GLA_V_WIDTH]
    for c0, c1 in _col_chunks(GLA_V_WIDTH):
        z_ref[:, Z_R + c0:Z_R + c1] = rinv * r[:, c0:c1]
    for c0, c1 in _col_chunks(Z_R):
        z_ref[:, c0:c1] = rinv * _dot(a, w_ref[:, c0:c1])


def _inproj(h, nw, w, layer, tm):
    L = h.shape[0]
    return pl.pallas_call(
        _inproj_kernel,
        grid=(L // tm,),
        in_specs=[pl.BlockSpec((tm, D_MODEL), lambda i: (i, 0)),
                  pl.BlockSpec((None, 1, D_MODEL), lambda i: (layer, 0, 0)),
                  _layer_block(w, layer)],
        out_specs=pl.BlockSpec((tm, Z_WIDTH), lambda i: (i, 0)),
        out_shape=jax.ShapeDtypeStruct((L, Z_WIDTH), F32),
        scratch_shapes=[pltpu.VMEM((tm, D_MODEL), BF16)],
        compiler_params=_cparams(1),
        name="inproj",
    )(h, nw, w)


def _gelu_tanh(x):
    return 0.5 * x * (1.0 + jnp.tanh(math.sqrt(2.0 / math.pi) * (x + 0.044715 * (x * x * x))))


def _s5_kernel(*refs, tb, pass2):
    if pass2:
        (u_ref, wa_ref, are_ref, aim_ref, ride_ref, xe_re_ref, xe_im_ref, ap_re_ref, ap_im_ref,
         wy_ref, d_ref, wglu_ref, bglu_ref, y_ref, cast_ref, ubuf, xb, st_re, st_im) = refs
    else:
        (u_ref, wa_ref, are_ref, aim_ref, ride_ref, xe_re_ref, xe_im_ref, cast_ref,
         ubuf, st_re, st_im) = refs
    nseg, s = S5_SEGMENTS, S5_STRIDE
    nsub = tb // s
    prow = nsub * nseg
    cast_ref[...] = ride_ref[...].astype(BF16)

    @pl.when(pl.program_id(0) == 0)
    def _init():
        if pass2:
            apr, api = ap_re_ref[...], ap_im_ref[...]
            st_re[0:1, :] = jnp.zeros((1, S5_COLS), F32)
            st_im[0:1, :] = jnp.zeros((1, S5_COLS), F32)
            for j in range(nseg - 1):
                cr, ci = st_re[j:j + 1, :], st_im[j:j + 1, :]
                st_re[j + 1:j + 2, :] = apr * cr - api * ci + xe_re_ref[j:j + 1, :]
                st_im[j + 1:j + 2, :] = apr * ci + api * cr + xe_im_ref[j:j + 1, :]
        else:
            st_re[...] = jnp.zeros((nseg, S5_COLS), F32)
            st_im[...] = jnp.zeros((nseg, S5_COLS), F32)

    for j in range(nseg):
        uj = u_ref[j]
        for b in range(S5_NBUNDLES):
            ubuf[b, pl.ds(j, tb, stride=nseg), :] = uj[:, b * LANES:(b + 1) * LANES]
    ys = []
    for b in range(S5_NBUNDLES):
        v = ubuf[b].reshape(nsub, s * nseg, LANES)
        lhs = jnp.concatenate([v[:, k * nseg:(k + 1) * nseg, :].reshape(prow, LANES) for k in range(s)],
                              axis=1).astype(BF16)
        bu = _dot(lhs, wa_ref[b])
        cols = slice(b * S5_BCOLS, (b + 1) * S5_BCOLS)
        cre = slice(2 * b * S5_BCOLS, (2 * b + 1) * S5_BCOLS)
        cim = slice((2 * b + 1) * S5_BCOLS, (2 * b + 2) * S5_BCOLS)
        ar = jnp.broadcast_to(are_ref[:, cols], (nseg, S5_BCOLS))
        ai = jnp.broadcast_to(aim_ref[:, cols], (nseg, S5_BCOLS))
        xr, xi = st_re[:, cols], st_im[:, cols]
        for m in range(nsub):
            rows = slice(m * nseg, (m + 1) * nseg)
            if pass2:
                xb[rows, cre] = xr
                xb[rows, cim] = xi
            xr, xi = (ar * xr - ai * xi + bu[rows, :S5_BCOLS], ar * xi + ai * xr + bu[rows, S5_BCOLS:])
        st_re[:, cols] = xr
        st_im[:, cols] = xi
        if pass2:
            xp = xb[:, 2 * b * S5_BCOLS:2 * (b + 1) * S5_BCOLS].astype(BF16)
            yp = _dot(jnp.concatenate([lhs, xp], axis=1), wy_ref[b])
            ys.append(jnp.concatenate(
                [yp[:, k * LANES:(k + 1) * LANES].reshape(nsub, nseg, LANES) for k in range(s)],
                axis=1).reshape(tb * nseg, LANES))

    if not pass2:
        xe_re_ref[...] = st_re[...]
        xe_im_ref[...] = st_im[...]
        return

    u = jnp.concatenate([ubuf[b] for b in range(S5_NBUNDLES)], axis=1)
    y = jnp.concatenate(ys, axis=1) + d_ref[...] * u
    y = _gelu_tanh(y)
    glu = _sigmoid(_dot(y.astype(BF16), wglu_ref[...]) + bglu_ref[...])
    out = y * glu
    for b in range(S5_NBUNDLES):
        ubuf[b] = out[:, b * LANES:(b + 1) * LANES]
    for j in range(nseg):
        y_ref[j] = jnp.concatenate(
            [ubuf[b, pl.ds(j, tb, stride=nseg), :] for b in range(S5_NBUNDLES)], axis=1).astype(BF16)


def _s5_mixer(z, prm, ride1, ride2, layer, tb):
    L = z.shape[0]
    lb = lambda name: _layer_block(prm[name], layer)
    nseg = S5_SEGMENTS
    seg_len = L // nseg
    nblk = seg_len // tb
    rows = tb * nseg

    def rider(w):
        slab = w.shape[1] // nblk
        assert slab * nblk == w.shape[1] and slab % (2 * SUBLANES) == 0
        return (pl.BlockSpec((None, slab, w.shape[2]), lambda i: (layer, i, 0)),
                pl.BlockSpec((slab, w.shape[2]), lambda i: (i, 0)),
                jax.ShapeDtypeStruct(w.shape[1:], BF16))

    r1_in, r1_out, r1_shape = rider(ride1)
    r2_in, r2_out, r2_shape = rider(ride2)
    z3 = z.reshape(nseg, seg_len, Z_WIDTH)
    u_spec = pl.BlockSpec((nseg, tb, S5_WIDTH), lambda i: (0, i, Z_S5 // S5_WIDTH))
    state_shape = jax.ShapeDtypeStruct((nseg, S5_COLS), F32)
    ubuf = pltpu.VMEM((S5_NBUNDLES, rows, LANES), F32)
    state = [pltpu.VMEM((nseg, S5_COLS), F32), pltpu.VMEM((nseg, S5_COLS), F32)]
    common = [u_spec, lb("wa"), lb("as_re"), lb("as_im")]

    xe_re, xe_im, cast1 = pl.pallas_call(
        functools.partial(_s5_kernel, tb=tb, pass2=False),
        grid=(nblk,),
        in_specs=common + [r1_in],
        out_specs=[_resident_out((nseg, S5_COLS)), _resident_out((nseg, S5_COLS)), r1_out],
        out_shape=[state_shape, state_shape, r1_shape],
        scratch_shapes=[ubuf] + state,
        compiler_params=_cparams(1),
        name="s5_states",
    )(z3, prm["wa"], prm["as_re"], prm["as_im"], ride1)

    y, cast2 = pl.pallas_call(
        functools.partial(_s5_kernel, tb=tb, pass2=True),
        grid=(nblk,),
        in_specs=common + [r2_in, _resident((nseg, S5_COLS)), _resident((nseg, S5_COLS)),
                           lb("ap_re"), lb("ap_im"), lb("wy"), lb("d"), lb("w_glu"), lb("b_glu")],
        out_specs=[pl.BlockSpec((nseg, tb, S5_WIDTH), lambda i: (0, i, 0)), r2_out],
        out_shape=[jax.ShapeDtypeStruct((nseg, seg_len, S5_WIDTH), BF16), r2_shape],
        scratch_shapes=[ubuf, pltpu.VMEM((rows // S5_STRIDE, 2 * S5_COLS), F32)] + state,
        compiler_params=_cparams(1),
        name="s5_outputs",
    )(z3, prm["wa"], prm["as_re"], prm["as_im"], ride2, xe_re, xe_im, prm["ap_re"], prm["ap_im"],
      prm["wy"], prm["d"], prm["w_glu"], prm["b_glu"])
    return y.reshape(L, S5_WIDTH), cast1, cast2


def _resident_out(shape):
    nd = len(shape)
    return pl.BlockSpec(shape, lambda *_: (0,) * nd)


def _s5_prepare(a_re, a_im, log_dt, b_re, b_im, c_re, c_im, d_skip, w_glu, b_glu, seg_len):
    dt = jnp.exp(log_dt)[:, None]
    mag = jnp.exp(a_re * dt)
    ab_re, ab_im = mag * jnp.cos(a_im * dt), mag * jnp.sin(a_im * dt)
    nr, ni = ab_re - 1.0, ab_im
    den = a_re * a_re + a_im * a_im
    f_re, f_im = (nr * a_re + ni * a_im) / den, (ni * a_re - nr * a_im) / den
    bb_re = f_re[..., None] * b_re - f_im[..., None] * b_im
    bb_im = f_re[..., None] * b_im + f_im[..., None] * b_re
    pr, pi = jnp.ones_like(ab_re), jnp.zeros_like(ab_re)
    sr, si, e = ab_re, ab_im, seg_len
    while e:
        if e & 1:
            pr, pi = pr * sr - pi * si, pr * si + pi * sr
        sr, si = sr * sr - si * si, 2.0 * sr * si
        e >>= 1
    s, nb, gb = S5_STRIDE, S5_NBUNDLES, S5_BUNDLE
    qr, qi = [jnp.ones_like(ab_re)], [jnp.zeros_like(ab_re)]
    for _ in range(s):
        qr, qi = qr + [qr[-1] * ab_re - qi[-1] * ab_im], qi + [qr[-1] * ab_im + qi[-1] * ab_re]
    p_re, p_im = jnp.stack(qr), jnp.stack(qi)

    d_re = jnp.stack([qr[s - 1 - k] for k in range(s)])[..., None]
    d_im = jnp.stack([qi[s - 1 - k] for k in range(s)])[..., None]

    def spread(compact, rows_group, cols_per_group, col_blocks):
        width = col_blocks * gb * cols_per_group
        col = jnp.arange(width)
        src = (col // (gb * cols_per_group)) * cols_per_group + col % cols_per_group
        tile = (jnp.arange(col_blocks * cols_per_group)[:, None] == src[None, :]).astype(F32)
        mask = (rows_group[:, None] == ((col // cols_per_group) % gb)[None, :]).astype(F32)
        return jnp.einsum("brm,mc->brc", compact, tile) * mask

    in_rows = (jnp.arange(s * LANES) // S5_GROUP) % gb
    st_rows = (jnp.arange(2 * S5_BCOLS) // S5_STATE) % gb

    sb = jnp.stack([d_re * bb_re - d_im * bb_im, d_re * bb_im + d_im * bb_re])
    wa_c = sb.reshape(2, s, nb, gb, S5_STATE, S5_GROUP).transpose(2, 1, 3, 5, 0, 4).reshape(
        nb, s * LANES, 2 * S5_STATE)
    wa = spread(wa_c, in_rows, S5_STATE, 2)

    h_re = c_re * p_re[:, :, None, :] - c_im * p_im[:, :, None, :]
    h_im = c_re * p_im[:, :, None, :] + c_im * p_re[:, :, None, :]
    hs = jnp.stack([h_re[1:], -h_im[1:]])
    ws_c = hs.reshape(2, s, nb, gb, S5_GROUP, S5_STATE).transpose(2, 0, 3, 5, 1, 4).reshape(
        nb, 2 * S5_BCOLS, s * S5_GROUP)
    ws = spread(ws_c, st_rows, S5_GROUP, s)
    k_lag = (jnp.einsum("lgon,gnc->lgoc", h_re[:s], bb_re)
             - jnp.einsum("lgon,gnc->lgoc", h_im[:s], bb_im))
    zero = jnp.zeros_like(k_lag[0])
    k_jk = jnp.stack([jnp.stack([k_lag[k - j] if k >= j else zero for k in range(s)])
                      for j in range(s)])
    wl_c = k_jk.reshape(s, s, nb, gb, S5_GROUP, S5_GROUP).transpose(2, 0, 3, 5, 1, 4).reshape(
        nb, s * LANES, s * S5_GROUP)
    wl = spread(wl_c, in_rows, S5_GROUP, s)
    return {
        "wa": wa.astype(BF16),
        "wy": jnp.concatenate([wl, ws], axis=1).astype(BF16),
        "as_re": p_re[s].reshape(1, S5_COLS), "as_im": p_im[s].reshape(1, S5_COLS),
        "ap_re": pr.reshape(1, S5_COLS), "ap_im": pi.reshape(1, S5_COLS),
        "d": d_skip.reshape(1, S5_WIDTH), "w_glu": w_glu.astype(BF16),
        "b_glu": b_glu.reshape(1, S5_WIDTH),
    }


def _mix_kernel(zp_ref, q_ref, k_ref, v_ref, r_ref, g_ref, pw_ref, ps_ref, wa_ref, ba_ref, gnw_ref,
                tri_ref, ypool_ref, ygla_ref, zext, s_ref, *, tm):
    i = pl.program_id(0)

    @pl.when(i == 0)
    def _init():
        zext[0:POOL_HALO, :] = jnp.zeros((POOL_HALO, POOL_WIDTH), F32)
        s_ref[...] = jnp.zeros(s_ref.shape, F32)

    zext[POOL_HALO:POOL_HALO + tm, :] = zp_ref[...]
    pos = (i * tm + 1 + lax.broadcasted_iota(jnp.int32, (tm, 1), 0)).astype(F32)
    for gi, w in enumerate(POOL_WINDOWS):
        cols = slice(gi * POOL_GROUP, (gi + 1) * POOL_GROUP)
        ze = zext[:, cols]
        s, span = ze, 1
        while span < w:
            s = s + pltpu.roll(s, span, axis=0)
            span *= 2
        zc = ze[POOL_HALO:, :]
        pooled = s[POOL_HALO:, :] / jnp.minimum(pos, float(w)) - zc
        mixed = _dot(pooled.astype(BF16), pw_ref[gi]) * ps_ref[:, cols]
        ypool_ref[:, cols] = mixed.astype(BF16)
    zext[0:POOL_HALO, :] = zext[tm:tm + POOL_HALO, :]

    logit = _dot(g_ref[...].astype(BF16), wa_ref[...]) + ba_ref[...]
    la = (jnp.minimum(logit, 0.0) - jnp.log(1.0 + jnp.exp(-jnp.abs(logit)))) / GLA_TAU
    la_hi = la.astype(BF16)
    la_lo = (la - la_hi.astype(F32)).astype(BF16)
    tri = tri_ref[...]
    b = _dot(tri, la_hi) + _dot(tri, la_lo)
    nch = tm // GLA_CHUNK
    b_last = jnp.concatenate(
        [jnp.broadcast_to(b[(c + 1) * GLA_CHUNK - 1:(c + 1) * GLA_CHUNK, :], (GLA_CHUNK, GLA_K_WIDTH))
         for c in range(nch)], axis=0)
    q_dec = (q_ref[...] * (GLA_DK ** -0.5) * jnp.exp(b)).astype(BF16)
    k = k_ref[...]
    k_dec = (k * jnp.exp(-b)).astype(BF16)
    k_end = (k * jnp.exp(b_last - b)).astype(BF16)
    decay = jnp.exp(b_last)
    causal = (lax.broadcasted_iota(jnp.int32, (GLA_CHUNK, GLA_CHUNK), 0)
              >= lax.broadcasted_iota(jnp.int32, (GLA_CHUNK, GLA_CHUNK), 1))
    gnw = gnw_ref[...]
    units = [(hd, c) for hd in range(GLA_HEADS) for c in range(nch)]
    rows = lambda c: slice(c * GLA_CHUNK, (c + 1) * GLA_CHUNK)
    kcol = lambda hd: slice(hd * GLA_DK, (hd + 1) * GLA_DK)
    vcol = lambda hd: slice(hd * GLA_DV, (hd + 1) * GLA_DV)
    vv = {(hd, c): v_ref[rows(c), vcol(hd)].astype(BF16) for hd, c in units}
    scores = {(hd, c): _dot_nt(q_dec[rows(c), kcol(hd)], k_dec[rows(c), kcol(hd)]) for hd, c in units}
    kv = {(hd, c): _dot_tn(vv[hd, c], k_end[rows(c), kcol(hd)]) for hd, c in units}
    st_in = {}
    for hd in range(GLA_HEADS):
        st = s_ref[hd]
        for c in range(nch):
            st_in[hd, c] = st.astype(BF16)
            st = decay[c * GLA_CHUNK:c * GLA_CHUNK + 1, kcol(hd)] * st + kv[hd, c]
        s_ref[hd] = st
    for hd, c in units:
        sc = jnp.where(causal, scores[hd, c], 0.0).astype(BF16)
        o = _dot(sc, vv[hd, c]) + _dot_nt(q_dec[rows(c), kcol(hd)], st_in[hd, c])
        o = _rms(o, gnw)
        rr = r_ref[rows(c), vcol(hd)]
        ygla_ref[rows(c), vcol(hd)] = (o * (rr * _sigmoid(rr))).astype(BF16)


def _mixers(z, prm, layer, tm):
    L = z.shape[0]
    row = lambda w, col: pl.BlockSpec((tm, w), lambda i: (i, col // w))
    lb = lambda name: _layer_block(prm[name], layer)
    return pl.pallas_call(
        functools.partial(_mix_kernel, tm=tm),
        grid=(L // tm,),
        in_specs=[row(POOL_WIDTH, Z_POOL), row(GLA_K_WIDTH, Z_Q), row(GLA_K_WIDTH, Z_K),
                  row(GLA_V_WIDTH, Z_V), row(GLA_V_WIDTH, Z_R), row(LANES, Z_G),
                  lb("pool_w"), lb("pool_scale"), lb("w_a2"), lb("b_a"), lb("gla_norm_w"),
                  _resident((tm, tm))],
        out_specs=[pl.BlockSpec((tm, POOL_WIDTH), lambda i: (i, 0)),
                   pl.BlockSpec((tm, GLA_V_WIDTH), lambda i: (i, 0))],
        out_shape=[jax.ShapeDtypeStruct((L, POOL_WIDTH), BF16),
                   jax.ShapeDtypeStruct((L, GLA_V_WIDTH), BF16)],
        scratch_shapes=[pltpu.VMEM((tm + POOL_HALO, POOL_WIDTH), F32),
                        pltpu.VMEM((GLA_HEADS, GLA_DV, GLA_DK), F32)],
        compiler_params=_cparams(1),
        name="mixers",
    )(z, z, z, z, z, z, prm["pool_w"], prm["pool_scale"], prm["w_a2"], prm["b_a"], prm["gla_norm_w"],
      prm["tri"])


def _outproj_kernel(h_ref, ys_ref, yp_ref, yg_ref, w_ref, nw_ref, o_ref, a_ref, m_ref):
    m_ref[:, 0:S5_WIDTH] = ys_ref[...]
    m_ref[:, S5_WIDTH:S5_WIDTH + POOL_WIDTH] = yp_ref[...]
    m_ref[:, S5_WIDTH + POOL_WIDTH:] = yg_ref[...]
    m = m_ref[...]
    for c0, c1 in _col_chunks(D_MODEL):
        o_ref[:, c0:c1] = h_ref[:, c0:c1] + _dot(m, w_ref[:, c0:c1])
    a_ref[...] = _rms(o_ref[...], nw_ref[...]).astype(BF16)


def _outproj(h, ys, yp, yg, w, nw, layer, tm):
    L = h.shape[0]
    row = lambda wd: pl.BlockSpec((tm, wd), lambda i: (i, 0))
    return pl.pallas_call(
        _outproj_kernel,
        grid=(L // tm,),
        in_specs=[row(D_MODEL), row(S5_WIDTH), row(POOL_WIDTH), row(GLA_V_WIDTH),
                  pl.BlockSpec((None, D_MODEL, D_MODEL), lambda i: (layer, 0, 0),
                               pipeline_mode=pl.Buffered(1)),
                  pl.BlockSpec((None, 1, D_MODEL), lambda i: (layer, 0, 0))],
        out_specs=[row(D_MODEL), row(D_MODEL)],
        out_shape=[jax.ShapeDtypeStruct((L, D_MODEL), F32), jax.ShapeDtypeStruct((L, D_MODEL), BF16)],
        scratch_shapes=[pltpu.VMEM((tm, D_MODEL), BF16)],
        compiler_params=_cparams(1),
        name="outproj",
    )(h, ys, yp, yg, w, nw)


FFN_FC = 512
FFN_NC = D_FF // FFN_FC
FFN_FN = 512
FFN_NN = D_MODEL // FFN_FN
CONV_HALO = SUBLANES


def _serpentine(i, k, n):
    k = jnp.clip(k, 0, n - 1)
    return jnp.where(i % 2 == 0, k, n - 1 - k)


def _ffn_kernel(a_ref, h_ref, wg_ref, wv_ref, cw_ref, cb_ref, wd_ref, o_ref, act_ref, graw, carry, *, tm):
    i, s = pl.program_id(0), pl.program_id(1)

    @pl.when(s < FFN_NC)
    def _up():
        c = _serpentine(i, s, FFN_NC)
        cols = pl.ds(pl.multiple_of(c * FFN_FC, FFN_FC), FFN_FC)
        a = a_ref[...]
        graw[0:CONV_HALO, :] = jnp.where(i > 0, carry[c], 0.0)
        graw[CONV_HALO:CONV_HALO + tm, :] = _dot(a, wg_ref[...])
        carry[c] = graw[tm:tm + CONV_HALO, :]
        cw = cw_ref[:, cols]
        g = graw[...]
        back = lambda k: pltpu.roll(g, k, axis=0)[CONV_HALO:, :]
        gc = cb_ref[:, cols] + back(2) * cw[0:1, :]
        gc = gc + back(1) * cw[1:2, :]
        gc = gc + g[CONV_HALO:, :] * cw[2:3, :]
        act_ref[:, cols] = (gc * _sigmoid(gc) * _dot(a, wv_ref[...])).astype(BF16)

    @pl.when(s >= FFN_NC)
    def _down():
        o_ref[...] = h_ref[...] + _dot(act_ref[...], wd_ref[...])


def _ffn(a, h, w_up, conv_w, conv_b, w_down, layer, tm):
    L = h.shape[0]
    up_c = lambda i, s: _serpentine(i, s, FFN_NC)
    down_n = lambda i, s: _serpentine(i, s - FFN_NC, FFN_NN)
    return pl.pallas_call(
        functools.partial(_ffn_kernel, tm=tm),
        grid=(L // tm, FFN_NC + FFN_NN),
        in_specs=[pl.BlockSpec((tm, D_MODEL), lambda i, s: (i, 0)),
                  pl.BlockSpec((tm, FFN_FN), lambda i, s: (i, down_n(i, s))),
                  pl.BlockSpec((D_MODEL, FFN_FC), lambda i, s: (0, up_c(i, s))),
                  pl.BlockSpec((D_MODEL, FFN_FC), lambda i, s: (0, FFN_NC + up_c(i, s))),
                  pl.BlockSpec((None, 3, D_FF), lambda i, s: (layer, 0, 0)),
                  pl.BlockSpec((None, 1, D_FF), lambda i, s: (layer, 0, 0)),
                  pl.BlockSpec((D_FF, FFN_FN), lambda i, s: (0, down_n(i, s)))],
        out_specs=pl.BlockSpec((tm, FFN_FN), lambda i, s: (i, down_n(i, s))),
        out_shape=jax.ShapeDtypeStruct((L, D_MODEL), F32),
        scratch_shapes=[pltpu.VMEM((tm, D_FF), BF16),
                        pltpu.VMEM((tm + CONV_HALO, FFN_FC), F32),
                        pltpu.VMEM((FFN_NC, CONV_HALO, FFN_FC), F32)],
        compiler_params=_cparams(2),
        name="ffn",
    )(a, h, w_up, w_up, conv_w, conv_b, w_down)


def _ple_kernel(*refs, final):
    if final:
        h_ref, p_ref, nw_ref, wpg_ref, wple_ref, fw_ref, o_ref, a_ref = refs
    else:
        h_ref, p_ref, nw_ref, wpg_ref, wple_ref, o_ref, a_ref = refs
    rinv = _rms_split(h_ref, nw_ref, a_ref, MM_TN)
    a = a_ref[...]
    pe = p_ref[...].astype(BF16)
    for c0, c1 in _col_chunks(D_MODEL):
        gate = _sigmoid(rinv * _dot(a, wpg_ref[:, c0:c1]))
        o_ref[:, c0:c1] = h_ref[:, c0:c1] + _dot(pe, wple_ref[:, c0:c1]) * gate
    if final:
        o_ref[...] = _rms(o_ref[...], fw_ref[...])


def _ple(h, p, nw, w_pg, w_ple, final_w, layer, tm):
    L = h.shape[0]
    final = final_w is not None
    in_specs = [pl.BlockSpec((tm, D_MODEL), lambda i: (i, 0)),
                pl.BlockSpec((None, tm, PLE_DIM), lambda i: (layer, i, 0)),
                pl.BlockSpec((None, 1, D_MODEL), lambda i: (layer, 0, 0)),
                pl.BlockSpec((None, D_MODEL, D_MODEL), lambda i: (layer, 0, 0),
                             pipeline_mode=pl.Buffered(1)),
                pl.BlockSpec((None, PLE_DIM, D_MODEL), lambda i: (layer, 0, 0),
                             pipeline_mode=pl.Buffered(1))]
    args = [h, p, nw, w_pg, w_ple]
    if final:
        in_specs.append(_resident((1, D_MODEL)))
        args.append(final_w)
    return pl.pallas_call(
        functools.partial(_ple_kernel, final=final),
        grid=(L // tm,),
        in_specs=in_specs,
        out_specs=pl.BlockSpec((tm, D_MODEL), lambda i: (i, 0)),
        out_shape=jax.ShapeDtypeStruct((L, D_MODEL), F32),
        scratch_shapes=[pltpu.VMEM((tm, D_MODEL), BF16)],
        compiler_params=_cparams(1),
        name="ple_final" if final else "ple",
    )(*args)


def _chunk_tril(tm):
    r = jnp.arange(tm)
    same = (r[:, None] // GLA_CHUNK) == (r[None, :] // GLA_CHUNK)
    return (same & (r[:, None] >= r[None, :])).astype(BF16)


def kernel(x, p, norm_mix_w, w_in, s5_a_re, s5_a_im, s5_log_dt, s5_b_re, s5_b_im, s5_c_re, s5_c_im, s5_d, s5_w_glu, s5_b_glu, pool_w, pool_scale, gla_w_a2, gla_b_a, gla_norm_w, w_out, norm_ffn_w, w_up, conv_w, conv_b, w_down, norm_ple_w, w_ple, w_pg, final_norm_w):
    bsz, L, _ = x.shape
    tb_s5 = min(S5_TB, L // S5_SEGMENTS)
    assert bsz == 1 and L % (S5_SEGMENTS * tb_s5) == 0 and tb_s5 % S5_STRIDE == 0
    depth = w_in.shape[0]
    tm = min(512, L)
    tm_ffn = min(1024, L)
    tm_mix = min(512, L)
    tri = _chunk_tril(tm_mix)
    nw_ffn = norm_ffn_w.reshape(depth, 1, D_MODEL)
    conv_b3 = conv_b.reshape(depth, 1, D_FF)
    w_in_b = jnp.concatenate(
        [w_in.astype(BF16), jnp.zeros((depth, D_MODEL, Z_WIDTH - w_in.shape[-1]), BF16)], axis=-1)
    w_out_b, w_pg_b, w_ple_b = w_out.astype(BF16), w_pg.astype(BF16), w_ple.astype(BF16)
    nw_mix = norm_mix_w.reshape(depth, 1, D_MODEL)
    nw_ple = norm_ple_w.reshape(depth, 1, D_MODEL)
    p3 = p.reshape(depth, L, PLE_DIM)
    s5_prm = jax.vmap(functools.partial(_s5_prepare, seg_len=L // S5_SEGMENTS))(
        s5_a_re, s5_a_im, s5_log_dt, s5_b_re, s5_b_im, s5_c_re, s5_c_im, s5_d, s5_w_glu, s5_b_glu)
    mix_prm = {
        "pool_w": pool_w.astype(BF16), "pool_scale": pool_scale.reshape(depth, 1, POOL_WIDTH),
        "w_a2": jnp.pad(gla_w_a2, ((0, 0), (0, LANES - GLA_GATE_RANK), (0, 0))).astype(BF16),
        "b_a": gla_b_a.reshape(depth, 1, GLA_K_WIDTH), "gla_norm_w": gla_norm_w.reshape(depth, 1, GLA_DV),
        "tri": tri,
    }
    h = x.reshape(L, D_MODEL)
    for i in range(depth):
        z = _inproj(h, nw_mix, w_in_b, i, tm)
        y_s5, w_down_b, w_up_b = _s5_mixer(z, s5_prm, w_down, w_up, i, tb_s5)
        y_pool, y_gla = _mixers(z, mix_prm, i, tm_mix)
        h, a = _outproj(h, y_s5, y_pool, y_gla, w_out_b, nw_ffn, i, tm)
        h = _ffn(a, h, w_up_b, conv_w, conv_b3, w_down_b, i, tm_ffn)
        final_w = final_norm_w.reshape(1, D_MODEL) if i == depth - 1 else None
        h = _ple(h, p3, nw_ple, w_pg_b, w_ple_b, final_w, i, tm)
    return h.reshape(bsz, L, D_MODEL)
```

```python
import functools
import math

import jax
import jax.numpy as jnp
from jax import lax
from jax.experimental import pallas as pl
from jax.experimental.pallas import tpu as pltpu

F32 = jnp.float32
BF16 = jnp.bfloat16

D_MODEL = 2048
S5_WIDTH = 512
S5_GROUP = 16
S5_GROUPS = 32
S5_STATE = 64
S5_COLS = S5_GROUPS * S5_STATE
POOL_WIDTH = 512
POOL_WINDOWS = (2, 4, 8, 16)
POOL_GROUP = 128
GLA_HEADS = 4
GLA_DK = 128
GLA_DV = 256
GLA_K_WIDTH = 512
GLA_V_WIDTH = 1024
GLA_GATE_RANK = 16
GLA_TAU = 16.0
GLA_CHUNK = 64
D_FF = 5632
PLE_DIM = 256
EPS = 1e-6

LANES = 128
SUBLANES = 8
S5_SEGMENTS = SUBLANES
S5_BUNDLE = LANES // S5_GROUP
S5_NBUNDLES = S5_GROUPS // S5_BUNDLE
S5_BCOLS = S5_BUNDLE * S5_STATE
S5_TB = 128
S5_STRIDE = 4
POOL_HALO = 16

Z_S5, Z_POOL, Z_Q, Z_K, Z_V, Z_R, Z_G = 0, 512, 1024, 1536, 2048, 3072, 4096
Z_WIDTH = 4224
MM_TN = 512

VMEM_LIMIT = 56 * 1024 * 1024


def _cparams(n_axes):
    return pltpu.CompilerParams(dimension_semantics=("arbitrary",) * n_axes,
                                vmem_limit_bytes=VMEM_LIMIT)


def _resident(shape):
    nd = len(shape)
    return pl.BlockSpec(shape, lambda *_: (0,) * nd, pipeline_mode=pl.Buffered(1))


def _layer_block(arr, layer):
    nd = arr.ndim
    return pl.BlockSpec((None,) + arr.shape[1:], lambda *_: (layer,) + (0,) * (nd - 1),
                        pipeline_mode=pl.Buffered(1))


def _rider(w, layer, nsteps):
    slab = w.shape[1] // nsteps
    assert slab * nsteps == w.shape[1] and slab % (2 * SUBLANES) == 0
    return (pl.BlockSpec((None, slab, w.shape[2]), lambda i: (layer, i, 0)),
            pl.BlockSpec((slab, w.shape[2]), lambda i: (i, 0)),
            jax.ShapeDtypeStruct(w.shape[1:], BF16))


def _rms(x, w):
    ms = jnp.mean(x * x, axis=-1, keepdims=True)
    return x * lax.rsqrt(ms + EPS) * w


def _sigmoid(x):
    return 0.5 * (1.0 + jnp.tanh(0.5 * x))


def _dot(a, b):
    return jnp.dot(a, b, preferred_element_type=F32)


def _dot_nt(a, b):
    return lax.dot_general(a, b, (((1,), (1,)), ((), ())), preferred_element_type=F32)


def _dot_tn(a, b):
    return lax.dot_general(a, b, (((0,), (0,)), ((), ())), preferred_element_type=F32)


def _col_chunks(n):
    return [(c0, min(c0 + MM_TN, n)) for c0 in range(0, n, MM_TN)]


MXU_K = 256


def _rms_split(h_ref, w_ref, a_ref, width):
    sq = None
    for k0 in range(0, h_ref.shape[1], MXU_K):
        hk = h_ref[:, k0:k0 + MXU_K]
        a_ref[:, k0:k0 + MXU_K] = (hk * w_ref[:, k0:k0 + MXU_K]).astype(BF16)
        sq = hk * hk if sq is None else sq + hk * hk
    rinv = lax.rsqrt(jnp.sum(sq, axis=-1, keepdims=True) * (1.0 / h_ref.shape[1]) + EPS)
    return jnp.broadcast_to(rinv, (h_ref.shape[0], width))


def _inproj_kernel(h_ref, nw_ref, w_ref, ride1_ref, ride2_ref, z_ref, cast1_ref, cast2_ref, a_ref):
    cast1_ref[...] = ride1_ref[...].astype(BF16)
    cast2_ref[...] = ride2_ref[...].astype(BF16)
    rinv = _rms_split(h_ref, nw_ref, a_ref, MM_TN)
    a = a_ref[...]
    zt = jnp.concatenate([_dot(a, w_ref[:, Z_R + c0:Z_R + c1]) for c0, c1 in _col_chunks(Z_WIDTH - Z_R)],
                         axis=1)
    z_ref[:, Z_G:] = rinv[:, :LANES] * zt[:, :LANES]
    r = zt[:, GLA_GATE_RANK:GLA_GATE_RANK + GLA_V_WIDTH]
    for c0, c1 in _col_chunks(GLA_V_WIDTH):
        z_ref[:, Z_R + c0:Z_R + c1] = rinv * r[:, c0:c1]
    for c0, c1 in _col_chunks(Z_R):
        z_ref[:, c0:c1] = rinv * _dot(a, w_ref[:, c0:c1])


def _inproj(h, nw, w, ride1, ride2, layer, tm):
    L = h.shape[0]
    r1_in, r1_out, r1_shape = _rider(ride1, layer, L // tm)
    r2_in, r2_out, r2_shape = _rider(ride2, layer, L // tm)
    return pl.pallas_call(
        _inproj_kernel,
        grid=(L // tm,),
        in_specs=[pl.BlockSpec((tm, D_MODEL), lambda i: (i, 0)),
                  pl.BlockSpec((None, 1, D_MODEL), lambda i: (layer, 0, 0)),
                  _layer_block(w, layer), r1_in, r2_in],
        out_specs=[pl.BlockSpec((tm, Z_WIDTH), lambda i: (i, 0)), r1_out, r2_out],
        out_shape=[jax.ShapeDtypeStruct((L, Z_WIDTH), F32), r1_shape, r2_shape],
        scratch_shapes=[pltpu.VMEM((tm, D_MODEL), BF16)],
        compiler_params=_cparams(1),
        name="inproj",
    )(h, nw, w, ride1, ride2)


def _gelu_tanh(x):
    return 0.5 * x * (1.0 + jnp.tanh(math.sqrt(2.0 / math.pi) * (x + 0.044715 * (x * x * x))))


def _s5_kernel(*refs, tb, pass2):
    if pass2:
        (u_ref, wa_ref, are_ref, aim_ref, ride_ref, xe_re_ref, xe_im_ref, ap_re_ref, ap_im_ref,
         wy_ref, d_ref, wglu_ref, bglu_ref, y_ref, cast_ref, ubuf, xb, st_re, st_im) = refs
    else:
        (u_ref, wa_ref, are_ref, aim_ref, xe_re_ref, xe_im_ref, ubuf, st_re, st_im) = refs
    nseg, s = S5_SEGMENTS, S5_STRIDE
    nsub = tb // s
    prow = nsub * nseg
    if pass2:
        cast_ref[...] = ride_ref[...].astype(BF16)

    @pl.when(pl.program_id(0) == 0)
    def _init():
        if pass2:
            apr, api = ap_re_ref[...], ap_im_ref[...]
            st_re[0:1, :] = jnp.zeros((1, S5_COLS), F32)
            st_im[0:1, :] = jnp.zeros((1, S5_COLS), F32)
            for j in range(nseg - 1):
                cr, ci = st_re[j:j + 1, :], st_im[j:j + 1, :]
                st_re[j + 1:j + 2, :] = apr * cr - api * ci + xe_re_ref[j:j + 1, :]
                st_im[j + 1:j + 2, :] = apr * ci + api * cr + xe_im_ref[j:j + 1, :]
        else:
            st_re[...] = jnp.zeros((nseg, S5_COLS), F32)
            st_im[...] = jnp.zeros((nseg, S5_COLS), F32)

    for j in range(nseg):
        uj = u_ref[j]
        for b in range(S5_NBUNDLES):
            ubuf[b, pl.ds(j, tb, stride=nseg), :] = uj[:, b * LANES:(b + 1) * LANES]
    ys = []
    for b in range(S5_NBUNDLES):
        v = ubuf[b].reshape(nsub, s * nseg, LANES)
        lhs = jnp.concatenate([v[:, k * nseg:(k + 1) * nseg, :].reshape(prow, LANES) for k in range(s)],
                              axis=1).astype(BF16)
        bu = _dot(lhs, wa_ref[b])
        cols = slice(b * S5_BCOLS, (b + 1) * S5_BCOLS)
        cre = slice(2 * b * S5_BCOLS, (2 * b + 1) * S5_BCOLS)
        cim = slice((2 * b + 1) * S5_BCOLS, (2 * b + 2) * S5_BCOLS)
        ar = jnp.broadcast_to(are_ref[:, cols], (nseg, S5_BCOLS))
        ai = jnp.broadcast_to(aim_ref[:, cols], (nseg, S5_BCOLS))
        xr, xi = st_re[:, cols], st_im[:, cols]
        for m in range(nsub):
            rows = slice(m * nseg, (m + 1) * nseg)
            if pass2:
                xb[rows, cre] = xr
                xb[rows, cim] = xi
            xr, xi = (ar * xr - ai * xi + bu[rows, :S5_BCOLS], ar * xi + ai * xr + bu[rows, S5_BCOLS:])
        st_re[:, cols] = xr
        st_im[:, cols] = xi
        if pass2:
            xp = xb[:, 2 * b * S5_BCOLS:2 * (b + 1) * S5_BCOLS].astype(BF16)
            yp = _dot(jnp.concatenate([lhs, xp], axis=1), wy_ref[b])
            ys.append(jnp.concatenate(
                [yp[:, k * LANES:(k + 1) * LANES].reshape(nsub, nseg, LANES) for k in range(s)],
                axis=1).reshape(tb * nseg, LANES))

    if not pass2:
        xe_re_ref[...] = st_re[...]
        xe_im_ref[...] = st_im[...]
        return

    u = jnp.concatenate([ubuf[b] for b in range(S5_NBUNDLES)], axis=1)
    y = jnp.concatenate(ys, axis=1) + d_ref[...] * u
    y = _gelu_tanh(y)
    glu = _sigmoid(_dot(y.astype(BF16), wglu_ref[...]) + bglu_ref[...])
    out = y * glu
    for b in range(S5_NBUNDLES):
        ubuf[b] = out[:, b * LANES:(b + 1) * LANES]
    for j in range(nseg):
        y_ref[j] = jnp.concatenate(
            [ubuf[b, pl.ds(j, tb, stride=nseg), :] for b in range(S5_NBUNDLES)], axis=1).astype(BF16)


def _s5_mixer(z, prm, ride, layer, tb):
    L = z.shape[0]
    lb = lambda name: _layer_block(prm[name], layer)
    nseg = S5_SEGMENTS
    seg_len = L // nseg
    nblk = seg_len // tb
    rows = tb * nseg
    r2_in, r2_out, r2_shape = _rider(ride, layer, nblk)
    z3 = z.reshape(nseg, seg_len, Z_WIDTH)
    u_spec = pl.BlockSpec((nseg, tb, S5_WIDTH), lambda i: (0, i, Z_S5 // S5_WIDTH))
    state_shape = jax.ShapeDtypeStruct((nseg, S5_COLS), F32)
    ubuf = pltpu.VMEM((S5_NBUNDLES, rows, LANES), F32)
    state = [pltpu.VMEM((nseg, S5_COLS), F32), pltpu.VMEM((nseg, S5_COLS), F32)]
    common = [u_spec, lb("wa"), lb("as_re"), lb("as_im")]

    xe_re, xe_im = pl.pallas_call(
        functools.partial(_s5_kernel, tb=tb, pass2=False),
        grid=(nblk,),
        in_specs=common,
        out_specs=[_resident_out((nseg, S5_COLS)), _resident_out((nseg, S5_COLS))],
        out_shape=[state_shape, state_shape],
        scratch_shapes=[ubuf] + state,
        compiler_params=_cparams(1),
        name="s5_states",
    )(z3, prm["wa"], prm["as_re"], prm["as_im"])

    y, cast2 = pl.pallas_call(
        functools.partial(_s5_kernel, tb=tb, pass2=True),
        grid=(nblk,),
        in_specs=common + [r2_in, _resident((nseg, S5_COLS)), _resident((nseg, S5_COLS)),
                           lb("ap_re"), lb("ap_im"), lb("wy"), lb("d"), lb("w_glu"), lb("b_glu")],
        out_specs=[pl.BlockSpec((nseg, tb, S5_WIDTH), lambda i: (0, i, 0)), r2_out],
        out_shape=[jax.ShapeDtypeStruct((nseg, seg_len, S5_WIDTH), BF16), r2_shape],
        scratch_shapes=[ubuf, pltpu.VMEM((rows // S5_STRIDE, 2 * S5_COLS), F32)] + state,
        compiler_params=_cparams(1),
        name="s5_outputs",
    )(z3, prm["wa"], prm["as_re"], prm["as_im"], ride, xe_re, xe_im, prm["ap_re"], prm["ap_im"],
      prm["wy"], prm["d"], prm["w_glu"], prm["b_glu"])
    return y.reshape(L, S5_WIDTH), cast2


def _resident_out(shape):
    nd = len(shape)
    return pl.BlockSpec(shape, lambda *_: (0,) * nd)


def _s5_prepare(a_re, a_im, log_dt, b_re, b_im, c_re, c_im, d_skip, w_glu, b_glu, seg_len):
    dt = jnp.exp(log_dt)[:, None]
    mag = jnp.exp(a_re * dt)
    ab_re, ab_im = mag * jnp.cos(a_im * dt), mag * jnp.sin(a_im * dt)
    nr, ni = ab_re - 1.0, ab_im
    den = a_re * a_re + a_im * a_im
    f_re, f_im = (nr * a_re + ni * a_im) / den, (ni * a_re - nr * a_im) / den
    bb_re = f_re[..., None] * b_re - f_im[..., None] * b_im
    bb_im = f_re[..., None] * b_im + f_im[..., None] * b_re
    pr, pi = jnp.ones_like(ab_re), jnp.zeros_like(ab_re)
    sr, si, e = ab_re, ab_im, seg_len
    while e:
        if e & 1:
            pr, pi = pr * sr - pi * si, pr * si + pi * sr
        sr, si = sr * sr - si * si, 2.0 * sr * si
        e >>= 1
    s, nb, gb = S5_STRIDE, S5_NBUNDLES, S5_BUNDLE
    qr, qi = [jnp.ones_like(ab_re)], [jnp.zeros_like(ab_re)]
    for _ in range(s):
        qr, qi = qr + [qr[-1] * ab_re - qi[-1] * ab_im], qi + [qr[-1] * ab_im + qi[-1] * ab_re]
    p_re, p_im = jnp.stack(qr), jnp.stack(qi)

    d_re = jnp.stack([qr[s - 1 - k] for k in range(s)])[..., None]
    d_im = jnp.stack([qi[s - 1 - k] for k in range(s)])[..., None]

    def spread(compact, rows_group, cols_per_group, col_blocks):
        width = col_blocks * gb * cols_per_group
        col = jnp.arange(width)
        src = (col // (gb * cols_per_group)) * cols_per_group + col % cols_per_group
        tile = (jnp.arange(col_blocks * cols_per_group)[:, None] == src[None, :]).astype(F32)
        mask = (rows_group[:, None] == ((col // cols_per_group) % gb)[None, :]).astype(F32)
        return jnp.einsum("brm,mc->brc", compact, tile) * mask

    in_rows = (jnp.arange(s * LANES) // S5_GROUP) % gb
    st_rows = (jnp.arange(2 * S5_BCOLS) // S5_STATE) % gb

    sb = jnp.stack([d_re * bb_re - d_im * bb_im, d_re * bb_im + d_im * bb_re])
    wa_c = sb.reshape(2, s, nb, gb, S5_STATE, S5_GROUP).transpose(2, 1, 3, 5, 0, 4).reshape(
        nb, s * LANES, 2 * S5_STATE)
    wa = spread(wa_c, in_rows, S5_STATE, 2)

    h_re = c_re * p_re[:, :, None, :] - c_im * p_im[:, :, None, :]
    h_im = c_re * p_im[:, :, None, :] + c_im * p_re[:, :, None, :]
    hs = jnp.stack([h_re[1:], -h_im[1:]])
    ws_c = hs.reshape(2, s, nb, gb, S5_GROUP, S5_STATE).transpose(2, 0, 3, 5, 1, 4).reshape(
        nb, 2 * S5_BCOLS, s * S5_GROUP)
    ws = spread(ws_c, st_rows, S5_GROUP, s)
    k_lag = (jnp.einsum("lgon,gnc->lgoc", h_re[:s], bb_re)
             - jnp.einsum("lgon,gnc->lgoc", h_im[:s], bb_im))
    zero = jnp.zeros_like(k_lag[0])
    k_jk = jnp.stack([jnp.stack([k_lag[k - j] if k >= j else zero for k in range(s)])
                      for j in range(s)])
    wl_c = k_jk.reshape(s, s, nb, gb, S5_GROUP, S5_GROUP).transpose(2, 0, 3, 5, 1, 4).reshape(
        nb, s * LANES, s * S5_GROUP)
    wl = spread(wl_c, in_rows, S5_GROUP, s)
    return {
        "wa": wa.astype(BF16),
        "wy": jnp.concatenate([wl, ws], axis=1).astype(BF16),
        "as_re": p_re[s].reshape(1, S5_COLS), "as_im": p_im[s].reshape(1, S5_COLS),
        "ap_re": pr.reshape(1, S5_COLS), "ap_im": pi.reshape(1, S5_COLS),
        "d": d_skip.reshape(1, S5_WIDTH), "w_glu": w_glu.astype(BF16),
        "b_glu": b_glu.reshape(1, S5_WIDTH),
    }


def _mix_kernel(zp_ref, q_ref, k_ref, v_ref, r_ref, g_ref, pw_ref, ps_ref, wa_ref, ba_ref, gnw_ref,
                tri_ref, ypool_ref, ygla_ref, zext, s_ref, *, tm):
    i = pl.program_id(0)

    @pl.when(i == 0)
    def _init():
        zext[0:POOL_HALO, :] = jnp.zeros((POOL_HALO, POOL_WIDTH), F32)
        s_ref[...] = jnp.zeros(s_ref.shape, F32)

    zext[POOL_HALO:POOL_HALO + tm, :] = zp_ref[...]
    pos = (i * tm + 1 + lax.broadcasted_iota(jnp.int32, (tm, 1), 0)).astype(F32)
    for gi, w in enumerate(POOL_WINDOWS):
        cols = slice(gi * POOL_GROUP, (gi + 1) * POOL_GROUP)
        ze = zext[:, cols]
        s, span = ze, 1
        while span < w:
            s = s + pltpu.roll(s, span, axis=0)
            span *= 2
        zc = ze[POOL_HALO:, :]
        pooled = s[POOL_HALO:, :] / jnp.minimum(pos, float(w)) - zc
        mixed = _dot(pooled.astype(BF16), pw_ref[gi]) * ps_ref[:, cols]
        ypool_ref[:, cols] = mixed.astype(BF16)
    zext[0:POOL_HALO, :] = zext[tm:tm + POOL_HALO, :]

    logit = _dot(g_ref[...].astype(BF16), wa_ref[...]) + ba_ref[...]
    la = (jnp.minimum(logit, 0.0) - jnp.log(1.0 + jnp.exp(-jnp.abs(logit)))) / GLA_TAU
    la_hi = la.astype(BF16)
    la_lo = (la - la_hi.astype(F32)).astype(BF16)
    tri = tri_ref[...]
    b = _dot(tri, la_hi) + _dot(tri, la_lo)
    nch = tm // GLA_CHUNK
    b_last = jnp.concatenate(
        [jnp.broadcast_to(b[(c + 1) * GLA_CHUNK - 1:(c + 1) * GLA_CHUNK, :], (GLA_CHUNK, GLA_K_WIDTH))
         for c in range(nch)], axis=0)
    q_dec = (q_ref[...] * (GLA_DK ** -0.5) * jnp.exp(b)).astype(BF16)
    k = k_ref[...]
    k_dec = (k * jnp.exp(-b)).astype(BF16)
    k_end = (k * jnp.exp(b_last - b)).astype(BF16)
    decay = jnp.exp(b_last)
    causal = (lax.broadcasted_iota(jnp.int32, (GLA_CHUNK, GLA_CHUNK), 0)
              >= lax.broadcasted_iota(jnp.int32, (GLA_CHUNK, GLA_CHUNK), 1))
    gnw = gnw_ref[...]
    units = [(hd, c) for hd in range(GLA_HEADS) for c in range(nch)]
    rows = lambda c: slice(c * GLA_CHUNK, (c + 1) * GLA_CHUNK)
    kcol = lambda hd: slice(hd * GLA_DK, (hd + 1) * GLA_DK)
    vcol = lambda hd: slice(hd * GLA_DV, (hd + 1) * GLA_DV)
    vv = {(hd, c): v_ref[rows(c), vcol(hd)].astype(BF16) for hd, c in units}
    scores = {(hd, c): _dot_nt(q_dec[rows(c), kcol(hd)], k_dec[rows(c), kcol(hd)]) for hd, c in units}
    kv = {(hd, c): _dot_tn(vv[hd, c], k_end[rows(c), kcol(hd)]) for hd, c in units}
    st_in = {}
    for hd in range(GLA_HEADS):
        st = s_ref[hd]
        for c in range(nch):
            st_in[hd, c] = st.astype(BF16)
            st = decay[c * GLA_CHUNK:c * GLA_CHUNK + 1, kcol(hd)] * st + kv[hd, c]
        s_ref[hd] = st
    for hd, c in units:
        sc = jnp.where(causal, scores[hd, c], 0.0).astype(BF16)
        o = _dot(sc, vv[hd, c]) + _dot_nt(q_dec[rows(c), kcol(hd)], st_in[hd, c])
        o = _rms(o, gnw)
        rr = r_ref[rows(c), vcol(hd)]
        ygla_ref[rows(c), vcol(hd)] = (o * (rr * _sigmoid(rr))).astype(BF16)


def _mixers(z, prm, layer, tm):
    L = z.shape[0]
    row = lambda w, col: pl.BlockSpec((tm, w), lambda i: (i, col // w))
    lb = lambda name: _layer_block(prm[name], layer)
    return pl.pallas_call(
        functools.partial(_mix_kernel, tm=tm),
        grid=(L // tm,),
        in_specs=[row(POOL_WIDTH, Z_POOL), row(GLA_K_WIDTH, Z_Q), row(GLA_K_WIDTH, Z_K),
                  row(GLA_V_WIDTH, Z_V), row(GLA_V_WIDTH, Z_R), row(LANES, Z_G),
                  lb("pool_w"), lb("pool_scale"), lb("w_a2"), lb("b_a"), lb("gla_norm_w"),
                  _resident((tm, tm))],
        out_specs=[pl.BlockSpec((tm, POOL_WIDTH), lambda i: (i, 0)),
                   pl.BlockSpec((tm, GLA_V_WIDTH), lambda i: (i, 0))],
        out_shape=[jax.ShapeDtypeStruct((L, POOL_WIDTH), BF16),
                   jax.ShapeDtypeStruct((L, GLA_V_WIDTH), BF16)],
        scratch_shapes=[pltpu.VMEM((tm + POOL_HALO, POOL_WIDTH), F32),
                        pltpu.VMEM((GLA_HEADS, GLA_DV, GLA_DK), F32)],
        compiler_params=_cparams(1),
        name="mixers",
    )(z, z, z, z, z, z, prm["pool_w"], prm["pool_scale"], prm["w_a2"], prm["b_a"], prm["gla_norm_w"],
      prm["tri"])


def _outproj_kernel(h_ref, ys_ref, yp_ref, yg_ref, w_ref, nw_ref, ride_ref, o_ref, a_ref, cast_ref, m_ref):
    cast_ref[...] = ride_ref[...].astype(BF16)
    m_ref[:, 0:S5_WIDTH] = ys_ref[...]
    m_ref[:, S5_WIDTH:S5_WIDTH + POOL_WIDTH] = yp_ref[...]
    m_ref[:, S5_WIDTH + POOL_WIDTH:] = yg_ref[...]
    m = m_ref[...]
    for c0, c1 in _col_chunks(D_MODEL):
        o_ref[:, c0:c1] = h_ref[:, c0:c1] + _dot(m, w_ref[:, c0:c1])
    a_ref[...] = _rms(o_ref[...], nw_ref[...]).astype(BF16)


def _outproj(h, ys, yp, yg, w, nw, ride, layer, tm):
    L = h.shape[0]
    row = lambda wd: pl.BlockSpec((tm, wd), lambda i: (i, 0))
    r_in, r_out, r_shape = _rider(ride, layer, L // tm)
    return pl.pallas_call(
        _outproj_kernel,
        grid=(L // tm,),
        in_specs=[row(D_MODEL), row(S5_WIDTH), row(POOL_WIDTH), row(GLA_V_WIDTH),
                  pl.BlockSpec((D_MODEL, D_MODEL), lambda i: (0, 0), pipeline_mode=pl.Buffered(1)),
                  pl.BlockSpec((None, 1, D_MODEL), lambda i: (layer, 0, 0)), r_in],
        out_specs=[row(D_MODEL), row(D_MODEL), r_out],
        out_shape=[jax.ShapeDtypeStruct((L, D_MODEL), F32), jax.ShapeDtypeStruct((L, D_MODEL), BF16),
                   r_shape],
        scratch_shapes=[pltpu.VMEM((tm, D_MODEL), BF16)],
        compiler_params=_cparams(1),
        name="outproj",
    )(h, ys, yp, yg, w, nw, ride)


FFN_FC = 512
FFN_NC = D_FF // FFN_FC
FFN_FN = 512
FFN_NN = D_MODEL // FFN_FN
CONV_HALO = SUBLANES


def _serpentine(i, k, n):
    k = jnp.clip(k, 0, n - 1)
    return jnp.where(i % 2 == 0, k, n - 1 - k)


def _ffn_kernel(a_ref, h_ref, wg_ref, wv_ref, cw_ref, cb_ref, wd_ref, o_ref, act_ref, graw, carry, *, tm):
    i, s = pl.program_id(0), pl.program_id(1)

    @pl.when(s < FFN_NC)
    def _up():
        c = _serpentine(i, s, FFN_NC)
        cols = pl.ds(pl.multiple_of(c * FFN_FC, FFN_FC), FFN_FC)
        a = a_ref[...]
        graw[0:CONV_HALO, :] = jnp.where(i > 0, carry[c], 0.0)
        graw[CONV_HALO:CONV_HALO + tm, :] = _dot(a, wg_ref[...])
        carry[c] = graw[tm:tm + CONV_HALO, :]
        cw = cw_ref[:, cols]
        g = graw[...]
        back = lambda k: pltpu.roll(g, k, axis=0)[CONV_HALO:, :]
        gc = cb_ref[:, cols] + back(2) * cw[0:1, :]
        gc = gc + back(1) * cw[1:2, :]
        gc = gc + g[CONV_HALO:, :] * cw[2:3, :]
        act_ref[:, cols] = (gc * _sigmoid(gc) * _dot(a, wv_ref[...])).astype(BF16)

    @pl.when(s >= FFN_NC)
    def _down():
        o_ref[...] = h_ref[...] + _dot(act_ref[...], wd_ref[...])


def _ffn(a, h, w_up, conv_w, conv_b, w_down, layer, tm):
    L = h.shape[0]
    up_c = lambda i, s: _serpentine(i, s, FFN_NC)
    down_n = lambda i, s: _serpentine(i, s - FFN_NC, FFN_NN)
    return pl.pallas_call(
        functools.partial(_ffn_kernel, tm=tm),
        grid=(L // tm, FFN_NC + FFN_NN),
        in_specs=[pl.BlockSpec((tm, D_MODEL), lambda i, s: (i, 0)),
                  pl.BlockSpec((tm, FFN_FN), lambda i, s: (i, down_n(i, s))),
                  pl.BlockSpec((D_MODEL, FFN_FC), lambda i, s: (0, up_c(i, s))),
                  pl.BlockSpec((D_MODEL, FFN_FC), lambda i, s: (0, FFN_NC + up_c(i, s))),
                  pl.BlockSpec((None, 3, D_FF), lambda i, s: (layer, 0, 0)),
                  pl.BlockSpec((None, 1, D_FF), lambda i, s: (layer, 0, 0)),
                  pl.BlockSpec((D_FF, FFN_FN), lambda i, s: (0, down_n(i, s)))],
        out_specs=pl.BlockSpec((tm, FFN_FN), lambda i, s: (i, down_n(i, s))),
        out_shape=jax.ShapeDtypeStruct((L, D_MODEL), F32),
        scratch_shapes=[pltpu.VMEM((tm, D_FF), BF16),
                        pltpu.VMEM((tm + CONV_HALO, FFN_FC), F32),
                        pltpu.VMEM((FFN_NC, CONV_HALO, FFN_FC), F32)],
        compiler_params=_cparams(2),
        name="ffn",
    )(a, h, w_up, w_up, conv_w, conv_b, w_down)


def _ple_kernel(*refs, final):
    if final:
        h_ref, p_ref, nw_ref, wpg_ref, wple_ref, fw_ref, o_ref, a_ref = refs
    else:
        h_ref, p_ref, nw_ref, wpg_ref, wple_ref, o_ref, a_ref = refs
    rinv = _rms_split(h_ref, nw_ref, a_ref, MM_TN)
    a = a_ref[...]
    pe = p_ref[...].astype(BF16)
    for c0, c1 in _col_chunks(D_MODEL):
        gate = _sigmoid(rinv * _dot(a, wpg_ref[:, c0:c1]))
        o_ref[:, c0:c1] = h_ref[:, c0:c1] + _dot(pe, wple_ref[:, c0:c1]) * gate
    if final:
        o_ref[...] = _rms(o_ref[...], fw_ref[...])


def _ple(h, p, nw, w_pg, w_ple, final_w, layer, tm):
    L = h.shape[0]
    final = final_w is not None
    in_specs = [pl.BlockSpec((tm, D_MODEL), lambda i: (i, 0)),
                pl.BlockSpec((None, tm, PLE_DIM), lambda i: (layer, i, 0)),
                pl.BlockSpec((None, 1, D_MODEL), lambda i: (layer, 0, 0)),
                pl.BlockSpec((D_MODEL, D_MODEL), lambda i: (0, 0), pipeline_mode=pl.Buffered(1)),
                pl.BlockSpec((None, PLE_DIM, D_MODEL), lambda i: (layer, 0, 0),
                             pipeline_mode=pl.Buffered(1))]
    args = [h, p, nw, w_pg, w_ple]
    if final:
        in_specs.append(_resident((1, D_MODEL)))
        args.append(final_w)
    return pl.pallas_call(
        functools.partial(_ple_kernel, final=final),
        grid=(L // tm,),
        in_specs=in_specs,
        out_specs=pl.BlockSpec((tm, D_MODEL), lambda i: (i, 0)),
        out_shape=jax.ShapeDtypeStruct((L, D_MODEL), F32),
        scratch_shapes=[pltpu.VMEM((tm, D_MODEL), BF16)],
        compiler_params=_cparams(1),
        name="ple_final" if final else "ple",
    )(*args)


def _chunk_tril(tm):
    r = jnp.arange(tm)
    same = (r[:, None] // GLA_CHUNK) == (r[None, :] // GLA_CHUNK)
    return (same & (r[:, None] >= r[None, :])).astype(BF16)


def kernel(x, p, norm_mix_w, w_in, s5_a_re, s5_a_im, s5_log_dt, s5_b_re, s5_b_im, s5_c_re, s5_c_im, s5_d, s5_w_glu, s5_b_glu, pool_w, pool_scale, gla_w_a2, gla_b_a, gla_norm_w, w_out, norm_ffn_w, w_up, conv_w, conv_b, w_down, norm_ple_w, w_ple, w_pg, final_norm_w):
    bsz, L, _ = x.shape
    tb_s5 = min(S5_TB, L // S5_SEGMENTS)
    assert bsz == 1 and L % (S5_SEGMENTS * tb_s5) == 0 and tb_s5 % S5_STRIDE == 0
    depth = w_in.shape[0]
    tm = min(512, L)
    tm_ffn = min(1024, L)
    tm_mix = min(512, L)
    tri = _chunk_tril(tm_mix)
    nw_ffn = norm_ffn_w.reshape(depth, 1, D_MODEL)
    conv_b3 = conv_b.reshape(depth, 1, D_FF)
    w_in_b = jnp.concatenate(
        [w_in.astype(BF16), jnp.zeros((depth, D_MODEL, Z_WIDTH - w_in.shape[-1]), BF16)], axis=-1)
    w_ple_b = w_ple.astype(BF16)
    nw_mix = norm_mix_w.reshape(depth, 1, D_MODEL)
    nw_ple = norm_ple_w.reshape(depth, 1, D_MODEL)
    p3 = p.reshape(depth, L, PLE_DIM)
    s5_prm = jax.vmap(functools.partial(_s5_prepare, seg_len=L // S5_SEGMENTS))(
        s5_a_re, s5_a_im, s5_log_dt, s5_b_re, s5_b_im, s5_c_re, s5_c_im, s5_d, s5_w_glu, s5_b_glu)
    mix_prm = {
        "pool_w": pool_w.astype(BF16), "pool_scale": pool_scale.reshape(depth, 1, POOL_WIDTH),
        "w_a2": jnp.pad(gla_w_a2, ((0, 0), (0, LANES - GLA_GATE_RANK), (0, 0))).astype(BF16),
        "b_a": gla_b_a.reshape(depth, 1, GLA_K_WIDTH), "gla_norm_w": gla_norm_w.reshape(depth, 1, GLA_DV),
        "tri": tri,
    }
    h = x.reshape(L, D_MODEL)
    for i in range(depth):
        z, w_down_b, w_out_b = _inproj(h, nw_mix, w_in_b, w_down, w_out, i, tm)
        y_s5, w_up_b = _s5_mixer(z, s5_prm, w_up, i, tb_s5)
        y_pool, y_gla = _mixers(z, mix_prm, i, tm_mix)
        h, a, w_pg_b = _outproj(h, y_s5, y_pool, y_gla, w_out_b, nw_ffn, w_pg, i, tm)
        h = _ffn(a, h, w_up_b, conv_w, conv_b3, w_down_b, i, tm_ffn)
        final_w = final_norm_w.reshape(1, D_MODEL) if i == depth - 1 else None
        h = _ple(h, p3, nw_ple, w_pg_b, w_ple_b, final_w, i, tm)
    return h.reshape(bsz, L, D_MODEL)
```

```python
import functools
import math

import jax
import jax.numpy as jnp
from jax import lax
from jax.experimental import pallas as pl
from jax.experimental.pallas import tpu as pltpu

F32 = jnp.float32
BF16 = jnp.bfloat16

D_MODEL = 2048
S5_WIDTH = 512
S5_GROUP = 16
S5_GROUPS = 32
S5_STATE = 64
S5_COLS = S5_GROUPS * S5_STATE
POOL_WIDTH = 512
POOL_WINDOWS = (2, 4, 8, 16)
POOL_GROUP = 128
GLA_HEADS = 4
GLA_DK = 128
GLA_DV = 256
GLA_K_WIDTH = 512
GLA_V_WIDTH = 1024
GLA_GATE_RANK = 16
GLA_TAU = 16.0
GLA_CHUNK = 64
D_FF = 5632
PLE_DIM = 256
EPS = 1e-6

LANES = 128
SUBLANES = 8
S5_SEGMENTS = SUBLANES
S5_BUNDLE = LANES // S5_GROUP
S5_NBUNDLES = S5_GROUPS // S5_BUNDLE
S5_BCOLS = S5_BUNDLE * S5_STATE
S5_TB = 128
S5_STRIDE = 4
POOL_HALO = 16

Z_S5, Z_POOL, Z_Q, Z_K, Z_V, Z_R, Z_G = 0, 512, 1024, 1536, 2048, 3072, 4096
Z_WIDTH = 4224
MM_TN = 512

VMEM_LIMIT = 56 * 1024 * 1024


def _cparams(n_axes):
    return pltpu.CompilerParams(dimension_semantics=("arbitrary",) * n_axes,
                                vmem_limit_bytes=VMEM_LIMIT)


def _resident(shape):
    nd = len(shape)
    return pl.BlockSpec(shape, lambda *_: (0,) * nd, pipeline_mode=pl.Buffered(1))


def _layer_block(arr, layer):
    nd = arr.ndim
    return pl.BlockSpec((None,) + arr.shape[1:], lambda *_: (layer,) + (0,) * (nd - 1),
                        pipeline_mode=pl.Buffered(1))


def _rider(w, layer, nsteps):
    slab = w.shape[1] // nsteps
    assert slab * nsteps == w.shape[1] and slab % (2 * SUBLANES) == 0
    return (pl.BlockSpec((None, slab, w.shape[2]), lambda i: (layer, i, 0)),
            pl.BlockSpec((slab, w.shape[2]), lambda i: (i, 0)),
            jax.ShapeDtypeStruct(w.shape[1:], BF16))


def _rms(x, w):
    ms = jnp.mean(x * x, axis=-1, keepdims=True)
    return x * lax.rsqrt(ms + EPS) * w


def _sigmoid(x):
    return 0.5 * (1.0 + jnp.tanh(0.5 * x))


def _silu(x):
    h = 0.5 * x
    return h + h * jnp.tanh(h)


def _dot(a, b):
    return jnp.dot(a, b, preferred_element_type=F32)


def _dot_nt(a, b):
    return lax.dot_general(a, b, (((1,), (1,)), ((), ())), preferred_element_type=F32)


def _dot_tn(a, b):
    return lax.dot_general(a, b, (((0,), (0,)), ((), ())), preferred_element_type=F32)


def _col_chunks(n):
    return [(c0, min(c0 + MM_TN, n)) for c0 in range(0, n, MM_TN)]


MXU_K = 256


def _rms_split(h_ref, w_ref, a_ref, width):
    sq = None
    for k0 in range(0, h_ref.shape[1], MXU_K):
        hk = h_ref[:, k0:k0 + MXU_K]
        a_ref[:, k0:k0 + MXU_K] = (hk * w_ref[:, k0:k0 + MXU_K]).astype(BF16)
        sq = hk * hk if sq is None else sq + hk * hk
    rinv = lax.rsqrt(jnp.sum(sq, axis=-1, keepdims=True) * (1.0 / h_ref.shape[1]) + EPS)
    return jnp.broadcast_to(rinv, (h_ref.shape[0], width))


def _inproj_kernel(h_ref, nw_ref, w_ref, ride1_ref, ride2_ref, z_ref, cast1_ref, cast2_ref, a_ref):
    cast1_ref[...] = ride1_ref[...].astype(BF16)
    cast2_ref[...] = ride2_ref[...].astype(BF16)
    rinv = _rms_split(h_ref, nw_ref, a_ref, MM_TN)
    a = a_ref[...]
    zt = jnp.concatenate([_dot(a, w_ref[:, Z_R + c0:Z_R + c1])
                          for c0, c1 in _col_chunks(w_ref.shape[1] - Z_R)], axis=1)
    z_ref[:, Z_G:] = rinv[:, :LANES] * zt[:, :LANES]
    r = zt[:, GLA_GATE_RANK:GLA_GATE_RANK + GLA_V_WIDTH]
    for c0, c1 in _col_chunks(GLA_V_WIDTH):
        z_ref[:, Z_R + c0:Z_R + c1] = rinv * r[:, c0:c1]
    for c0, c1 in _col_chunks(Z_R):
        z_ref[:, c0:c1] = rinv * _dot(a, w_ref[:, c0:c1])


def _inproj(h, nw, w, ride1, ride2, layer, tm):
    L = h.shape[0]
    r1_in, r1_out, r1_shape = _rider(ride1, layer, L // tm)
    r2_in, r2_out, r2_shape = _rider(ride2, layer, L // tm)
    return pl.pallas_call(
        _inproj_kernel,
        grid=(L // tm,),
        in_specs=[pl.BlockSpec((tm, D_MODEL), lambda i: (i, 0)),
                  pl.BlockSpec((None, 1, D_MODEL), lambda i: (layer, 0, 0)),
                  _layer_block(w, layer), r1_in, r2_in],
        out_specs=[pl.BlockSpec((tm, Z_WIDTH), lambda i: (i, 0)), r1_out, r2_out],
        out_shape=[jax.ShapeDtypeStruct((L, Z_WIDTH), F32), r1_shape, r2_shape],
        scratch_shapes=[pltpu.VMEM((tm, D_MODEL), BF16)],
        compiler_params=_cparams(1),
        name="inproj",
    )(h, nw, w, ride1, ride2)


def _gelu_tanh(x):
    return 0.5 * x * (1.0 + jnp.tanh(math.sqrt(2.0 / math.pi) * (x + 0.044715 * (x * x * x))))


def _s5_kernel(*refs, tb, pass2):
    if pass2:
        (u_ref, wa_ref, are_ref, aim_ref, ride_ref, xe_re_ref, xe_im_ref, ap_re_ref, ap_im_ref,
         wy_ref, d_ref, wglu_ref, bglu_ref, y_ref, cast_ref, ubuf, xb, st_re, st_im) = refs
    else:
        (u_ref, wa_ref, are_ref, aim_ref, xe_re_ref, xe_im_ref, ubuf, st_re, st_im) = refs
    nseg, s = S5_SEGMENTS, S5_STRIDE
    nsub = tb // s
    prow = nsub * nseg
    if pass2:
        cast_ref[...] = ride_ref[...].astype(BF16)

    @pl.when(pl.program_id(0) == 0)
    def _init():
        if pass2:
            apr, api = ap_re_ref[...], ap_im_ref[...]
            st_re[0:1, :] = jnp.zeros((1, S5_COLS), F32)
            st_im[0:1, :] = jnp.zeros((1, S5_COLS), F32)
            for j in range(nseg - 1):
                cr, ci = st_re[j:j + 1, :], st_im[j:j + 1, :]
                st_re[j + 1:j + 2, :] = apr * cr - api * ci + xe_re_ref[j:j + 1, :]
                st_im[j + 1:j + 2, :] = apr * ci + api * cr + xe_im_ref[j:j + 1, :]
        else:
            st_re[...] = jnp.zeros((nseg, S5_COLS), F32)
            st_im[...] = jnp.zeros((nseg, S5_COLS), F32)

    for j in range(nseg):
        uj = u_ref[j]
        for b in range(S5_NBUNDLES):
            ubuf[b, pl.ds(j, tb, stride=nseg), :] = uj[:, b * LANES:(b + 1) * LANES]
    ys = []
    for b in range(S5_NBUNDLES):
        v = ubuf[b].reshape(nsub, s * nseg, LANES)
        lhs = jnp.concatenate([v[:, k * nseg:(k + 1) * nseg, :].reshape(prow, LANES) for k in range(s)],
                              axis=1).astype(BF16)
        bu = _dot(lhs, wa_ref[b])
        cols = slice(b * S5_BCOLS, (b + 1) * S5_BCOLS)
        cre = slice(2 * b * S5_BCOLS, (2 * b + 1) * S5_BCOLS)
        cim = slice((2 * b + 1) * S5_BCOLS, (2 * b + 2) * S5_BCOLS)
        ar = jnp.broadcast_to(are_ref[:, cols], (nseg, S5_BCOLS))
        ai = jnp.broadcast_to(aim_ref[:, cols], (nseg, S5_BCOLS))
        xr, xi = st_re[:, cols], st_im[:, cols]
        for m in range(nsub):
            rows = slice(m * nseg, (m + 1) * nseg)
            if pass2:
                xb[rows, cre] = xr
                xb[rows, cim] = xi
            xr, xi = (ar * xr - ai * xi + bu[rows, :S5_BCOLS], ar * xi + ai * xr + bu[rows, S5_BCOLS:])
        st_re[:, cols] = xr
        st_im[:, cols] = xi
        if pass2:
            xp = xb[:, 2 * b * S5_BCOLS:2 * (b + 1) * S5_BCOLS].astype(BF16)
            yp = _dot(jnp.concatenate([lhs, xp], axis=1), wy_ref[b])
            ys.append(jnp.concatenate(
                [yp[:, k * LANES:(k + 1) * LANES].reshape(nsub, nseg, LANES) for k in range(s)],
                axis=1).reshape(tb * nseg, LANES))

    if not pass2:
        xe_re_ref[...] = st_re[...]
        xe_im_ref[...] = st_im[...]
        return

    u = jnp.concatenate([ubuf[b] for b in range(S5_NBUNDLES)], axis=1)
    y = jnp.concatenate(ys, axis=1) + d_ref[...] * u
    y = _gelu_tanh(y)
    glu = _sigmoid(_dot(y.astype(BF16), wglu_ref[...]) + bglu_ref[...])
    out = y * glu
    for b in range(S5_NBUNDLES):
        ubuf[b] = out[:, b * LANES:(b + 1) * LANES]
    for j in range(nseg):
        y_ref[j] = jnp.concatenate(
            [ubuf[b, pl.ds(j, tb, stride=nseg), :] for b in range(S5_NBUNDLES)], axis=1).astype(BF16)


def _s5_mixer(z, prm, ride, layer, tb):
    L = z.shape[0]
    lb = lambda name: _layer_block(prm[name], layer)
    nseg = S5_SEGMENTS
    seg_len = L // nseg
    nblk = seg_len // tb
    rows = tb * nseg
    r2_in, r2_out, r2_shape = _rider(ride, layer, nblk)
    z3 = z.reshape(nseg, seg_len, Z_WIDTH)
    u_spec = pl.BlockSpec((nseg, tb, S5_WIDTH), lambda i: (0, i, Z_S5 // S5_WIDTH))
    state_shape = jax.ShapeDtypeStruct((nseg, S5_COLS), F32)
    ubuf = pltpu.VMEM((S5_NBUNDLES, rows, LANES), F32)
    state = [pltpu.VMEM((nseg, S5_COLS), F32), pltpu.VMEM((nseg, S5_COLS), F32)]
    common = [u_spec, lb("wa"), lb("as_re"), lb("as_im")]

    xe_re, xe_im = pl.pallas_call(
        functools.partial(_s5_kernel, tb=tb, pass2=False),
        grid=(nblk,),
        in_specs=common,
        out_specs=[_resident_out((nseg, S5_COLS)), _resident_out((nseg, S5_COLS))],
        out_shape=[state_shape, state_shape],
        scratch_shapes=[ubuf] + state,
        compiler_params=_cparams(1),
        name="s5_states",
    )(z3, prm["wa"], prm["as_re"], prm["as_im"])

    y, cast2 = pl.pallas_call(
        functools.partial(_s5_kernel, tb=tb, pass2=True),
        grid=(nblk,),
        in_specs=common + [r2_in, _resident((nseg, S5_COLS)), _resident((nseg, S5_COLS)),
                           lb("ap_re"), lb("ap_im"), lb("wy"), lb("d"), lb("w_glu"), lb("b_glu")],
        out_specs=[pl.BlockSpec((nseg, tb, S5_WIDTH), lambda i: (0, i, 0)), r2_out],
        out_shape=[jax.ShapeDtypeStruct((nseg, seg_len, S5_WIDTH), BF16), r2_shape],
        scratch_shapes=[ubuf, pltpu.VMEM((rows // S5_STRIDE, 2 * S5_COLS), F32)] + state,
        compiler_params=_cparams(1),
        name="s5_outputs",
    )(z3, prm["wa"], prm["as_re"], prm["as_im"], ride, xe_re, xe_im, prm["ap_re"], prm["ap_im"],
      prm["wy"], prm["d"], prm["w_glu"], prm["b_glu"])
    return y.reshape(L, S5_WIDTH), cast2


def _resident_out(shape):
    nd = len(shape)
    return pl.BlockSpec(shape, lambda *_: (0,) * nd)


def _s5_prepare(a_re, a_im, log_dt, b_re, b_im, c_re, c_im, d_skip, w_glu, b_glu, seg_len):
    dt = jnp.exp(log_dt)[:, None]
    mag = jnp.exp(a_re * dt)
    ab_re, ab_im = mag * jnp.cos(a_im * dt), mag * jnp.sin(a_im * dt)
    nr, ni = ab_re - 1.0, ab_im
    den = a_re * a_re + a_im * a_im
    f_re, f_im = (nr * a_re + ni * a_im) / den, (ni * a_re - nr * a_im) / den
    bb_re = f_re[..., None] * b_re - f_im[..., None] * b_im
    bb_im = f_re[..., None] * b_im + f_im[..., None] * b_re
    pr, pi = jnp.ones_like(ab_re), jnp.zeros_like(ab_re)
    sr, si, e = ab_re, ab_im, seg_len
    while e:
        if e & 1:
            pr, pi = pr * sr - pi * si, pr * si + pi * sr
        sr, si = sr * sr - si * si, 2.0 * sr * si
        e >>= 1
    s, nb, gb = S5_STRIDE, S5_NBUNDLES, S5_BUNDLE
    qr, qi = [jnp.ones_like(ab_re)], [jnp.zeros_like(ab_re)]
    for _ in range(s):
        qr, qi = qr + [qr[-1] * ab_re - qi[-1] * ab_im], qi + [qr[-1] * ab_im + qi[-1] * ab_re]
    p_re, p_im = jnp.stack(qr), jnp.stack(qi)

    d_re = jnp.stack([qr[s - 1 - k] for k in range(s)])[..., None]
    d_im = jnp.stack([qi[s - 1 - k] for k in range(s)])[..., None]

    def spread(compact, rows_group, cols_per_group, col_blocks):
        width = col_blocks * gb * cols_per_group
        col = jnp.arange(width)
        src = (col // (gb * cols_per_group)) * cols_per_group + col % cols_per_group
        tile = (jnp.arange(col_blocks * cols_per_group)[:, None] == src[None, :]).astype(F32)
        mask = (rows_group[:, None] == ((col // cols_per_group) % gb)[None, :]).astype(F32)
        return jnp.einsum("brm,mc->brc", compact, tile) * mask

    in_rows = (jnp.arange(s * LANES) // S5_GROUP) % gb
    st_rows = (jnp.arange(2 * S5_BCOLS) // S5_STATE) % gb

    sb = jnp.stack([d_re * bb_re - d_im * bb_im, d_re * bb_im + d_im * bb_re])
    wa_c = sb.reshape(2, s, nb, gb, S5_STATE, S5_GROUP).transpose(2, 1, 3, 5, 0, 4).reshape(
        nb, s * LANES, 2 * S5_STATE)
    wa = spread(wa_c, in_rows, S5_STATE, 2)

    h_re = c_re * p_re[:, :, None, :] - c_im * p_im[:, :, None, :]
    h_im = c_re * p_im[:, :, None, :] + c_im * p_re[:, :, None, :]
    hs = jnp.stack([h_re[1:], -h_im[1:]])
    ws_c = hs.reshape(2, s, nb, gb, S5_GROUP, S5_STATE).transpose(2, 0, 3, 5, 1, 4).reshape(
        nb, 2 * S5_BCOLS, s * S5_GROUP)
    ws = spread(ws_c, st_rows, S5_GROUP, s)
    k_lag = (jnp.einsum("lgon,gnc->lgoc", h_re[:s], bb_re)
             - jnp.einsum("lgon,gnc->lgoc", h_im[:s], bb_im))
    zero = jnp.zeros_like(k_lag[0])
    k_jk = jnp.stack([jnp.stack([k_lag[k - j] if k >= j else zero for k in range(s)])
                      for j in range(s)])
    wl_c = k_jk.reshape(s, s, nb, gb, S5_GROUP, S5_GROUP).transpose(2, 0, 3, 5, 1, 4).reshape(
        nb, s * LANES, s * S5_GROUP)
    wl = spread(wl_c, in_rows, S5_GROUP, s)
    return {
        "wa": wa.astype(BF16),
        "wy": jnp.concatenate([wl, ws], axis=1).astype(BF16),
        "as_re": p_re[s].reshape(1, S5_COLS), "as_im": p_im[s].reshape(1, S5_COLS),
        "ap_re": pr.reshape(1, S5_COLS), "ap_im": pi.reshape(1, S5_COLS),
        "d": d_skip.reshape(1, S5_WIDTH), "w_glu": w_glu.astype(BF16),
        "b_glu": b_glu.reshape(1, S5_WIDTH),
    }


def _mix_kernel(zp_ref, q_ref, k_ref, v_ref, r_ref, g_ref, pw_ref, ps_ref, wa_ref, ba_ref, gnw_ref,
                tri_ref, ypool_ref, ygla_ref, zext, s_ref, *, tm):
    i = pl.program_id(0)

    @pl.when(i == 0)
    def _init():
        zext[0:POOL_HALO, :] = jnp.zeros((POOL_HALO, POOL_WIDTH), F32)
        s_ref[...] = jnp.zeros(s_ref.shape, F32)

    zext[POOL_HALO:POOL_HALO + tm, :] = zp_ref[...]
    pos = (i * tm + 1 + lax.broadcasted_iota(jnp.int32, (tm, 1), 0)).astype(F32)
    for gi, w in enumerate(POOL_WINDOWS):
        cols = slice(gi * POOL_GROUP, (gi + 1) * POOL_GROUP)
        ze = zext[:, cols]
        s, span = ze, 1
        while span < w:
            s = s + pltpu.roll(s, span, axis=0)
            span *= 2
        zc = ze[POOL_HALO:, :]
        pooled = s[POOL_HALO:, :] / jnp.minimum(pos, float(w)) - zc
        mixed = _dot(pooled.astype(BF16), pw_ref[gi]) * ps_ref[:, cols]
        ypool_ref[:, cols] = mixed.astype(BF16)
    zext[0:POOL_HALO, :] = zext[tm:tm + POOL_HALO, :]

    logit = _dot(g_ref[...].astype(BF16), wa_ref[...]) + ba_ref[...]
    la = (jnp.minimum(logit, 0.0) - jnp.log(1.0 + jnp.exp(-jnp.abs(logit)))) / GLA_TAU
    la_hi = la.astype(BF16)
    la_lo = (la - la_hi.astype(F32)).astype(BF16)
    tri = tri_ref[...]
    b = _dot(tri, la_hi) + _dot(tri, la_lo)
    nch = tm // GLA_CHUNK
    b_last = jnp.concatenate(
        [jnp.broadcast_to(b[(c + 1) * GLA_CHUNK - 1:(c + 1) * GLA_CHUNK, :], (GLA_CHUNK, GLA_K_WIDTH))
         for c in range(nch)], axis=0)
    q_dec = (q_ref[...] * (GLA_DK ** -0.5) * jnp.exp(b)).astype(BF16)
    decay = jnp.exp(b_last)
    k_scaled = k_ref[...] * jnp.exp(-b)
    k_dec = k_scaled.astype(BF16)
    k_end = (k_scaled * decay).astype(BF16)
    causal = (lax.broadcasted_iota(jnp.int32, (GLA_CHUNK, GLA_CHUNK), 0)
              >= lax.broadcasted_iota(jnp.int32, (GLA_CHUNK, GLA_CHUNK), 1))
    gnw = gnw_ref[...]
    units = [(hd, c) for hd in range(GLA_HEADS) for c in range(nch)]
    rows = lambda c: slice(c * GLA_CHUNK, (c + 1) * GLA_CHUNK)
    kcol = lambda hd: slice(hd * GLA_DK, (hd + 1) * GLA_DK)
    vcol = lambda hd: slice(hd * GLA_DV, (hd + 1) * GLA_DV)
    vv = {(hd, c): v_ref[rows(c), vcol(hd)].astype(BF16) for hd, c in units}
    scores = {(hd, c): _dot_nt(q_dec[rows(c), kcol(hd)], k_dec[rows(c), kcol(hd)]) for hd, c in units}
    kv = {(hd, c): _dot_tn(vv[hd, c], k_end[rows(c), kcol(hd)]) for hd, c in units}
    st_in = {}
    for hd in range(GLA_HEADS):
        st = s_ref[hd]
        for c in range(nch):
            st_in[hd, c] = st.astype(BF16)
            st = decay[c * GLA_CHUNK:c * GLA_CHUNK + 1, kcol(hd)] * st + kv[hd, c]
        s_ref[hd] = st
    for hd, c in units:
        sc = jnp.where(causal, scores[hd, c], 0.0).astype(BF16)
        o = _dot(sc, vv[hd, c]) + _dot_nt(q_dec[rows(c), kcol(hd)], st_in[hd, c])
        o = _rms(o, gnw)
        rr = r_ref[rows(c), vcol(hd)]
        ygla_ref[rows(c), vcol(hd)] = (o * _silu(rr)).astype(BF16)


def _mixers(z, prm, layer, tm):
    L = z.shape[0]
    row = lambda w, col: pl.BlockSpec((tm, w), lambda i: (i, col // w))
    lb = lambda name: _layer_block(prm[name], layer)
    return pl.pallas_call(
        functools.partial(_mix_kernel, tm=tm),
        grid=(L // tm,),
        in_specs=[row(POOL_WIDTH, Z_POOL), row(GLA_K_WIDTH, Z_Q), row(GLA_K_WIDTH, Z_K),
                  row(GLA_V_WIDTH, Z_V), row(GLA_V_WIDTH, Z_R), row(LANES, Z_G),
                  lb("pool_w"), lb("pool_scale"), lb("w_a2"), lb("b_a"), lb("gla_norm_w"),
                  _resident((tm, tm))],
        out_specs=[pl.BlockSpec((tm, POOL_WIDTH), lambda i: (i, 0)),
                   pl.BlockSpec((tm, GLA_V_WIDTH), lambda i: (i, 0))],
        out_shape=[jax.ShapeDtypeStruct((L, POOL_WIDTH), BF16),
                   jax.ShapeDtypeStruct((L, GLA_V_WIDTH), BF16)],
        scratch_shapes=[pltpu.VMEM((tm + POOL_HALO, POOL_WIDTH), F32),
                        pltpu.VMEM((GLA_HEADS, GLA_DV, GLA_DK), F32)],
        compiler_params=_cparams(1),
        name="mixers",
    )(z, z, z, z, z, z, prm["pool_w"], prm["pool_scale"], prm["w_a2"], prm["b_a"], prm["gla_norm_w"],
      prm["tri"])


def _outproj_kernel(h_ref, ys_ref, yp_ref, yg_ref, w_ref, nw_ref, ride_ref, o_ref, a_ref, cast_ref, m_ref):
    cast_ref[...] = ride_ref[...].astype(BF16)
    m_ref[:, 0:S5_WIDTH] = ys_ref[...]
    m_ref[:, S5_WIDTH:S5_WIDTH + POOL_WIDTH] = yp_ref[...]
    m_ref[:, S5_WIDTH + POOL_WIDTH:] = yg_ref[...]
    m = m_ref[...]
    for c0, c1 in _col_chunks(D_MODEL):
        o_ref[:, c0:c1] = h_ref[:, c0:c1] + _dot(m, w_ref[:, c0:c1])
    a_ref[...] = _rms(o_ref[...], nw_ref[...]).astype(BF16)


def _outproj(h, ys, yp, yg, w, nw, ride, layer, tm):
    L = h.shape[0]
    row = lambda wd: pl.BlockSpec((tm, wd), lambda i: (i, 0))
    r_in, r_out, r_shape = _rider(ride, layer, L // tm)
    return pl.pallas_call(
        _outproj_kernel,
        grid=(L // tm,),
        in_specs=[row(D_MODEL), row(S5_WIDTH), row(POOL_WIDTH), row(GLA_V_WIDTH),
                  pl.BlockSpec((D_MODEL, D_MODEL), lambda i: (0, 0), pipeline_mode=pl.Buffered(1)),
                  pl.BlockSpec((None, 1, D_MODEL), lambda i: (layer, 0, 0)), r_in],
        out_specs=[row(D_MODEL), row(D_MODEL), r_out],
        out_shape=[jax.ShapeDtypeStruct((L, D_MODEL), F32), jax.ShapeDtypeStruct((L, D_MODEL), BF16),
                   r_shape],
        scratch_shapes=[pltpu.VMEM((tm, D_MODEL), BF16)],
        compiler_params=_cparams(1),
        name="outproj",
    )(h, ys, yp, yg, w, nw, ride)


FFN_FC = 512
FFN_NC = D_FF // FFN_FC
FFN_FN = 512
FFN_NN = D_MODEL // FFN_FN
CONV_HALO = SUBLANES


def _serpentine(i, k, n):
    k = jnp.clip(k, 0, n - 1)
    return jnp.where(i % 2 == 0, k, n - 1 - k)


def _ffn_kernel(a_ref, h_ref, wg_ref, wv_ref, cw_ref, cb_ref, wd_ref, o_ref, act_ref, graw, carry, *, tm):
    i, s = pl.program_id(0), pl.program_id(1)

    @pl.when(s < FFN_NC)
    def _up():
        c = _serpentine(i, s, FFN_NC)
        cols = pl.ds(pl.multiple_of(c * FFN_FC, FFN_FC), FFN_FC)
        a = a_ref[...]
        graw[0:CONV_HALO, :] = jnp.where(i > 0, carry[c], 0.0)
        graw[CONV_HALO:CONV_HALO + tm, :] = _dot(a, wg_ref[...])
        carry[c] = graw[tm:tm + CONV_HALO, :]
        cw = cw_ref[:, cols]
        g = graw[...]
        back = lambda k: pltpu.roll(g, k, axis=0)[CONV_HALO:, :]
        gc = cb_ref[:, cols] + back(2) * cw[0:1, :]
        gc = gc + back(1) * cw[1:2, :]
        gc = gc + g[CONV_HALO:, :] * cw[2:3, :]
        act_ref[:, cols] = (_silu(gc) * _dot(a, wv_ref[...])).astype(BF16)

    @pl.when(s >= FFN_NC)
    def _down():
        o_ref[...] = h_ref[...] + _dot(act_ref[...], wd_ref[...])


def _ffn(a, h, w_up, conv_w, conv_b, w_down, layer, tm):
    L = h.shape[0]
    up_c = lambda i, s: _serpentine(i, s, FFN_NC)
    down_n = lambda i, s: _serpentine(i, s - FFN_NC, FFN_NN)
    return pl.pallas_call(
        functools.partial(_ffn_kernel, tm=tm),
        grid=(L // tm, FFN_NC + FFN_NN),
        in_specs=[pl.BlockSpec((tm, D_MODEL), lambda i, s: (i, 0)),
                  pl.BlockSpec((tm, FFN_FN), lambda i, s: (i, down_n(i, s))),
                  pl.BlockSpec((D_MODEL, FFN_FC), lambda i, s: (0, up_c(i, s))),
                  pl.BlockSpec((D_MODEL, FFN_FC), lambda i, s: (0, FFN_NC + up_c(i, s))),
                  pl.BlockSpec((None, 3, D_FF), lambda i, s: (layer, 0, 0)),
                  pl.BlockSpec((None, 1, D_FF), lambda i, s: (layer, 0, 0)),
                  pl.BlockSpec((D_FF, FFN_FN), lambda i, s: (0, down_n(i, s)))],
        out_specs=pl.BlockSpec((tm, FFN_FN), lambda i, s: (i, down_n(i, s))),
        out_shape=jax.ShapeDtypeStruct((L, D_MODEL), F32),
        scratch_shapes=[pltpu.VMEM((tm, D_FF), BF16),
                        pltpu.VMEM((tm + CONV_HALO, FFN_FC), F32),
                        pltpu.VMEM((FFN_NC, CONV_HALO, FFN_FC), F32)],
        compiler_params=_cparams(2),
        name="ffn",
    )(a, h, w_up, w_up, conv_w, conv_b, w_down)


def _ple_kernel(*refs, final):
    if final:
        h_ref, p_ref, nw_ref, wpg_ref, wple_ref, fw_ref, o_ref, a_ref = refs
    else:
        h_ref, p_ref, nw_ref, wpg_ref, wple_ref, o_ref, a_ref = refs
    rinv = _rms_split(h_ref, nw_ref, a_ref, MM_TN)
    a = a_ref[...]
    pe = p_ref[...].astype(BF16)
    for c0, c1 in _col_chunks(D_MODEL):
        gate = _sigmoid(rinv * _dot(a, wpg_ref[:, c0:c1]))
        o_ref[:, c0:c1] = h_ref[:, c0:c1] + _dot(pe, wple_ref[:, c0:c1]) * gate
    if final:
        o_ref[...] = _rms(o_ref[...], fw_ref[...])


def _ple(h, p, nw, w_pg, w_ple, final_w, layer, tm):
    L = h.shape[0]
    final = final_w is not None
    in_specs = [pl.BlockSpec((tm, D_MODEL), lambda i: (i, 0)),
                pl.BlockSpec((None, tm, PLE_DIM), lambda i: (layer, i, 0)),
                pl.BlockSpec((None, 1, D_MODEL), lambda i: (layer, 0, 0)),
                pl.BlockSpec((D_MODEL, D_MODEL), lambda i: (0, 0), pipeline_mode=pl.Buffered(1)),
                pl.BlockSpec((None, PLE_DIM, D_MODEL), lambda i: (layer, 0, 0),
                             pipeline_mode=pl.Buffered(1))]
    args = [h, p, nw, w_pg, w_ple]
    if final:
        in_specs.append(_resident((1, D_MODEL)))
        args.append(final_w)
    return pl.pallas_call(
        functools.partial(_ple_kernel, final=final),
        grid=(L // tm,),
        in_specs=in_specs,
        out_specs=pl.BlockSpec((tm, D_MODEL), lambda i: (i, 0)),
        out_shape=jax.ShapeDtypeStruct((L, D_MODEL), F32),
        scratch_shapes=[pltpu.VMEM((tm, D_MODEL), BF16)],
        compiler_params=_cparams(1),
        name="ple_final" if final else "ple",
    )(*args)


def _chunk_tril(tm):
    r = jnp.arange(tm)
    same = (r[:, None] // GLA_CHUNK) == (r[None, :] // GLA_CHUNK)
    return (same & (r[:, None] >= r[None, :])).astype(BF16)


def kernel(x, p, norm_mix_w, w_in, s5_a_re, s5_a_im, s5_log_dt, s5_b_re, s5_b_im, s5_c_re, s5_c_im, s5_d, s5_w_glu, s5_b_glu, pool_w, pool_scale, gla_w_a2, gla_b_a, gla_norm_w, w_out, norm_ffn_w, w_up, conv_w, conv_b, w_down, norm_ple_w, w_ple, w_pg, final_norm_w):
    bsz, L, _ = x.shape
    tb_s5 = min(S5_TB, L // S5_SEGMENTS)
    assert bsz == 1 and L % (S5_SEGMENTS * tb_s5) == 0 and tb_s5 % S5_STRIDE == 0
    depth = w_in.shape[0]
    tm = min(512, L)
    tm_ffn = min(1024, L)
    tm_mix = min(512, L)
    tri = _chunk_tril(tm_mix)
    nw_ffn = norm_ffn_w.reshape(depth, 1, D_MODEL)
    conv_b3 = conv_b.reshape(depth, 1, D_FF)
    w_in_b = w_in.astype(BF16)
    w_ple_b = w_ple.astype(BF16)
    nw_mix = norm_mix_w.reshape(depth, 1, D_MODEL)
    nw_ple = norm_ple_w.reshape(depth, 1, D_MODEL)
    p3 = p.reshape(depth, L, PLE_DIM)
    s5_prm = jax.vmap(functools.partial(_s5_prepare, seg_len=L // S5_SEGMENTS))(
        s5_a_re, s5_a_im, s5_log_dt, s5_b_re, s5_b_im, s5_c_re, s5_c_im, s5_d, s5_w_glu, s5_b_glu)
    mix_prm = {
        "pool_w": pool_w.astype(BF16), "pool_scale": pool_scale.reshape(depth, 1, POOL_WIDTH),
        "w_a2": jnp.pad(gla_w_a2, ((0, 0), (0, LANES - GLA_GATE_RANK), (0, 0))).astype(BF16),
        "b_a": gla_b_a.reshape(depth, 1, GLA_K_WIDTH), "gla_norm_w": gla_norm_w.reshape(depth, 1, GLA_DV),
        "tri": tri,
    }
    h = x.reshape(L, D_MODEL)
    for i in range(depth):
        z, w_down_b, w_out_b = _inproj(h, nw_mix, w_in_b, w_down, w_out, i, tm)
        y_s5, w_up_b = _s5_mixer(z, s5_prm, w_up, i, tb_s5)
        y_pool, y_gla = _mixers(z, mix_prm, i, tm_mix)
        h, a, w_pg_b = _outproj(h, y_s5, y_pool, y_gla, w_out_b, nw_ffn, w_pg, i, tm)
        h = _ffn(a, h, w_up_b, conv_w, conv_b3, w_down_b, i, tm_ffn)
        final_w = final_norm_w.reshape(1, D_MODEL) if i == depth - 1 else None
        h = _ple(h, p3, nw_ple, w_pg_b, w_ple_b, final_w, i, tm)
    return h.reshape(bsz, L, D_MODEL)
```

```python
import functools
import math

import jax
import jax.numpy as jnp
from jax import lax
from jax.experimental import pallas as pl
from jax.experimental.pallas import tpu as pltpu

F32 = jnp.float32
BF16 = jnp.bfloat16

D_MODEL = 2048
S5_WIDTH = 512
S5_GROUP = 16
S5_GROUPS = 32
S5_STATE = 64
S5_COLS = S5_GROUPS * S5_STATE
POOL_WIDTH = 512
POOL_WINDOWS = (2, 4, 8, 16)
POOL_GROUP = 128
GLA_HEADS = 4
GLA_DK = 128
GLA_DV = 256
GLA_K_WIDTH = 512
GLA_V_WIDTH = 1024
GLA_GATE_RANK = 16
GLA_TAU = 16.0
GLA_CHUNK = 64
D_FF = 5632
PLE_DIM = 256
EPS = 1e-6

LANES = 128
SUBLANES = 8
S5_SEGMENTS = SUBLANES
S5_BUNDLE = LANES // S5_GROUP
S5_NBUNDLES = S5_GROUPS // S5_BUNDLE
S5_BCOLS = S5_BUNDLE * S5_STATE
S5_TB = 128
S5_STRIDE = 4
POOL_HALO = 16

Z_S5, Z_POOL, Z_Q, Z_K, Z_V, Z_R, Z_G = 0, 512, 1024, 1536, 2048, 3072, 4096
Z_WIDTH = 4224
MM_TN = 512

VMEM_LIMIT = 56 * 1024 * 1024


def _cparams(n_axes):
    return pltpu.CompilerParams(dimension_semantics=("arbitrary",) * n_axes,
                                vmem_limit_bytes=VMEM_LIMIT)


def _resident(shape):
    nd = len(shape)
    return pl.BlockSpec(shape, lambda *_: (0,) * nd, pipeline_mode=pl.Buffered(1))


def _layer_block(arr, layer):
    nd = arr.ndim
    return pl.BlockSpec((None,) + arr.shape[1:], lambda *_: (layer,) + (0,) * (nd - 1),
                        pipeline_mode=pl.Buffered(1))


def _rider(w, layer, nsteps):
    slab = w.shape[1] // nsteps
    assert slab * nsteps == w.shape[1] and slab % (2 * SUBLANES) == 0
    return (pl.BlockSpec((None, slab, w.shape[2]), lambda i: (layer, i, 0)),
            pl.BlockSpec((slab, w.shape[2]), lambda i: (i, 0)),
            jax.ShapeDtypeStruct(w.shape[1:], BF16))


def _rms(x, w):
    ms = jnp.mean(x * x, axis=-1, keepdims=True)
    return x * lax.rsqrt(ms + EPS) * w


def _sigmoid(x):
    return 0.5 * (1.0 + jnp.tanh(0.5 * x))


def _silu(x):
    h = 0.5 * x
    return h + h * jnp.tanh(h)


def _dot(a, b):
    return jnp.dot(a, b, preferred_element_type=F32)


def _dot_nt(a, b):
    return lax.dot_general(a, b, (((1,), (1,)), ((), ())), preferred_element_type=F32)


def _dot_tn(a, b):
    return lax.dot_general(a, b, (((0,), (0,)), ((), ())), preferred_element_type=F32)


def _col_chunks(n):
    return [(c0, min(c0 + MM_TN, n)) for c0 in range(0, n, MM_TN)]


MXU_K = 256


def _rms_split(h_ref, w_ref, a_ref, width):
    sq = None
    for k0 in range(0, h_ref.shape[1], MXU_K):
        hk = h_ref[:, k0:k0 + MXU_K]
        a_ref[:, k0:k0 + MXU_K] = (hk * w_ref[:, k0:k0 + MXU_K]).astype(BF16)
        sq = hk * hk if sq is None else sq + hk * hk
    rinv = lax.rsqrt(jnp.sum(sq, axis=-1, keepdims=True) * (1.0 / h_ref.shape[1]) + EPS)
    return jnp.broadcast_to(rinv, (h_ref.shape[0], width))


def _inproj_kernel(h_ref, nw_ref, w_ref, ride1_ref, ride2_ref, z_ref, cast1_ref, cast2_ref, a_ref):
    cast1_ref[...] = ride1_ref[...].astype(BF16)
    cast2_ref[...] = ride2_ref[...].astype(BF16)
    rinv = _rms_split(h_ref, nw_ref, a_ref, MM_TN)
    a = a_ref[...]
    zt = jnp.concatenate([_dot(a, w_ref[:, Z_R + c0:Z_R + c1])
                          for c0, c1 in _col_chunks(w_ref.shape[1] - Z_R)], axis=1)
    z_ref[:, Z_G:] = rinv[:, :LANES] * zt[:, :LANES]
    r = zt[:, GLA_GATE_RANK:GLA_GATE_RANK + GLA_V_WIDTH]
    for c0, c1 in _col_chunks(GLA_V_WIDTH):
        z_ref[:, Z_R + c0:Z_R + c1] = rinv * r[:, c0:c1]
    for c0, c1 in _col_chunks(Z_R):
        z_ref[:, c0:c1] = rinv * _dot(a, w_ref[:, c0:c1])


def _inproj(h, nw, w, ride1, ride2, layer, tm):
    L = h.shape[0]
    r1_in, r1_out, r1_shape = _rider(ride1, layer, L // tm)
    r2_in, r2_out, r2_shape = _rider(ride2, layer, L // tm)
    return pl.pallas_call(
        _inproj_kernel,
        grid=(L // tm,),
        in_specs=[pl.BlockSpec((tm, D_MODEL), lambda i: (i, 0)),
                  pl.BlockSpec((None, 1, D_MODEL), lambda i: (layer, 0, 0)),
                  _layer_block(w, layer), r1_in, r2_in],
        out_specs=[pl.BlockSpec((tm, Z_WIDTH), lambda i: (i, 0)), r1_out, r2_out],
        out_shape=[jax.ShapeDtypeStruct((L, Z_WIDTH), F32), r1_shape, r2_shape],
        scratch_shapes=[pltpu.VMEM((tm, D_MODEL), BF16)],
        compiler_params=_cparams(1),
        name="inproj",
    )(h, nw, w, ride1, ride2)


def _gelu_tanh(x):
    return 0.5 * x * (1.0 + jnp.tanh(math.sqrt(2.0 / math.pi) * (x + 0.044715 * (x * x * x))))


def _s5_kernel(*refs, tb, pass2):
    if pass2:
        (u_ref, wa_ref, are_ref, aim_ref, ride_ref, xe_re_ref, xe_im_ref, ap_re_ref, ap_im_ref,
         wy_ref, d_ref, wglu_ref, bglu_ref, y_ref, cast_ref, ubuf, xb, st_re, st_im) = refs
    else:
        (u_ref, wa_ref, are_ref, aim_ref, xe_re_ref, xe_im_ref, ubuf, st_re, st_im) = refs
    nseg, s = S5_SEGMENTS, S5_STRIDE
    nsub = tb // s
    prow = nsub * nseg
    if pass2:
        cast_ref[...] = ride_ref[...].astype(BF16)

    @pl.when(pl.program_id(0) == 0)
    def _init():
        if pass2:
            apr, api = ap_re_ref[...], ap_im_ref[...]
            st_re[0:1, :] = jnp.zeros((1, S5_COLS), F32)
            st_im[0:1, :] = jnp.zeros((1, S5_COLS), F32)
            for j in range(nseg - 1):
                cr, ci = st_re[j:j + 1, :], st_im[j:j + 1, :]
                st_re[j + 1:j + 2, :] = apr * cr - api * ci + xe_re_ref[j:j + 1, :]
                st_im[j + 1:j + 2, :] = apr * ci + api * cr + xe_im_ref[j:j + 1, :]
        else:
            st_re[...] = jnp.zeros((nseg, S5_COLS), F32)
            st_im[...] = jnp.zeros((nseg, S5_COLS), F32)

    for j in range(nseg):
        uj = u_ref[j]
        for b in range(S5_NBUNDLES):
            ubuf[b, pl.ds(j, tb, stride=nseg), :] = uj[:, b * LANES:(b + 1) * LANES]
    ys = []
    for b in range(S5_NBUNDLES):
        v = ubuf[b].reshape(nsub, s * nseg, LANES)
        lhs = jnp.concatenate([v[:, k * nseg:(k + 1) * nseg, :].reshape(prow, LANES) for k in range(s)],
                              axis=1).astype(BF16)
        bu = _dot(lhs, wa_ref[b])
        cols = slice(b * S5_BCOLS, (b + 1) * S5_BCOLS)
        cre = slice(2 * b * S5_BCOLS, (2 * b + 1) * S5_BCOLS)
        cim = slice((2 * b + 1) * S5_BCOLS, (2 * b + 2) * S5_BCOLS)
        ar = jnp.broadcast_to(are_ref[:, cols], (nseg, S5_BCOLS))
        ai = jnp.broadcast_to(aim_ref[:, cols], (nseg, S5_BCOLS))
        xr, xi = st_re[:, cols], st_im[:, cols]
        for m in range(nsub):
            rows = slice(m * nseg, (m + 1) * nseg)
            if pass2:
                xb[rows, cre] = xr
                xb[rows, cim] = xi
            xr, xi = (ar * xr - ai * xi + bu[rows, :S5_BCOLS], ar * xi + ai * xr + bu[rows, S5_BCOLS:])
        st_re[:, cols] = xr
        st_im[:, cols] = xi
        if pass2:
            xp = xb[:, 2 * b * S5_BCOLS:2 * (b + 1) * S5_BCOLS].astype(BF16)
            yp = _dot(jnp.concatenate([lhs, xp], axis=1), wy_ref[b])
            ys.append(jnp.concatenate(
                [yp[:, k * LANES:(k + 1) * LANES].reshape(nsub, nseg, LANES) for k in range(s)],
                axis=1).reshape(tb * nseg, LANES))

    if not pass2:
        xe_re_ref[...] = st_re[...]
        xe_im_ref[...] = st_im[...]
        return

    u = jnp.concatenate([ubuf[b] for b in range(S5_NBUNDLES)], axis=1)
    y = jnp.concatenate(ys, axis=1) + d_ref[...] * u
    y = _gelu_tanh(y)
    glu = _sigmoid(_dot(y.astype(BF16), wglu_ref[...]) + bglu_ref[...])
    out = y * glu
    for b in range(S5_NBUNDLES):
        ubuf[b] = out[:, b * LANES:(b + 1) * LANES]
    for j in range(nseg):
        y_ref[j] = jnp.concatenate(
            [ubuf[b, pl.ds(j, tb, stride=nseg), :] for b in range(S5_NBUNDLES)], axis=1).astype(BF16)


def _s5_mixer(z, prm, ride, layer, tb):
    L = z.shape[0]
    lb = lambda name: _layer_block(prm[name], layer)
    nseg = S5_SEGMENTS
    seg_len = L // nseg
    nblk = seg_len // tb
    rows = tb * nseg
    r2_in, r2_out, r2_shape = _rider(ride, layer, nblk)
    z3 = z.reshape(nseg, seg_len, Z_WIDTH)
    u_spec = pl.BlockSpec((nseg, tb, S5_WIDTH), lambda i: (0, i, Z_S5 // S5_WIDTH))
    state_shape = jax.ShapeDtypeStruct((nseg, S5_COLS), F32)
    ubuf = pltpu.VMEM((S5_NBUNDLES, rows, LANES), F32)
    state = [pltpu.VMEM((nseg, S5_COLS), F32), pltpu.VMEM((nseg, S5_COLS), F32)]
    common = [u_spec, lb("wa"), lb("as_re"), lb("as_im")]

    xe_re, xe_im = pl.pallas_call(
        functools.partial(_s5_kernel, tb=tb, pass2=False),
        grid=(nblk,),
        in_specs=common,
        out_specs=[_resident_out((nseg, S5_COLS)), _resident_out((nseg, S5_COLS))],
        out_shape=[state_shape, state_shape],
        scratch_shapes=[ubuf] + state,
        compiler_params=_cparams(1),
        name="s5_states",
    )(z3, prm["wa"], prm["as_re"], prm["as_im"])

    y, cast2 = pl.pallas_call(
        functools.partial(_s5_kernel, tb=tb, pass2=True),
        grid=(nblk,),
        in_specs=common + [r2_in, _resident((nseg, S5_COLS)), _resident((nseg, S5_COLS)),
                           lb("ap_re"), lb("ap_im"), lb("wy"), lb("d"), lb("w_glu"), lb("b_glu")],
        out_specs=[pl.BlockSpec((nseg, tb, S5_WIDTH), lambda i: (0, i, 0)), r2_out],
        out_shape=[jax.ShapeDtypeStruct((nseg, seg_len, S5_WIDTH), BF16), r2_shape],
        scratch_shapes=[ubuf, pltpu.VMEM((rows // S5_STRIDE, 2 * S5_COLS), F32)] + state,
        compiler_params=_cparams(1),
        name="s5_outputs",
    )(z3, prm["wa"], prm["as_re"], prm["as_im"], ride, xe_re, xe_im, prm["ap_re"], prm["ap_im"],
      prm["wy"], prm["d"], prm["w_glu"], prm["b_glu"])
    return y.reshape(L, S5_WIDTH), cast2


def _resident_out(shape):
    nd = len(shape)
    return pl.BlockSpec(shape, lambda *_: (0,) * nd)


def _s5_prepare(a_re, a_im, log_dt, b_re, b_im, c_re, c_im, d_skip, w_glu, b_glu, seg_len):
    dt = jnp.exp(log_dt)[:, None]
    mag = jnp.exp(a_re * dt)
    ab_re, ab_im = mag * jnp.cos(a_im * dt), mag * jnp.sin(a_im * dt)
    nr, ni = ab_re - 1.0, ab_im
    den = a_re * a_re + a_im * a_im
    f_re, f_im = (nr * a_re + ni * a_im) / den, (ni * a_re - nr * a_im) / den
    bb_re = f_re[..., None] * b_re - f_im[..., None] * b_im
    bb_im = f_re[..., None] * b_im + f_im[..., None] * b_re
    pr, pi = jnp.ones_like(ab_re), jnp.zeros_like(ab_re)
    sr, si, e = ab_re, ab_im, seg_len
    while e:
        if e & 1:
            pr, pi = pr * sr - pi * si, pr * si + pi * sr
        sr, si = sr * sr - si * si, 2.0 * sr * si
        e >>= 1
    s, nb, gb = S5_STRIDE, S5_NBUNDLES, S5_BUNDLE
    qr, qi = [jnp.ones_like(ab_re)], [jnp.zeros_like(ab_re)]
    for _ in range(s):
        qr, qi = qr + [qr[-1] * ab_re - qi[-1] * ab_im], qi + [qr[-1] * ab_im + qi[-1] * ab_re]
    p_re, p_im = jnp.stack(qr), jnp.stack(qi)

    d_re = jnp.stack([qr[s - 1 - k] for k in range(s)])[..., None]
    d_im = jnp.stack([qi[s - 1 - k] for k in range(s)])[..., None]

    def spread(compact, rows_group, cols_per_group, col_blocks):
        width = col_blocks * gb * cols_per_group
        col = jnp.arange(width)
        src = (col // (gb * cols_per_group)) * cols_per_group + col % cols_per_group
        tile = (jnp.arange(col_blocks * cols_per_group)[:, None] == src[None, :]).astype(F32)
        mask = (rows_group[:, None] == ((col // cols_per_group) % gb)[None, :]).astype(F32)
        return jnp.einsum("brm,mc->brc", compact, tile) * mask

    in_rows = (jnp.arange(s * LANES) // S5_GROUP) % gb
    st_rows = (jnp.arange(2 * S5_BCOLS) // S5_STATE) % gb

    sb = jnp.stack([d_re * bb_re - d_im * bb_im, d_re * bb_im + d_im * bb_re])
    wa_c = sb.reshape(2, s, nb, gb, S5_STATE, S5_GROUP).transpose(2, 1, 3, 5, 0, 4).reshape(
        nb, s * LANES, 2 * S5_STATE)
    wa = spread(wa_c, in_rows, S5_STATE, 2)

    h_re = c_re * p_re[:, :, None, :] - c_im * p_im[:, :, None, :]
    h_im = c_re * p_im[:, :, None, :] + c_im * p_re[:, :, None, :]
    hs = jnp.stack([h_re[1:], -h_im[1:]])
    ws_c = hs.reshape(2, s, nb, gb, S5_GROUP, S5_STATE).transpose(2, 0, 3, 5, 1, 4).reshape(
        nb, 2 * S5_BCOLS, s * S5_GROUP)
    ws = spread(ws_c, st_rows, S5_GROUP, s)
    k_lag = (jnp.einsum("lgon,gnc->lgoc", h_re[:s], bb_re)
             - jnp.einsum("lgon,gnc->lgoc", h_im[:s], bb_im))
    zero = jnp.zeros_like(k_lag[0])
    k_jk = jnp.stack([jnp.stack([k_lag[k - j] if k >= j else zero for k in range(s)])
                      for j in range(s)])
    wl_c = k_jk.reshape(s, s, nb, gb, S5_GROUP, S5_GROUP).transpose(2, 0, 3, 5, 1, 4).reshape(
        nb, s * LANES, s * S5_GROUP)
    wl = spread(wl_c, in_rows, S5_GROUP, s)
    return {
        "wa": wa.astype(BF16),
        "wy": jnp.concatenate([wl, ws], axis=1).astype(BF16),
        "as_re": p_re[s].reshape(1, S5_COLS), "as_im": p_im[s].reshape(1, S5_COLS),
        "ap_re": pr.reshape(1, S5_COLS), "ap_im": pi.reshape(1, S5_COLS),
        "d": d_skip.reshape(1, S5_WIDTH), "w_glu": w_glu.astype(BF16),
        "b_glu": b_glu.reshape(1, S5_WIDTH),
    }


def _mix_kernel(zp_ref, q_ref, k_ref, v_ref, r_ref, g_ref, pw_ref, ps_ref, wa_ref, ba_ref, gnw_ref,
                tri_ref, ypool_ref, ygla_ref, zext, s_ref, *, tm):
    i = pl.program_id(0)

    @pl.when(i == 0)
    def _init():
        zext[0:POOL_HALO, :] = jnp.zeros((POOL_HALO, POOL_WIDTH), F32)
        s_ref[...] = jnp.zeros(s_ref.shape, F32)

    zext[POOL_HALO:POOL_HALO + tm, :] = zp_ref[...]
    pos = (i * tm + 1 + lax.broadcasted_iota(jnp.int32, (tm, 1), 0)).astype(F32)
    for gi, w in enumerate(POOL_WINDOWS):
        cols = slice(gi * POOL_GROUP, (gi + 1) * POOL_GROUP)
        ze = zext[:, cols]
        s, span = ze, 1
        while span < w:
            s = s + pltpu.roll(s, span, axis=0)
            span *= 2
        zc = ze[POOL_HALO:, :]
        pooled = s[POOL_HALO:, :] / jnp.minimum(pos, float(w)) - zc
        mixed = _dot(pooled.astype(BF16), pw_ref[gi]) * ps_ref[:, cols]
        ypool_ref[:, cols] = mixed.astype(BF16)
    zext[0:POOL_HALO, :] = zext[tm:tm + POOL_HALO, :]

    logit = _dot(g_ref[...].astype(BF16), wa_ref[...]) + ba_ref[...]
    la = (jnp.minimum(logit, 0.0) - jnp.log(1.0 + jnp.exp(-jnp.abs(logit)))) / GLA_TAU
    la_hi = la.astype(BF16)
    la_lo = (la - la_hi.astype(F32)).astype(BF16)
    tri = tri_ref[...]
    b = _dot(tri, la_hi) + _dot(tri, la_lo)
    nch = tm // GLA_CHUNK
    b_last = jnp.concatenate(
        [jnp.broadcast_to(b[(c + 1) * GLA_CHUNK - 1:(c + 1) * GLA_CHUNK, :], (GLA_CHUNK, GLA_K_WIDTH))
         for c in range(nch)], axis=0)
    q_dec = (q_ref[...] * (GLA_DK ** -0.5) * jnp.exp(b)).astype(BF16)
    decay = jnp.exp(b_last)
    k_scaled = k_ref[...] * jnp.exp(-b)
    k_dec = k_scaled.astype(BF16)
    k_end = (k_scaled * decay).astype(BF16)
    causal = (lax.broadcasted_iota(jnp.int32, (GLA_CHUNK, GLA_CHUNK), 0)
              >= lax.broadcasted_iota(jnp.int32, (GLA_CHUNK, GLA_CHUNK), 1))
    gnw = gnw_ref[...]
    units = [(hd, c) for hd in range(GLA_HEADS) for c in range(nch)]
    rows = lambda c: slice(c * GLA_CHUNK, (c + 1) * GLA_CHUNK)
    kcol = lambda hd: slice(hd * GLA_DK, (hd + 1) * GLA_DK)
    vcol = lambda hd: slice(hd * GLA_DV, (hd + 1) * GLA_DV)
    vv = {(hd, c): v_ref[rows(c), vcol(hd)].astype(BF16) for hd, c in units}
    scores = {(hd, c): _dot_nt(q_dec[rows(c), kcol(hd)], k_dec[rows(c), kcol(hd)]) for hd, c in units}
    kv = {(hd, c): _dot_tn(vv[hd, c], k_end[rows(c), kcol(hd)]) for hd, c in units}
    st_in = {}
    for hd in range(GLA_HEADS):
        st = s_ref[hd]
        for c in range(nch):
            st_in[hd, c] = st.astype(BF16)
            st = decay[c * GLA_CHUNK:c * GLA_CHUNK + 1, kcol(hd)] * st + kv[hd, c]
        s_ref[hd] = st
    for hd, c in units:
        sc = jnp.where(causal, scores[hd, c], 0.0).astype(BF16)
        o = _dot(sc, vv[hd, c]) + _dot_nt(q_dec[rows(c), kcol(hd)], st_in[hd, c])
        o = _rms(o, gnw)
        rr = r_ref[rows(c), vcol(hd)]
        ygla_ref[rows(c), vcol(hd)] = (o * _silu(rr)).astype(BF16)


def _mixers(z, prm, layer, tm):
    L = z.shape[0]
    row = lambda w, col: pl.BlockSpec((tm, w), lambda i: (i, col // w))
    lb = lambda name: _layer_block(prm[name], layer)
    return pl.pallas_call(
        functools.partial(_mix_kernel, tm=tm),
        grid=(L // tm,),
        in_specs=[row(POOL_WIDTH, Z_POOL), row(GLA_K_WIDTH, Z_Q), row(GLA_K_WIDTH, Z_K),
                  row(GLA_V_WIDTH, Z_V), row(GLA_V_WIDTH, Z_R), row(LANES, Z_G),
                  lb("pool_w"), lb("pool_scale"), lb("w_a2"), lb("b_a"), lb("gla_norm_w"),
                  _resident((tm, tm))],
        out_specs=[pl.BlockSpec((tm, POOL_WIDTH), lambda i: (i, 0)),
                   pl.BlockSpec((tm, GLA_V_WIDTH), lambda i: (i, 0))],
        out_shape=[jax.ShapeDtypeStruct((L, POOL_WIDTH), BF16),
                   jax.ShapeDtypeStruct((L, GLA_V_WIDTH), BF16)],
        scratch_shapes=[pltpu.VMEM((tm + POOL_HALO, POOL_WIDTH), F32),
                        pltpu.VMEM((GLA_HEADS, GLA_DV, GLA_DK), F32)],
        compiler_params=_cparams(1),
        name="mixers",
    )(z, z, z, z, z, z, prm["pool_w"], prm["pool_scale"], prm["w_a2"], prm["b_a"], prm["gla_norm_w"],
      prm["tri"])


def _outproj_kernel(h_ref, ys_ref, yp_ref, yg_ref, w_ref, nw_ref, ride_ref, o_ref, a_ref, cast_ref):
    cast_ref[...] = ride_ref[...].astype(BF16)
    m = jnp.concatenate([ys_ref[...], yp_ref[...], yg_ref[...]], axis=1)
    for c0, c1 in _col_chunks(D_MODEL):
        o_ref[:, c0:c1] = h_ref[:, c0:c1] + _dot(m, w_ref[:, c0:c1])
    a_ref[...] = _rms(o_ref[...], nw_ref[...]).astype(BF16)


def _outproj(h, ys, yp, yg, w, nw, ride, layer, tm):
    L = h.shape[0]
    row = lambda wd: pl.BlockSpec((tm, wd), lambda i: (i, 0))
    r_in, r_out, r_shape = _rider(ride, layer, L // tm)
    return pl.pallas_call(
        _outproj_kernel,
        grid=(L // tm,),
        in_specs=[row(D_MODEL), row(S5_WIDTH), row(POOL_WIDTH), row(GLA_V_WIDTH),
                  pl.BlockSpec((D_MODEL, D_MODEL), lambda i: (0, 0), pipeline_mode=pl.Buffered(1)),
                  pl.BlockSpec((None, 1, D_MODEL), lambda i: (layer, 0, 0)), r_in],
        out_specs=[row(D_MODEL), row(D_MODEL), r_out],
        out_shape=[jax.ShapeDtypeStruct((L, D_MODEL), F32), jax.ShapeDtypeStruct((L, D_MODEL), BF16),
                   r_shape],
        compiler_params=_cparams(1),
        name="outproj",
    )(h, ys, yp, yg, w, nw, ride)


FFN_FC = 512
FFN_NC = D_FF // FFN_FC
FFN_FN = 512
FFN_NN = D_MODEL // FFN_FN
CONV_HALO = SUBLANES


def _serpentine(i, k, n):
    k = jnp.clip(k, 0, n - 1)
    return jnp.where(i % 2 == 0, k, n - 1 - k)


def _ffn_kernel(a_ref, h_ref, wg_ref, wv_ref, cw_ref, cb_ref, wd_ref, o_ref, act_ref, graw, carry, *, tm):
    i, s = pl.program_id(0), pl.program_id(1)

    @pl.when(s < FFN_NC)
    def _up():
        c = _serpentine(i, s, FFN_NC)
        cols = pl.ds(pl.multiple_of(c * FFN_FC, FFN_FC), FFN_FC)
        a = a_ref[...]
        graw[0:CONV_HALO, :] = jnp.where(i > 0, carry[c], 0.0)
        graw[CONV_HALO:CONV_HALO + tm, :] = _dot(a, wg_ref[...])
        carry[c] = graw[tm:tm + CONV_HALO, :]
        cw = cw_ref[:, cols]
        g = graw[...]
        back = lambda k: pltpu.roll(g, k, axis=0)[CONV_HALO:, :]
        gc = cb_ref[:, cols] + back(2) * cw[0:1, :]
        gc = gc + back(1) * cw[1:2, :]
        gc = gc + g[CONV_HALO:, :] * cw[2:3, :]
        act_ref[:, cols] = (_silu(gc) * _dot(a, wv_ref[...])).astype(BF16)

    @pl.when(s >= FFN_NC)
    def _down():
        o_ref[...] = h_ref[...] + _dot(act_ref[...], wd_ref[...])


def _ffn(a, h, w_up, conv_w, conv_b, w_down, layer, tm):
    L = h.shape[0]
    up_c = lambda i, s: _serpentine(i, s, FFN_NC)
    down_n = lambda i, s: _serpentine(i, s - FFN_NC, FFN_NN)
    return pl.pallas_call(
        functools.partial(_ffn_kernel, tm=tm),
        grid=(L // tm, FFN_NC + FFN_NN),
        in_specs=[pl.BlockSpec((tm, D_MODEL), lambda i, s: (i, 0)),
                  pl.BlockSpec((tm, FFN_FN), lambda i, s: (i, down_n(i, s))),
                  pl.BlockSpec((D_MODEL, FFN_FC), lambda i, s: (0, up_c(i, s))),
                  pl.BlockSpec((D_MODEL, FFN_FC), lambda i, s: (0, FFN_NC + up_c(i, s))),
                  pl.BlockSpec((None, 3, D_FF), lambda i, s: (layer, 0, 0)),
                  pl.BlockSpec((None, 1, D_FF), lambda i, s: (layer, 0, 0)),
                  pl.BlockSpec((D_FF, FFN_FN), lambda i, s: (0, down_n(i, s)))],
        out_specs=pl.BlockSpec((tm, FFN_FN), lambda i, s: (i, down_n(i, s))),
        out_shape=jax.ShapeDtypeStruct((L, D_MODEL), F32),
        scratch_shapes=[pltpu.VMEM((tm, D_FF), BF16),
                        pltpu.VMEM((tm + CONV_HALO, FFN_FC), F32),
                        pltpu.VMEM((FFN_NC, CONV_HALO, FFN_FC), F32)],
        compiler_params=_cparams(2),
        name="ffn",
    )(a, h, w_up, w_up, conv_w, conv_b, w_down)


def _ple_kernel(*refs, final):
    if final:
        h_ref, p_ref, nw_ref, wpg_ref, wple_ref, fw_ref, o_ref, a_ref = refs
    else:
        h_ref, p_ref, nw_ref, wpg_ref, wple_ref, o_ref, a_ref = refs
    rinv = _rms_split(h_ref, nw_ref, a_ref, MM_TN)
    a = a_ref[...]
    pe = p_ref[...].astype(BF16)
    edges = [0, 512, 1024, 1536, 1792, D_MODEL]
    for c0, c1 in zip(edges[:-1], edges[1:]):
        gate = _sigmoid(rinv[:, :c1 - c0] * _dot(a, wpg_ref[:, c0:c1]))
        o_ref[:, c0:c1] = h_ref[:, c0:c1] + _dot(pe, wple_ref[:, c0:c1]) * gate
    if final:
        o_ref[...] = _rms(o_ref[...], fw_ref[...])


def _ple(h, p, nw, w_pg, w_ple, final_w, layer, tm):
    L = h.shape[0]
    final = final_w is not None
    in_specs = [pl.BlockSpec((tm, D_MODEL), lambda i: (i, 0)),
                pl.BlockSpec((None, tm, PLE_DIM), lambda i: (layer, i, 0)),
                pl.BlockSpec((None, 1, D_MODEL), lambda i: (layer, 0, 0)),
                pl.BlockSpec((D_MODEL, D_MODEL), lambda i: (0, 0), pipeline_mode=pl.Buffered(1)),
                pl.BlockSpec((None, PLE_DIM, D_MODEL), lambda i: (layer, 0, 0),
                             pipeline_mode=pl.Buffered(1))]
    args = [h, p, nw, w_pg, w_ple]
    if final:
        in_specs.append(_resident((1, D_MODEL)))
        args.append(final_w)
    return pl.pallas_call(
        functools.partial(_ple_kernel, final=final),
        grid=(L // tm,),
        in_specs=in_specs,
        out_specs=pl.BlockSpec((tm, D_MODEL), lambda i: (i, 0)),
        out_shape=jax.ShapeDtypeStruct((L, D_MODEL), F32),
        scratch_shapes=[pltpu.VMEM((tm, D_MODEL), BF16)],
        compiler_params=_cparams(1),
        name="ple_final" if final else "ple",
    )(*args)


def _chunk_tril(tm):
    r = jnp.arange(tm)
    same = (r[:, None] // GLA_CHUNK) == (r[None, :] // GLA_CHUNK)
    return (same & (r[:, None] >= r[None, :])).astype(BF16)


def kernel(x, p, norm_mix_w, w_in, s5_a_re, s5_a_im, s5_log_dt, s5_b_re, s5_b_im, s5_c_re, s5_c_im, s5_d, s5_w_glu, s5_b_glu, pool_w, pool_scale, gla_w_a2, gla_b_a, gla_norm_w, w_out, norm_ffn_w, w_up, conv_w, conv_b, w_down, norm_ple_w, w_ple, w_pg, final_norm_w):
    bsz, L, _ = x.shape
    tb_s5 = min(S5_TB, L // S5_SEGMENTS)
    assert bsz == 1 and L % (S5_SEGMENTS * tb_s5) == 0 and tb_s5 % S5_STRIDE == 0
    depth = w_in.shape[0]
    tm = min(512, L)
    tm_ffn = min(1024, L)
    tm_mix = min(512, L)
    tri = _chunk_tril(tm_mix)
    nw_ffn = norm_ffn_w.reshape(depth, 1, D_MODEL)
    conv_b3 = conv_b.reshape(depth, 1, D_FF)
    w_in_b = w_in.astype(BF16)
    w_ple_b = w_ple.astype(BF16)
    nw_mix = norm_mix_w.reshape(depth, 1, D_MODEL)
    nw_ple = norm_ple_w.reshape(depth, 1, D_MODEL)
    p3 = p.reshape(depth, L, PLE_DIM)
    s5_prm = jax.vmap(functools.partial(_s5_prepare, seg_len=L // S5_SEGMENTS))(
        s5_a_re, s5_a_im, s5_log_dt, s5_b_re, s5_b_im, s5_c_re, s5_c_im, s5_d, s5_w_glu, s5_b_glu)
    mix_prm = {
        "pool_w": pool_w.astype(BF16), "pool_scale": pool_scale.reshape(depth, 1, POOL_WIDTH),
        "w_a2": jnp.pad(gla_w_a2, ((0, 0), (0, LANES - GLA_GATE_RANK), (0, 0))).astype(BF16),
        "b_a": gla_b_a.reshape(depth, 1, GLA_K_WIDTH), "gla_norm_w": gla_norm_w.reshape(depth, 1, GLA_DV),
        "tri": tri,
    }
    h = x.reshape(L, D_MODEL)
    for i in range(depth):
        z, w_down_b, w_out_b = _inproj(h, nw_mix, w_in_b, w_down, w_out, i, tm)
        y_s5, w_up_b = _s5_mixer(z, s5_prm, w_up, i, tb_s5)
        y_pool, y_gla = _mixers(z, mix_prm, i, tm_mix)
        h, a, w_pg_b = _outproj(h, y_s5, y_pool, y_gla, w_out_b, nw_ffn, w_pg, i, tm)
        h = _ffn(a, h, w_up_b, conv_w, conv_b3, w_down_b, i, tm_ffn)
        final_w = final_norm_w.reshape(1, D_MODEL) if i == depth - 1 else None
        h = _ple(h, p3, nw_ple, w_pg_b, w_ple_b, final_w, i, tm)
    return h.reshape(bsz, L, D_MODEL)
```

```python
import functools
import math

import jax
import jax.numpy as jnp
from jax import lax
from jax.experimental import pallas as pl
from jax.experimental.pallas import tpu as pltpu

F32 = jnp.float32
BF16 = jnp.bfloat16

D_MODEL = 2048
S5_WIDTH = 512
S5_GROUP = 16
S5_GROUPS = 32
S5_STATE = 64
S5_COLS = S5_GROUPS * S5_STATE
POOL_WIDTH = 512
POOL_WINDOWS = (2, 4, 8, 16)
POOL_GROUP = 128
GLA_HEADS = 4
GLA_DK = 128
GLA_DV = 256
GLA_K_WIDTH = 512
GLA_V_WIDTH = 1024
GLA_GATE_RANK = 16
GLA_TAU = 16.0
GLA_CHUNK = 64
D_FF = 5632
PLE_DIM = 256
EPS = 1e-6

LANES = 128
SUBLANES = 8
S5_SEGMENTS = SUBLANES
S5_BUNDLE = LANES // S5_GROUP
S5_NBUNDLES = S5_GROUPS // S5_BUNDLE
S5_BCOLS = S5_BUNDLE * S5_STATE
S5_TB = 128
S5_STRIDE = 4
POOL_HALO = 16

Z_S5, Z_POOL, Z_Q, Z_K, Z_V, Z_R, Z_G = 0, 512, 1024, 1536, 2048, 3072, 4096
Z_WIDTH = 4224
MM_TN = 512

VMEM_LIMIT = 56 * 1024 * 1024


def _cparams(n_axes):
    return pltpu.CompilerParams(dimension_semantics=("arbitrary",) * n_axes,
                                vmem_limit_bytes=VMEM_LIMIT)


def _resident(shape):
    nd = len(shape)
    return pl.BlockSpec(shape, lambda *_: (0,) * nd, pipeline_mode=pl.Buffered(1))


def _layer_block(arr, layer):
    nd = arr.ndim
    return pl.BlockSpec((None,) + arr.shape[1:], lambda *_: (layer,) + (0,) * (nd - 1),
                        pipeline_mode=pl.Buffered(1))


def _rider(w, layer, nsteps):
    slab = w.shape[1] // nsteps
    assert slab * nsteps == w.shape[1] and slab % (2 * SUBLANES) == 0
    return (pl.BlockSpec((None, slab, w.shape[2]), lambda i: (layer, i, 0)),
            pl.BlockSpec((slab, w.shape[2]), lambda i: (i, 0)),
            jax.ShapeDtypeStruct(w.shape[1:], BF16))


def _rms(x, w):
    ms = jnp.mean(x * x, axis=-1, keepdims=True)
    return x * lax.rsqrt(ms + EPS) * w


def _sigmoid(x):
    return 0.5 * (1.0 + jnp.tanh(0.5 * x))


def _silu(x):
    h = 0.5 * x
    return h + h * jnp.tanh(h)


def _dot(a, b):
    return jnp.dot(a, b, preferred_element_type=F32)


def _dot_nt(a, b):
    return lax.dot_general(a, b, (((1,), (1,)), ((), ())), preferred_element_type=F32)


def _dot_tn(a, b):
    return lax.dot_general(a, b, (((0,), (0,)), ((), ())), preferred_element_type=F32)


def _col_chunks(n):
    return [(c0, min(c0 + MM_TN, n)) for c0 in range(0, n, MM_TN)]


MXU_K = 256


def _rms_split(h_ref, w_ref, a_ref, width):
    sq = None
    for k0 in range(0, h_ref.shape[1], MXU_K):
        hk = h_ref[:, k0:k0 + MXU_K]
        a_ref[:, k0:k0 + MXU_K] = (hk * w_ref[:, k0:k0 + MXU_K]).astype(BF16)
        sq = hk * hk if sq is None else sq + hk * hk
    rinv = lax.rsqrt(jnp.sum(sq, axis=-1, keepdims=True) * (1.0 / h_ref.shape[1]) + EPS)
    return jnp.broadcast_to(rinv, (h_ref.shape[0], width))


def _inproj_kernel(h_ref, nw_ref, w_ref, ride1_ref, ride2_ref, z_ref, cast1_ref, cast2_ref, a_ref):
    cast1_ref[...] = ride1_ref[...].astype(BF16)
    cast2_ref[...] = ride2_ref[...].astype(BF16)
    rinv = _rms_split(h_ref, nw_ref, a_ref, MM_TN)
    a = a_ref[...]
    zt = jnp.concatenate([_dot(a, w_ref[:, Z_R + c0:Z_R + c1])
                          for c0, c1 in _col_chunks(w_ref.shape[1] - Z_R)], axis=1)
    z_ref[:, Z_G:] = rinv[:, :LANES] * zt[:, :LANES]
    r = zt[:, GLA_GATE_RANK:GLA_GATE_RANK + GLA_V_WIDTH]
    for c0, c1 in _col_chunks(GLA_V_WIDTH):
        z_ref[:, Z_R + c0:Z_R + c1] = rinv * r[:, c0:c1]
    for c0, c1 in _col_chunks(Z_R):
        z_ref[:, c0:c1] = rinv * _dot(a, w_ref[:, c0:c1])


def _inproj(h, nw, w, ride1, ride2, layer, tm):
    L = h.shape[0]
    r1_in, r1_out, r1_shape = _rider(ride1, layer, L // tm)
    r2_in, r2_out, r2_shape = _rider(ride2, layer, L // tm)
    return pl.pallas_call(
        _inproj_kernel,
        grid=(L // tm,),
        in_specs=[pl.BlockSpec((tm, D_MODEL), lambda i: (i, 0)),
                  pl.BlockSpec((None, 1, D_MODEL), lambda i: (layer, 0, 0)),
                  _layer_block(w, layer), r1_in, r2_in],
        out_specs=[pl.BlockSpec((tm, Z_WIDTH), lambda i: (i, 0)), r1_out, r2_out],
        out_shape=[jax.ShapeDtypeStruct((L, Z_WIDTH), F32), r1_shape, r2_shape],
        scratch_shapes=[pltpu.VMEM((tm, D_MODEL), BF16)],
        compiler_params=_cparams(1),
        name="inproj",
    )(h, nw, w, ride1, ride2)


def _gelu_tanh(x):
    return 0.5 * x * (1.0 + jnp.tanh(math.sqrt(2.0 / math.pi) * (x + 0.044715 * (x * x * x))))


def _s5_kernel(*refs, tb, pass2):
    if pass2:
        (u_ref, wa_ref, are_ref, aim_ref, ride_ref, xe_re_ref, xe_im_ref, ap_re_ref, ap_im_ref,
         wy_ref, d_ref, wglu_ref, bglu_ref, y_ref, cast_ref, ubuf, xb, st_re, st_im) = refs
    else:
        (u_ref, wa_ref, are_ref, aim_ref, xe_re_ref, xe_im_ref, ubuf, st_re, st_im) = refs
    nseg, s = S5_SEGMENTS, S5_STRIDE
    nsub = tb // s
    prow = nsub * nseg
    if pass2:
        cast_ref[...] = ride_ref[...].astype(BF16)

    @pl.when(pl.program_id(0) == 0)
    def _init():
        if pass2:
            apr, api = ap_re_ref[...], ap_im_ref[...]
            st_re[0:1, :] = jnp.zeros((1, S5_COLS), F32)
            st_im[0:1, :] = jnp.zeros((1, S5_COLS), F32)
            for j in range(nseg - 1):
                cr, ci = st_re[j:j + 1, :], st_im[j:j + 1, :]
                st_re[j + 1:j + 2, :] = apr * cr - api * ci + xe_re_ref[j:j + 1, :]
                st_im[j + 1:j + 2, :] = apr * ci + api * cr + xe_im_ref[j:j + 1, :]
        else:
            st_re[...] = jnp.zeros((nseg, S5_COLS), F32)
            st_im[...] = jnp.zeros((nseg, S5_COLS), F32)

    for j in range(nseg):
        uj = u_ref[j]
        for b in range(S5_NBUNDLES):
            ubuf[b, pl.ds(j, tb, stride=nseg), :] = uj[:, b * LANES:(b + 1) * LANES]
    ys = []
    for b in range(S5_NBUNDLES):
        v = ubuf[b].reshape(nsub, s * nseg, LANES)
        lhs = jnp.concatenate([v[:, k * nseg:(k + 1) * nseg, :].reshape(prow, LANES) for k in range(s)],
                              axis=1).astype(BF16)
        bu = _dot(lhs, wa_ref[b])
        cols = slice(b * S5_BCOLS, (b + 1) * S5_BCOLS)
        cre = slice(2 * b * S5_BCOLS, (2 * b + 1) * S5_BCOLS)
        cim = slice((2 * b + 1) * S5_BCOLS, (2 * b + 2) * S5_BCOLS)
        ar = jnp.broadcast_to(are_ref[:, cols], (nseg, S5_BCOLS))
        ai = jnp.broadcast_to(aim_ref[:, cols], (nseg, S5_BCOLS))
        xr, xi = st_re[:, cols], st_im[:, cols]
        for m in range(nsub):
            rows = slice(m * nseg, (m + 1) * nseg)
            if pass2:
                xb[rows, cre] = xr
                xb[rows, cim] = xi
            xr, xi = (ar * xr - ai * xi + bu[rows, :S5_BCOLS], ar * xi + ai * xr + bu[rows, S5_BCOLS:])
        st_re[:, cols] = xr
        st_im[:, cols] = xi
        if pass2:
            xp = xb[:, 2 * b * S5_BCOLS:2 * (b + 1) * S5_BCOLS].astype(BF16)
            yp = _dot(jnp.concatenate([lhs, xp], axis=1), wy_ref[b])
            ys.append(jnp.concatenate(
                [yp[:, k * LANES:(k + 1) * LANES].reshape(nsub, nseg, LANES) for k in range(s)],
                axis=1).reshape(tb * nseg, LANES))

    if not pass2:
        xe_re_ref[...] = st_re[...]
        xe_im_ref[...] = st_im[...]
        return

    u = jnp.concatenate([ubuf[b] for b in range(S5_NBUNDLES)], axis=1)
    y = jnp.concatenate(ys, axis=1) + d_ref[...] * u
    y = _gelu_tanh(y)
    glu = _sigmoid(_dot(y.astype(BF16), wglu_ref[...]) + bglu_ref[...])
    out = y * glu
    for b in range(S5_NBUNDLES):
        ubuf[b] = out[:, b * LANES:(b + 1) * LANES]
    for j in range(nseg):
        y_ref[j] = jnp.concatenate(
            [ubuf[b, pl.ds(j, tb, stride=nseg), :] for b in range(S5_NBUNDLES)], axis=1).astype(BF16)


def _s5_mixer(z, prm, ride, layer, tb):
    L = z.shape[0]
    lb = lambda name: _layer_block(prm[name], layer)
    nseg = S5_SEGMENTS
    seg_len = L // nseg
    nblk = seg_len // tb
    rows = tb * nseg
    r2_in, r2_out, r2_shape = _rider(ride, layer, nblk)
    z3 = z.reshape(nseg, seg_len, Z_WIDTH)
    u_spec = pl.BlockSpec((nseg, tb, S5_WIDTH), lambda i: (0, i, Z_S5 // S5_WIDTH))
    state_shape = jax.ShapeDtypeStruct((nseg, S5_COLS), F32)
    ubuf = pltpu.VMEM((S5_NBUNDLES, rows, LANES), F32)
    state = [pltpu.VMEM((nseg, S5_COLS), F32), pltpu.VMEM((nseg, S5_COLS), F32)]
    common = [u_spec, lb("wa"), lb("as_re"), lb("as_im")]

    xe_re, xe_im = pl.pallas_call(
        functools.partial(_s5_kernel, tb=tb, pass2=False),
        grid=(nblk,),
        in_specs=common,
        out_specs=[_resident_out((nseg, S5_COLS)), _resident_out((nseg, S5_COLS))],
        out_shape=[state_shape, state_shape],
        scratch_shapes=[ubuf] + state,
        compiler_params=_cparams(1),
        name="s5_states",
    )(z3, prm["wa"], prm["as_re"], prm["as_im"])

    y, cast2 = pl.pallas_call(
        functools.partial(_s5_kernel, tb=tb, pass2=True),
        grid=(nblk,),
        in_specs=common + [r2_in, _resident((nseg, S5_COLS)), _resident((nseg, S5_COLS)),
                           lb("ap_re"), lb("ap_im"), lb("wy"), lb("d"), lb("w_glu"), lb("b_glu")],
        out_specs=[pl.BlockSpec((nseg, tb, S5_WIDTH), lambda i: (0, i, 0)), r2_out],
        out_shape=[jax.ShapeDtypeStruct((nseg, seg_len, S5_WIDTH), BF16), r2_shape],
        scratch_shapes=[ubuf, pltpu.VMEM((rows // S5_STRIDE, 2 * S5_COLS), F32)] + state,
        compiler_params=_cparams(1),
        name="s5_outputs",
    )(z3, prm["wa"], prm["as_re"], prm["as_im"], ride, xe_re, xe_im, prm["ap_re"], prm["ap_im"],
      prm["wy"], prm["d"], prm["w_glu"], prm["b_glu"])
    return y.reshape(L, S5_WIDTH), cast2


def _resident_out(shape):
    nd = len(shape)
    return pl.BlockSpec(shape, lambda *_: (0,) * nd)


def _s5_prepare(a_re, a_im, log_dt, b_re, b_im, c_re, c_im, d_skip, w_glu, b_glu, seg_len):
    dt = jnp.exp(log_dt)[:, None]
    mag = jnp.exp(a_re * dt)
    ab_re, ab_im = mag * jnp.cos(a_im * dt), mag * jnp.sin(a_im * dt)
    nr, ni = ab_re - 1.0, ab_im
    den = a_re * a_re + a_im * a_im
    f_re, f_im = (nr * a_re + ni * a_im) / den, (ni * a_re - nr * a_im) / den
    bb_re = f_re[..., None] * b_re - f_im[..., None] * b_im
    bb_im = f_re[..., None] * b_im + f_im[..., None] * b_re
    pr, pi = jnp.ones_like(ab_re), jnp.zeros_like(ab_re)
    sr, si, e = ab_re, ab_im, seg_len
    while e:
        if e & 1:
            pr, pi = pr * sr - pi * si, pr * si + pi * sr
        sr, si = sr * sr - si * si, 2.0 * sr * si
        e >>= 1
    s, nb, gb = S5_STRIDE, S5_NBUNDLES, S5_BUNDLE
    qr, qi = [jnp.ones_like(ab_re)], [jnp.zeros_like(ab_re)]
    for _ in range(s):
        qr, qi = qr + [qr[-1] * ab_re - qi[-1] * ab_im], qi + [qr[-1] * ab_im + qi[-1] * ab_re]
    p_re, p_im = jnp.stack(qr), jnp.stack(qi)

    d_re = jnp.stack([qr[s - 1 - k] for k in range(s)])[..., None]
    d_im = jnp.stack([qi[s - 1 - k] for k in range(s)])[..., None]

    def spread(compact, rows_group, cols_per_group, col_blocks):
        width = col_blocks * gb * cols_per_group
        col = jnp.arange(width)
        src = (col // (gb * cols_per_group)) * cols_per_group + col % cols_per_group
        tile = (jnp.arange(col_blocks * cols_per_group)[:, None] == src[None, :]).astype(F32)
        mask = (rows_group[:, None] == ((col // cols_per_group) % gb)[None, :]).astype(F32)
        return jnp.einsum("brm,mc->brc", compact, tile) * mask

    in_rows = (jnp.arange(s * LANES) // S5_GROUP) % gb
    st_rows = (jnp.arange(2 * S5_BCOLS) // S5_STATE) % gb

    sb = jnp.stack([d_re * bb_re - d_im * bb_im, d_re * bb_im + d_im * bb_re])
    wa_c = sb.reshape(2, s, nb, gb, S5_STATE, S5_GROUP).transpose(2, 1, 3, 5, 0, 4).reshape(
        nb, s * LANES, 2 * S5_STATE)
    wa = spread(wa_c, in_rows, S5_STATE, 2)

    h_re = c_re * p_re[:, :, None, :] - c_im * p_im[:, :, None, :]
    h_im = c_re * p_im[:, :, None, :] + c_im * p_re[:, :, None, :]
    hs = jnp.stack([h_re[1:], -h_im[1:]])
    ws_c = hs.reshape(2, s, nb, gb, S5_GROUP, S5_STATE).transpose(2, 0, 3, 5, 1, 4).reshape(
        nb, 2 * S5_BCOLS, s * S5_GROUP)
    ws = spread(ws_c, st_rows, S5_GROUP, s)
    k_lag = (jnp.einsum("lgon,gnc->lgoc", h_re[:s], bb_re)
             - jnp.einsum("lgon,gnc->lgoc", h_im[:s], bb_im))
    zero = jnp.zeros_like(k_lag[0])
    k_jk = jnp.stack([jnp.stack([k_lag[k - j] if k >= j else zero for k in range(s)])
                      for j in range(s)])
    wl_c = k_jk.reshape(s, s, nb, gb, S5_GROUP, S5_GROUP).transpose(2, 0, 3, 5, 1, 4).reshape(
        nb, s * LANES, s * S5_GROUP)
    wl = spread(wl_c, in_rows, S5_GROUP, s)
    return {
        "wa": wa.astype(BF16),
        "wy": jnp.concatenate([wl, ws], axis=1).astype(BF16),
        "as_re": p_re[s].reshape(1, S5_COLS), "as_im": p_im[s].reshape(1, S5_COLS),
        "ap_re": pr.reshape(1, S5_COLS), "ap_im": pi.reshape(1, S5_COLS),
        "d": d_skip.reshape(1, S5_WIDTH), "w_glu": w_glu.astype(BF16),
        "b_glu": b_glu.reshape(1, S5_WIDTH),
    }


def _mix_kernel(zp_ref, q_ref, k_ref, v_ref, r_ref, g_ref, pw_ref, ps_ref, wa_ref, ba_ref, gnw_ref,
                tri_ref, ypool_ref, ygla_ref, zext, s_ref, *, tm):
    i = pl.program_id(0)

    @pl.when(i == 0)
    def _init():
        zext[0:POOL_HALO, :] = jnp.zeros((POOL_HALO, POOL_WIDTH), F32)
        s_ref[...] = jnp.zeros(s_ref.shape, F32)

    zext[POOL_HALO:POOL_HALO + tm, :] = zp_ref[...]
    pos = (i * tm + 1 + lax.broadcasted_iota(jnp.int32, (tm, 1), 0)).astype(F32)
    for gi, w in enumerate(POOL_WINDOWS):
        cols = slice(gi * POOL_GROUP, (gi + 1) * POOL_GROUP)
        ze = zext[:, cols]
        s, span = ze, 1
        while span < w:
            s = s + pltpu.roll(s, span, axis=0)
            span *= 2
        zc = ze[POOL_HALO:, :]
        pooled = s[POOL_HALO:, :] / jnp.minimum(pos, float(w)) - zc
        mixed = _dot(pooled.astype(BF16), pw_ref[gi]) * ps_ref[:, cols]
        ypool_ref[:, cols] = mixed.astype(BF16)
    zext[0:POOL_HALO, :] = zext[tm:tm + POOL_HALO, :]

    logit = _dot(g_ref[...].astype(BF16), wa_ref[...]) + ba_ref[...]
    la = (jnp.minimum(logit, 0.0) - jnp.log(1.0 + jnp.exp(-jnp.abs(logit)))) / GLA_TAU
    la_hi = la.astype(BF16)
    la_lo = (la - la_hi.astype(F32)).astype(BF16)
    tri = tri_ref[...]
    b = _dot(tri, la_hi) + _dot(tri, la_lo)
    nch = tm // GLA_CHUNK
    b_last = jnp.concatenate(
        [jnp.broadcast_to(b[(c + 1) * GLA_CHUNK - 1:(c + 1) * GLA_CHUNK, :], (GLA_CHUNK, GLA_K_WIDTH))
         for c in range(nch)], axis=0)
    q_dec = (q_ref[...] * (GLA_DK ** -0.5) * jnp.exp(b)).astype(BF16)
    decay = jnp.exp(b_last)
    k_scaled = k_ref[...] * jnp.exp(-b)
    k_dec = k_scaled.astype(BF16)
    k_end = (k_scaled * decay).astype(BF16)
    causal = (lax.broadcasted_iota(jnp.int32, (GLA_CHUNK, GLA_CHUNK), 0)
              >= lax.broadcasted_iota(jnp.int32, (GLA_CHUNK, GLA_CHUNK), 1))
    gnw = gnw_ref[...]
    units = [(hd, c) for hd in range(GLA_HEADS) for c in range(nch)]
    rows = lambda c: slice(c * GLA_CHUNK, (c + 1) * GLA_CHUNK)
    kcol = lambda hd: slice(hd * GLA_DK, (hd + 1) * GLA_DK)
    vcol = lambda hd: slice(hd * GLA_DV, (hd + 1) * GLA_DV)
    vv = {(hd, c): v_ref[rows(c), vcol(hd)].astype(BF16) for hd, c in units}
    scores = {(hd, c): _dot_nt(q_dec[rows(c), kcol(hd)], k_dec[rows(c), kcol(hd)]) for hd, c in units}
    kv = {(hd, c): _dot_tn(vv[hd, c], k_end[rows(c), kcol(hd)]) for hd, c in units}
    st_in = {}
    for hd in range(GLA_HEADS):
        st = s_ref[hd]
        for c in range(nch):
            st_in[hd, c] = st.astype(BF16)
            st = decay[c * GLA_CHUNK:c * GLA_CHUNK + 1, kcol(hd)] * st + kv[hd, c]
        s_ref[hd] = st
    for hd, c in units:
        sc = jnp.where(causal, scores[hd, c], 0.0).astype(BF16)
        o = _dot(sc, vv[hd, c]) + _dot_nt(q_dec[rows(c), kcol(hd)], st_in[hd, c])
        o = _rms(o, gnw)
        rr = r_ref[rows(c), vcol(hd)]
        ygla_ref[rows(c), vcol(hd)] = (o * _silu(rr)).astype(BF16)


def _mixers(z, prm, layer, tm):
    L = z.shape[0]
    row = lambda w, col: pl.BlockSpec((tm, w), lambda i: (i, col // w))
    lb = lambda name: _layer_block(prm[name], layer)
    return pl.pallas_call(
        functools.partial(_mix_kernel, tm=tm),
        grid=(L // tm,),
        in_specs=[row(POOL_WIDTH, Z_POOL), row(GLA_K_WIDTH, Z_Q), row(GLA_K_WIDTH, Z_K),
                  row(GLA_V_WIDTH, Z_V), row(GLA_V_WIDTH, Z_R), row(LANES, Z_G),
                  lb("pool_w"), lb("pool_scale"), lb("w_a2"), lb("b_a"), lb("gla_norm_w"),
                  _resident((tm, tm))],
        out_specs=[pl.BlockSpec((tm, POOL_WIDTH), lambda i: (i, 0)),
                   pl.BlockSpec((tm, GLA_V_WIDTH), lambda i: (i, 0))],
        out_shape=[jax.ShapeDtypeStruct((L, POOL_WIDTH), BF16),
                   jax.ShapeDtypeStruct((L, GLA_V_WIDTH), BF16)],
        scratch_shapes=[pltpu.VMEM((tm + POOL_HALO, POOL_WIDTH), F32),
                        pltpu.VMEM((GLA_HEADS, GLA_DV, GLA_DK), F32)],
        compiler_params=_cparams(1),
        name="mixers",
    )(z, z, z, z, z, z, prm["pool_w"], prm["pool_scale"], prm["w_a2"], prm["b_a"], prm["gla_norm_w"],
      prm["tri"])


def _outproj_kernel(h_ref, ys_ref, yp_ref, yg_ref, w_ref, nw_ref, ride_ref, o_ref, a_ref, cast_ref, m_ref):
    cast_ref[...] = ride_ref[...].astype(BF16)
    m_ref[:, 0:S5_WIDTH] = ys_ref[...]
    m_ref[:, S5_WIDTH:S5_WIDTH + POOL_WIDTH] = yp_ref[...]
    m_ref[:, S5_WIDTH + POOL_WIDTH:] = yg_ref[...]
    m = m_ref[...]
    for c0, c1 in _col_chunks(D_MODEL):
        o_ref[:, c0:c1] = h_ref[:, c0:c1] + _dot(m, w_ref[:, c0:c1])
    a_ref[...] = _rms(o_ref[...], nw_ref[...]).astype(BF16)


def _outproj(h, ys, yp, yg, w, nw, ride, layer, tm):
    L = h.shape[0]
    row = lambda wd: pl.BlockSpec((tm, wd), lambda i: (i, 0))
    r_in, r_out, r_shape = _rider(ride, layer, L // tm)
    return pl.pallas_call(
        _outproj_kernel,
        grid=(L // tm,),
        in_specs=[row(D_MODEL), row(S5_WIDTH), row(POOL_WIDTH), row(GLA_V_WIDTH),
                  pl.BlockSpec((D_MODEL, D_MODEL), lambda i: (0, 0), pipeline_mode=pl.Buffered(1)),
                  pl.BlockSpec((None, 1, D_MODEL), lambda i: (layer, 0, 0)), r_in],
        out_specs=[row(D_MODEL), row(D_MODEL), r_out],
        out_shape=[jax.ShapeDtypeStruct((L, D_MODEL), F32), jax.ShapeDtypeStruct((L, D_MODEL), BF16),
                   r_shape],
        scratch_shapes=[pltpu.VMEM((tm, D_MODEL), BF16)],
        compiler_params=_cparams(1),
        name="outproj",
    )(h, ys, yp, yg, w, nw, ride)


FFN_FC = 512
FFN_NC = D_FF // FFN_FC
FFN_FN = 512
FFN_NN = D_MODEL // FFN_FN
CONV_HALO = SUBLANES


def _serpentine(i, k, n):
    k = jnp.clip(k, 0, n - 1)
    return jnp.where(i % 2 == 0, k, n - 1 - k)


def _ffn_kernel(a_ref, h_ref, wg_ref, wv_ref, cw_ref, cb_ref, wd_ref, o_ref, act_ref, graw, carry, *, tm):
    i, s = pl.program_id(0), pl.program_id(1)

    @pl.when(s < FFN_NC)
    def _up():
        c = _serpentine(i, s, FFN_NC)
        cols = pl.ds(pl.multiple_of(c * FFN_FC, FFN_FC), FFN_FC)
        a = a_ref[...]
        graw[0:CONV_HALO, :] = jnp.where(i > 0, carry[c], 0.0)
        graw[CONV_HALO:CONV_HALO + tm, :] = _dot(a, wg_ref[...])
        carry[c] = graw[tm:tm + CONV_HALO, :]
        cw = cw_ref[:, cols]
        g = graw[...]
        back = lambda k: pltpu.roll(g, k, axis=0)[CONV_HALO:, :]
        gc = cb_ref[:, cols] + back(2) * cw[0:1, :]
        gc = gc + back(1) * cw[1:2, :]
        gc = gc + g[CONV_HALO:, :] * cw[2:3, :]
        act_ref[:, cols] = (gc * _sigmoid(gc) * _dot(a, wv_ref[...])).astype(BF16)

    @pl.when(s >= FFN_NC)
    def _down():
        o_ref[...] = h_ref[...] + _dot(act_ref[...], wd_ref[...])


def _ffn(a, h, w_up, conv_w, conv_b, w_down, layer, tm):
    L = h.shape[0]
    up_c = lambda i, s: _serpentine(i, s, FFN_NC)
    down_n = lambda i, s: _serpentine(i, s - FFN_NC, FFN_NN)
    return pl.pallas_call(
        functools.partial(_ffn_kernel, tm=tm),
        grid=(L // tm, FFN_NC + FFN_NN),
        in_specs=[pl.BlockSpec((tm, D_MODEL), lambda i, s: (i, 0)),
                  pl.BlockSpec((tm, FFN_FN), lambda i, s: (i, down_n(i, s))),
                  pl.BlockSpec((D_MODEL, FFN_FC), lambda i, s: (0, up_c(i, s))),
                  pl.BlockSpec((D_MODEL, FFN_FC), lambda i, s: (0, FFN_NC + up_c(i, s))),
                  pl.BlockSpec((None, 3, D_FF), lambda i, s: (layer, 0, 0)),
                  pl.BlockSpec((None, 1, D_FF), lambda i, s: (layer, 0, 0)),
                  pl.BlockSpec((D_FF, FFN_FN), lambda i, s: (0, down_n(i, s)))],
        out_specs=pl.BlockSpec((tm, FFN_FN), lambda i, s: (i, down_n(i, s))),
        out_shape=jax.ShapeDtypeStruct((L, D_MODEL), F32),
        scratch_shapes=[pltpu.VMEM((tm, D_FF), BF16),
                        pltpu.VMEM((tm + CONV_HALO, FFN_FC), F32),
                        pltpu.VMEM((FFN_NC, CONV_HALO, FFN_FC), F32)],
        compiler_params=_cparams(2),
        name="ffn",
    )(a, h, w_up, w_up, conv_w, conv_b, w_down)


def _ple_kernel(*refs, final):
    if final:
        h_ref, p_ref, nw_ref, wpg_ref, wple_ref, fw_ref, o_ref, a_ref = refs
    else:
        h_ref, p_ref, nw_ref, wpg_ref, wple_ref, o_ref, a_ref = refs
    rinv = _rms_split(h_ref, nw_ref, a_ref, MM_TN)
    a = a_ref[...]
    pe = p_ref[...].astype(BF16)
    edges = [0, 512, 1024, 1536, 1792, D_MODEL]
    for c0, c1 in zip(edges[:-1], edges[1:]):
        gate = _sigmoid(rinv[:, :c1 - c0] * _dot(a, wpg_ref[:, c0:c1]))
        o_ref[:, c0:c1] = h_ref[:, c0:c1] + _dot(pe, wple_ref[:, c0:c1]) * gate
    if final:
        o_ref[...] = _rms(o_ref[...], fw_ref[...])


def _ple(h, p, nw, w_pg, w_ple, final_w, layer, tm):
    L = h.shape[0]
    final = final_w is not None
    in_specs = [pl.BlockSpec((tm, D_MODEL), lambda i: (i, 0)),
                pl.BlockSpec((None, tm, PLE_DIM), lambda i: (layer, i, 0)),
                pl.BlockSpec((None, 1, D_MODEL), lambda i: (layer, 0, 0)),
                pl.BlockSpec((D_MODEL, D_MODEL), lambda i: (0, 0), pipeline_mode=pl.Buffered(1)),
                pl.BlockSpec((None, PLE_DIM, D_MODEL), lambda i: (layer, 0, 0),
                             pipeline_mode=pl.Buffered(1))]
    args = [h, p, nw, w_pg, w_ple]
    if final:
        in_specs.append(_resident((1, D_MODEL)))
        args.append(final_w)
    return pl.pallas_call(
        functools.partial(_ple_kernel, final=final),
        grid=(L // tm,),
        in_specs=in_specs,
        out_specs=pl.BlockSpec((tm, D_MODEL), lambda i: (i, 0)),
        out_shape=jax.ShapeDtypeStruct((L, D_MODEL), F32),
        scratch_shapes=[pltpu.VMEM((tm, D_MODEL), BF16)],
        compiler_params=_cparams(1),
        name="ple_final" if final else "ple",
    )(*args)


def _chunk_tril(tm):
    r = jnp.arange(tm)
    same = (r[:, None] // GLA_CHUNK) == (r[None, :] // GLA_CHUNK)
    return (same & (r[:, None] >= r[None, :])).astype(BF16)


def kernel(x, p, norm_mix_w, w_in, s5_a_re, s5_a_im, s5_log_dt, s5_b_re, s5_b_im, s5_c_re, s5_c_im, s5_d, s5_w_glu, s5_b_glu, pool_w, pool_scale, gla_w_a2, gla_b_a, gla_norm_w, w_out, norm_ffn_w, w_up, conv_w, conv_b, w_down, norm_ple_w, w_ple, w_pg, final_norm_w):
    bsz, L, _ = x.shape
    tb_s5 = min(S5_TB, L // S5_SEGMENTS)
    assert bsz == 1 and L % (S5_SEGMENTS * tb_s5) == 0 and tb_s5 % S5_STRIDE == 0
    depth = w_in.shape[0]
    tm = min(512, L)
    tm_ffn = min(1024, L)
    tm_mix = min(512, L)
    tri = _chunk_tril(tm_mix)
    nw_ffn = norm_ffn_w.reshape(depth, 1, D_MODEL)
    conv_b3 = conv_b.reshape(depth, 1, D_FF)
    w_in_b = w_in.astype(BF16)
    w_ple_b = w_ple.astype(BF16)
    nw_mix = norm_mix_w.reshape(depth, 1, D_MODEL)
    nw_ple = norm_ple_w.reshape(depth, 1, D_MODEL)
    p3 = p.reshape(depth, L, PLE_DIM)
    s5_prm = jax.vmap(functools.partial(_s5_prepare, seg_len=L // S5_SEGMENTS))(
        s5_a_re, s5_a_im, s5_log_dt, s5_b_re, s5_b_im, s5_c_re, s5_c_im, s5_d, s5_w_glu, s5_b_glu)
    mix_prm = {
        "pool_w": pool_w.astype(BF16), "pool_scale": pool_scale.reshape(depth, 1, POOL_WIDTH),
        "w_a2": jnp.pad(gla_w_a2, ((0, 0), (0, LANES - GLA_GATE_RANK), (0, 0))).astype(BF16),
        "b_a": gla_b_a.reshape(depth, 1, GLA_K_WIDTH), "gla_norm_w": gla_norm_w.reshape(depth, 1, GLA_DV),
        "tri": tri,
    }
    h = x.reshape(L, D_MODEL)
    for i in range(depth):
        z, w_down_b, w_out_b = _inproj(h, nw_mix, w_in_b, w_down, w_out, i, tm)
        y_s5, w_up_b = _s5_mixer(z, s5_prm, w_up, i, tb_s5)
        y_pool, y_gla = _mixers(z, mix_prm, i, tm_mix)
        h, a, w_pg_b = _outproj(h, y_s5, y_pool, y_gla, w_out_b, nw_ffn, w_pg, i, tm)
        h = _ffn(a, h, w_up_b, conv_w, conv_b3, w_down_b, i, tm_ffn)
        final_w = final_norm_w.reshape(1, D_MODEL) if i == depth - 1 else None
        h = _ple(h, p3, nw_ple, w_pg_b, w_ple_b, final_w, i, tm)
    return h.reshape(bsz, L, D_MODEL)
```
